```python
import math
import jax
import jax.numpy as jnp
from jax import lax
import numpy as np

D_MODEL = 4096
BATCH = 1
SEQ = 16384
DEPTH = 2
DEC_BATCH = 2
DEC_SEQ = 4096
PAST_LEN = 128

N_MIXERS = 2
N_HGRN_LAYERS = (DEPTH + 1) // 2
N_HYENA_LAYERS = DEPTH // 2
EPS = 1e-6
HG_HEAD_DIM = 128
HG_HEADS = D_MODEL // HG_HEAD_DIM
HG_CHUNK = 64
HG_STREAMS = 5
HY_ORDER = 2
HY_STREAMS = HY_ORDER + 1
HY_EMB_DIM = 33
HY_BANDS = (HY_EMB_DIM - 1) // 2
HY_FILTER_WIDTH = 64
HY_SHORT_CONV = 3
HY_TARGET = 1e-2
HY_FAST_DECAY_PCT = 0.3
HY_SLOW_DECAY_PCT = 1.5
HY_MAX_DECAY = math.log(HY_TARGET) / HY_FAST_DECAY_PCT
HY_MIN_DECAY = math.log(HY_TARGET) / HY_SLOW_DECAY_PCT
D_FF = -(-(8 * D_MODEL) // (3 * 256)) * 256

kernel_name = 'hgrn2_hyena_bidir_encoder'


def _rms_norm(x, gain):
    xf = x.astype(jnp.float32)
    y = xf * lax.rsqrt(jnp.mean(xf * xf, axis=-1, keepdims=True) + EPS)
    return (y * gain.astype(jnp.float32)).astype(x.dtype)


def _swiglu(h, w_gate, w_up, w_down):
    return (jax.nn.silu(h @ w_gate) * (h @ w_up)) @ w_down


def _hgrn2_scan(q, k, v, log_f):
    b_sz, t_len, n_h, d_k = q.shape
    d_v = v.shape[-1]
    n_chunks = t_len // HG_CHUNK

    def to_chunks(a):
        return a.reshape(b_sz, n_chunks, HG_CHUNK, n_h, a.shape[-1]).transpose(1, 0, 2, 3, 4)

    causal = jnp.tril(jnp.ones((HG_CHUNK, HG_CHUNK), dtype=bool))[None, :, :, None, None]

    def step(state, inp):
        qc, kc, vc, gc = inp
        cum = jnp.cumsum(gc, axis=1)
        o_inter = jnp.einsum('bthk,bhkv->bthv', qc * jnp.exp(cum), state)
        diff = cum[:, :, None] - cum[:, None, :]
        decay = jnp.where(causal, jnp.exp(jnp.minimum(diff, 0.0)), 0.0)
        scores = jnp.einsum('bthk,bshk,btshk->bhts', qc, kc, decay)
        o_intra = jnp.einsum('bhts,bshv->bthv', scores, vc)
        last = cum[:, -1]
        k_to_end = kc * jnp.exp(last[:, None] - cum)
        state = jnp.exp(last)[..., None] * state + jnp.einsum('bshk,bshv->bhkv', k_to_end, vc)
        return state, o_inter + o_intra

    state0 = jnp.zeros((b_sz, n_h, d_k, d_v), jnp.float32)
    _, out = lax.scan(step, state0, (to_chunks(q), to_chunks(k), to_chunks(v), to_chunks(log_f)))
    return out.transpose(1, 0, 2, 3, 4).reshape(b_sz, t_len, n_h, d_v)


def _hgrn2_mixer(h, w_in, lb, out_norm, w_out):
    b_sz, t_len, _ = h.shape
    proj = (h @ w_in).astype(jnp.float32)
    q, fl_fwd, fl_bwd, inp, gate = jnp.split(proj, HG_STREAMS, axis=-1)

    def heads(a):
        return a.reshape(b_sz, t_len, HG_HEADS, HG_HEAD_DIM)

    def rev(a):
        return jnp.flip(a, axis=1)

    q = jax.nn.silu(heads(q))
    v = heads(inp)
    lb = lb.astype(jnp.float32).reshape(2, HG_HEADS, HG_HEAD_DIM)

    def forget(logits, bound):
        f = bound + (1.0 - bound) * jax.nn.sigmoid(heads(logits))
        return 1.0 - f, jnp.log(f)

    k_f, lf_f = forget(fl_fwd, lb[0])
    k_b, lf_b = forget(fl_bwd, lb[1])
    o = _hgrn2_scan(q, k_f, v, lf_f) + rev(_hgrn2_scan(rev(q), rev(k_b), rev(v), rev(lf_b)))
    o = _rms_norm(o, out_norm.reshape(HG_HEADS, HG_HEAD_DIM)).reshape(b_sz, t_len, D_MODEL)
    o = o * jax.nn.silu(gate)
    return o.astype(h.dtype) @ w_out


def _hyena_mixer(h, w_in, conv_w, conv_b, fc1_w, fc1_b, fc2_w, fc2_b, fc3_w, fc3_b,
                 fc4_w, sin_freq, skip, w_out):
    b_sz, length, _ = h.shape
    f32 = jnp.float32
    u = h @ w_in
    up = jnp.pad(u, ((0, 0), (1, 1), (0, 0)))
    u = up[:, :-2] * conv_w[0] + up[:, 1:-1] * conv_w[1] + up[:, 2:] * conv_w[2] + conv_b
    v, x1, x2 = jnp.split(u.astype(f32), HY_STREAMS, axis=-1)
    t = jnp.linspace(0.0, 1.0, length, dtype=f32)[:, None]
    bands = jnp.linspace(1e-4, HY_BANDS - 1, HY_BANDS, dtype=f32)[None, :]
    ang = (2.0 * math.pi / length) * jnp.arange(length, dtype=f32)[:, None] * bands
    pos = jnp.concatenate([t, jnp.cos(ang), -jnp.sin(ang)], axis=-1)
    freq = sin_freq.astype(f32)
    z = jnp.sin(freq * (pos @ fc1_w.astype(f32) + fc1_b.astype(f32)))
    z = jnp.sin(freq * (z @ fc2_w.astype(f32) + fc2_b.astype(f32)))
    z = jnp.sin(freq * (z @ fc3_w.astype(f32) + fc3_b.astype(f32)))
    deltas = jnp.abs(jnp.linspace(HY_MIN_DECAY, HY_MAX_DECAY, D_MODEL, dtype=f32))
    window = jnp.exp(-t * deltas)[:, None, :]
    fc4 = fc4_w.astype(f32).reshape(HY_FILTER_WIDTH, HY_ORDER, 2 * D_MODEL)
    n_fft = 2 * length
    y = v
    for n, g in enumerate((x1, x2)):
        filt = (z @ fc4[:, n]).reshape(length, 2, D_MODEL) * window
        kern = jnp.concatenate([filt[:, 0], jnp.zeros((1, D_MODEL), f32), filt[:0:-1, 1]], axis=0)
        kern = kern / (jnp.sum(jnp.abs(kern), axis=0, keepdims=True) + EPS)
        spec = jnp.fft.rfft(y, n=n_fft, axis=1) * jnp.fft.rfft(kern, axis=0)[None]
        conv = jnp.fft.irfft(spec, n=n_fft, axis=1)[:, :length]
        y = g * (conv + y * skip[n].astype(f32))
    return y.astype(h.dtype) @ w_out


def _trunk(x, norm_mix, norm_ffn, norm_final, hg_w_in, hg_lb_logits, hg_out_norm, hg_w_out,
           hy_w_in, hy_conv_w, hy_conv_b, hy_fc1_w, hy_fc1_b, hy_fc2_w, hy_fc2_b, hy_fc3_w,
           hy_fc3_b, hy_fc4_w, hy_sin_freq, hy_skip, hy_w_out, ffn_w_gate, ffn_w_up, ffn_w_down):
    lb_all = jnp.cumsum(jax.nn.softmax(hg_lb_logits.astype(jnp.float32), axis=1), axis=1)
    for layer in range(DEPTH):
        slot = layer // N_MIXERS
        h = _rms_norm(x, norm_mix[layer])
        if layer % N_MIXERS == 0:
            x = x + _hgrn2_mixer(h, hg_w_in[slot], lb_all[:, slot], hg_out_norm[slot], hg_w_out[slot])
        else:
            x = x + _hyena_mixer(h, hy_w_in[slot], hy_conv_w[slot], hy_conv_b[slot],
                                 hy_fc1_w[slot], hy_fc1_b[slot], hy_fc2_w[slot], hy_fc2_b[slot],
                                 hy_fc3_w[slot], hy_fc3_b[slot], hy_fc4_w[slot], hy_sin_freq[slot],
                                 hy_skip[slot], hy_w_out[slot])
        x = x + _swiglu(_rms_norm(x, norm_ffn[layer]), ffn_w_gate[layer], ffn_w_up[layer], ffn_w_down[layer])
    return _rms_norm(x, norm_final)


def setup_inputs(seed: int = 0) -> dict:
    key = jax.random.key(seed)
    k = jax.random.split(key, 25)

    def nrm(kk, shape, scale):
        return scale * jax.random.normal(kk, shape, jnp.float32)

    d = D_MODEL
    na, nb = N_HGRN_LAYERS, N_HYENA_LAYERS
    wf = HY_FILTER_WIDTH
    return {
        'x_prompt': nrm(k[0], (BATCH, SEQ, d), 1.0),
        'x_sample': nrm(k[1], (DEC_BATCH, DEC_SEQ, d), 1.0),
        'norm_mix': 1.0 + nrm(k[2], (DEPTH, d), 0.02),
        'norm_ffn': 1.0 + nrm(k[3], (DEPTH, d), 0.02),
        'norm_final': 1.0 + nrm(k[4], (d,), 0.02),
        'hg_w_in': nrm(k[5], (na, d, HG_STREAMS * d), d ** -0.5),
        'hg_lb_logits': nrm(k[6], (2, na + 1, d), 0.5),
        'hg_out_norm': 1.0 + nrm(k[7], (na, d), 0.02),
        'hg_w_out': nrm(k[8], (na, d, d), d ** -0.5),
        'hy_w_in': nrm(k[9], (nb, d, HY_STREAMS * d), d ** -0.5),
        'hy_conv_w': nrm(k[10], (nb, HY_SHORT_CONV, HY_STREAMS * d), HY_SHORT_CONV ** -0.5),
        'hy_conv_b': nrm(k[11], (nb, HY_STREAMS * d), 0.02),
        'hy_fc1_w': nrm(k[12], (nb, HY_EMB_DIM, wf), HY_EMB_DIM ** -0.5),
        'hy_fc1_b': nrm(k[13], (nb, wf), 0.02),
        'hy_fc2_w': nrm(k[14], (nb, wf, wf), wf ** -0.5),
        'hy_fc2_b': nrm(k[15], (nb, wf), 0.02),
        'hy_fc3_w': nrm(k[16], (nb, wf, wf), wf ** -0.5),
        'hy_fc3_b': nrm(k[17], (nb, wf), 0.02),
        'hy_fc4_w': nrm(k[18], (nb, wf, HY_ORDER * 2 * d), wf ** -0.5),
        'hy_sin_freq': 1.0 + nrm(k[19], (nb, wf), 0.1),
        'hy_skip': nrm(k[20], (nb, HY_ORDER, d), 1.0),
        'hy_w_out': nrm(k[21], (nb, d, d), d ** -0.5),
        'ffn_w_gate': nrm(k[22], (DEPTH, d, D_FF), d ** -0.5),
        'ffn_w_up': nrm(k[23], (DEPTH, d, D_FF), d ** -0.5),
        'ffn_w_down': nrm(k[24], (DEPTH, D_FF, d), D_FF ** -0.5),
    }


def reference(x_prompt, x_sample, norm_mix, norm_ffn, norm_final, hg_w_in, hg_lb_logits,
              hg_out_norm, hg_w_out, hy_w_in, hy_conv_w, hy_conv_b, hy_fc1_w, hy_fc1_b,
              hy_fc2_w, hy_fc2_b, hy_fc3_w, hy_fc3_b, hy_fc4_w, hy_sin_freq, hy_skip, hy_w_out,
              ffn_w_gate, ffn_w_up, ffn_w_down):
    params = (norm_mix, norm_ffn, norm_final, hg_w_in, hg_lb_logits, hg_out_norm, hg_w_out,
              hy_w_in, hy_conv_w, hy_conv_b, hy_fc1_w, hy_fc1_b, hy_fc2_w, hy_fc2_b, hy_fc3_w,
              hy_fc3_b, hy_fc4_w, hy_sin_freq, hy_skip, hy_w_out, ffn_w_gate, ffn_w_up, ffn_w_down)
    y_prompt = _trunk(x_prompt, *params)
    y_sample = _trunk(x_sample, *params)
    return (y_prompt, y_sample)
```

```python
import functools
import math

import numpy as np
import jax
import jax.numpy as jnp
from jax import lax
from jax.experimental import pallas as pl
from jax.experimental.pallas import tpu as pltpu

_F32 = jnp.float32
_BF16 = jnp.bfloat16

_EPS = 1e-6
_HEAD_DIM = 128
_CHUNK = 64
_SUB = 16
_EXP_CLAMP = 80.0
_HY_ORDER = 2
_HY_EMB = 33
_HY_BANDS = (_HY_EMB - 1) // 2
_HY_WIDTH = 64
_HY_MAX_DECAY = math.log(1e-2) / 0.3
_HY_MIN_DECAY = math.log(1e-2) / 1.5
_FF_PAD = 512
_V7X_VMEM_LIMIT = 56 * 1024 * 1024


def _cparams(sem):
    return pltpu.CompilerParams(dimension_semantics=sem, vmem_limit_bytes=_V7X_VMEM_LIMIT)


def _tile(dim, want):
    if dim <= want:
        return dim
    t = want
    while t >= 128:
        if dim % t == 0:
            return t
        t -= 128
    return dim


def _row_tile(rows, want):
    t = min(rows, want)
    while rows % t:
        t //= 2
    return t


def _rmsnorm_body(x_ref, g_ref, o_ref):
    x = x_ref[...]
    ms = jnp.mean(x * x, axis=-1, keepdims=True)
    o_ref[...] = (x * lax.rsqrt(ms + _EPS) * g_ref[...]).astype(o_ref.dtype)


def _rmsnorm(x, gain, out_dtype):
    m, d = x.shape
    tm = _row_tile(m, 512)
    return pl.pallas_call(
        _rmsnorm_body,
        grid=(m // tm,),
        in_specs=[pl.BlockSpec((tm, d), lambda i: (i, 0)),
                  pl.BlockSpec((1, d), lambda i: (0, 0))],
        out_specs=pl.BlockSpec((tm, d), lambda i: (i, 0)),
        out_shape=jax.ShapeDtypeStruct((m, d), out_dtype),
        compiler_params=_cparams(("parallel",)),
        name="rmsnorm",
    )(x, gain.reshape(1, d).astype(_F32))


def _mm_body(*refs, mode):
    if mode == "swiglu":
        a_ref, wg_ref, wu_ref, o_ref = refs
        a = a_ref[...]
        g = jnp.dot(a, wg_ref[...], preferred_element_type=_F32)
        u = jnp.dot(a, wu_ref[...], preferred_element_type=_F32)
        o_ref[...] = (g * jax.nn.sigmoid(g) * u).astype(o_ref.dtype)
    elif mode == "residual":
        a_ref, w_ref, r_ref, o_ref = refs
        acc = jnp.dot(a_ref[...], w_ref[...], preferred_element_type=_F32)
        o_ref[...] = r_ref[...] + acc
    else:
        a_ref, w_ref, o_ref = refs
        o_ref[...] = jnp.dot(a_ref[...], w_ref[...], preferred_element_type=_F32).astype(o_ref.dtype)


def _matmul(a, ws, *, mode, residual=None, out_dtype=_F32, tm=1024, tn=512):
    m, k = a.shape
    n = ws[0].shape[1]
    tm = _row_tile(m, tm)
    tn = _tile(n, tn)
    grid = (n // tn, m // tm)
    in_specs = [pl.BlockSpec((tm, k), lambda j, i: (i, 0))]
    in_specs += [pl.BlockSpec((k, tn), lambda j, i: (0, j)) for _ in ws]
    args = [a, *ws]
    if mode == "residual":
        in_specs.append(pl.BlockSpec((tm, tn), lambda j, i: (i, j)))
        args.append(residual)
    return pl.pallas_call(
        functools.partial(_mm_body, mode=mode),
        grid=grid,
        in_specs=in_specs,
        out_specs=pl.BlockSpec((tm, tn), lambda j, i: (i, j)),
        out_shape=jax.ShapeDtypeStruct((m, n), out_dtype),
        compiler_params=_cparams(("parallel", "parallel")),
        name="matmul_" + mode,
    )(*args)


def _scan_body(*refs, reverse, n_chunks, hb, reset_blocks, n_tblocks, slot, epilogue):
    if epilogue:
        q_ref, fl_ref, v_ref, lbl_ref, of_ref, gate_ref, gain_ref, o_ref, st_ref = refs
    else:
        q_ref, fl_ref, v_ref, lbl_ref, o_ref, st_ref = refs
    c_len, hd = _CHUNK, _HEAD_DIM

    t = pl.program_id(1)
    blk = (n_tblocks - 1 - t) if reverse else t
    reset = blk == reset_blocks[0]
    for r in reset_blocks[1:]:
        reset = jnp.logical_or(reset, blk == r)

    @pl.when(reset)
    def _():
        st_ref[...] = jnp.zeros_like(st_ref)

    lbl = lbl_ref[...]
    e = jnp.exp(lbl - jnp.max(lbl, axis=0, keepdims=True))
    lb_all = jnp.sum(e[: slot + 1], axis=0, keepdims=True) / jnp.sum(e, axis=0, keepdims=True)

    row = lax.broadcasted_iota(jnp.int32, (c_len, c_len), 0)
    col = lax.broadcasted_iota(jnp.int32, (c_len, c_len), 1)
    keep = (row <= col) if reverse else (row >= col)
    tri = keep.astype(_F32)
    n_sub = c_len // _SUB

    def chunk_step(ci, carry):
        c = (n_chunks - 1 - ci) if reverse else ci
        r0 = pl.multiple_of(c * c_len, c_len)
        for h in range(hb):
            lanes = slice(h * hd, (h + 1) * hd)
            lb = lb_all[:, lanes]
            qr = q_ref[pl.ds(r0, c_len), lanes]
            fl = fl_ref[pl.ds(r0, c_len), lanes]
            v = v_ref[pl.ds(r0, c_len), lanes]
            q = qr * jax.nn.sigmoid(qr)
            f = lb + (1.0 - lb) * jax.nn.sigmoid(fl)
            k = 1.0 - f
            lf = jnp.log(f)
            cum = jnp.dot(tri, lf, preferred_element_type=_F32, precision=lax.Precision.HIGHEST)
            edge = cum[0:1] if reverse else cum[c_len - 1:c_len]
            st = st_ref[h]
            vb = v.astype(_BF16)
            o = lax.dot_general((q * jnp.exp(cum)).astype(_BF16), st.astype(_BF16),
                                (((1,), (1,)), ((), ())), preferred_element_type=_F32)
            blocks = []
            for i in range(n_sub):
                rows = slice(i * _SUB, (i + 1) * _SUB)
                if reverse:
                    ref_row = cum[(i + 1) * _SUB:(i + 1) * _SUB + 1] if i < n_sub - 1 else 0.0
                else:
                    ref_row = cum[i * _SUB - 1:i * _SUB] if i > 0 else 0.0
                a_i = q[rows] * jnp.exp(cum[rows] - ref_row)
                b_i = k * jnp.exp(jnp.minimum(ref_row - cum, _EXP_CLAMP))
                s_i = lax.dot_general(a_i.astype(_BF16), b_i.astype(_BF16),
                                      (((1,), (1,)), ((), ())), preferred_element_type=_F32)
                blocks.append(jnp.where(keep[rows], s_i, 0.0))
            scores = jnp.concatenate(blocks, axis=0)
            o = o + jnp.dot(scores.astype(_BF16), vb, preferred_element_type=_F32)
            k_end = (k * jnp.exp(edge - cum)).astype(_BF16)
            st_ref[h] = st * jnp.exp(edge) + lax.dot_general(
                vb, k_end, (((0,), (0,)), ((), ())), preferred_element_type=_F32)
            if epilogue:
                tot = of_ref[pl.ds(r0, c_len), lanes] + o
                ms = jnp.mean(tot * tot, axis=-1, keepdims=True)
                y = tot * lax.rsqrt(ms + _EPS) * gain_ref[:, lanes]
                g = gate_ref[pl.ds(r0, c_len), lanes]
                o_ref[pl.ds(r0, c_len), lanes] = (y * (g * jax.nn.sigmoid(g))).astype(o_ref.dtype)
            else:
                o_ref[pl.ds(r0, c_len), lanes] = o
        return carry

    lax.fori_loop(0, n_chunks, chunk_step, 0)


def _hgrn_scan(proj, lb_logits, seg_lens, *, reverse, slot, o_fwd=None, gain=None):
    m, d5 = proj.shape
    d = d5 // 5
    hb = 4 if d % (4 * _HEAD_DIM) == 0 else 1
    w = hb * _HEAD_DIM
    nhb = d // w
    tb = _row_tile(math.gcd(*seg_lens) if len(seg_lens) > 1 else seg_lens[0], 512)
    n_t = m // tb
    bounds = np.cumsum([0] + list(seg_lens)) // tb
    reset_blocks = tuple(int(b) - 1 for b in bounds[1:]) if reverse else tuple(int(b) for b in bounds[:-1])
    epilogue = o_fwd is not None

    def tmap(t):
        return (n_t - 1 - t) if reverse else t

    def stream(s):
        return pl.BlockSpec((tb, w), lambda h, t, s=s: (tmap(t), s * nhb + h))

    n_slots = lb_logits.shape[0]
    in_specs = [stream(0), stream(2 if reverse else 1), stream(3),
                pl.BlockSpec((n_slots, w), lambda h, t: (0, h))]
    args = [proj, proj, proj, lb_logits.astype(_F32)]
    if epilogue:
        in_specs += [pl.BlockSpec((tb, w), lambda h, t: (tmap(t), h)), stream(4),
                     pl.BlockSpec((1, w), lambda h, t: (0, h))]
        args += [o_fwd, proj, gain.reshape(1, d).astype(_F32)]
    body = functools.partial(_scan_body, reverse=reverse, n_chunks=tb // _CHUNK, hb=hb,
                             reset_blocks=reset_blocks, n_tblocks=n_t, slot=slot, epilogue=epilogue)
    return pl.pallas_call(
        body,
        grid=(nhb, n_t),
        in_specs=in_specs,
        out_specs=pl.BlockSpec((tb, w), lambda h, t: (tmap(t), h)),
        out_shape=jax.ShapeDtypeStruct((m, d), _BF16 if epilogue else _F32),
        scratch_shapes=[pltpu.VMEM((hb, _HEAD_DIM, _HEAD_DIM), _F32)],
        compiler_params=_cparams(("parallel", "arbitrary")),
        name="hgrn_scan_bwd" if reverse else "hgrn_scan_fwd",
    )(*args)


def _conv3_body(prev_ref, cur_ref, nxt_ref, w_ref, b_ref, o_ref, *, tm, seg_starts, seg_ends):
    i = pl.program_id(1)
    cur = cur_ref[...]
    rows = lax.broadcasted_iota(jnp.int32, cur.shape, 0)
    g = rows + i * tm
    first = g == seg_starts[0]
    for s in seg_starts[1:]:
        first = jnp.logical_or(first, g == s)
    last = g == seg_ends[0] - 1
    for s in seg_ends[1:]:
        last = jnp.logical_or(last, g == s - 1)
    up = pltpu.roll(cur, 1, 0)
    up = jnp.where(rows == 0, prev_ref[7:8, :], up)
    up = jnp.where(first, 0.0, up)
    dn = pltpu.roll(cur, tm - 1, 0)
    dn = jnp.where(rows == tm - 1, nxt_ref[0:1, :], dn)
    dn = jnp.where(last, 0.0, dn)
    w = w_ref[...]
    o_ref[...] = up * w[0:1] + cur * w[1:2] + dn * w[2:3] + b_ref[...]


def _conv3(u, conv_w, conv_b, seg_lens):
    m, n = u.shape
    tm = _row_tile(math.gcd(*seg_lens) if len(seg_lens) > 1 else seg_lens[0], 512)
    tn = _tile(n, 2048)
    bounds = np.cumsum([0] + list(seg_lens))
    r8 = tm // 8
    n8 = m // 8
    body = functools.partial(_conv3_body, tm=tm, seg_starts=tuple(int(b) for b in bounds[:-1]),
                             seg_ends=tuple(int(b) for b in bounds[1:]))
    return pl.pallas_call(
        body,
        grid=(n // tn, m // tm),
        in_specs=[pl.BlockSpec((8, tn), lambda j, i: (jnp.maximum(i * r8 - 1, 0), j)),
                  pl.BlockSpec((tm, tn), lambda j, i: (i, j)),
                  pl.BlockSpec((8, tn), lambda j, i: (jnp.minimum((i + 1) * r8, n8 - 1), j)),
                  pl.BlockSpec((3, tn), lambda j, i: (0, j)),
                  pl.BlockSpec((1, tn), lambda j, i: (0, j))],
        out_specs=pl.BlockSpec((tm, tn), lambda j, i: (i, j)),
        out_shape=jax.ShapeDtypeStruct((m, n), _F32),
        compiler_params=_cparams(("parallel", "parallel")),
        name="hyena_conv3",
    )(u, u, u, conv_w.astype(_F32), conv_b.reshape(1, n).astype(_F32))


def _fft_plan(seq_len):
    n = 2 * seq_len
    n1 = 256 if n >= 32768 else 128
    while n // n1 < 8:
        n1 //= 2
    return n1, n // n1


def _dft_tables(seq_len):
    n = 2 * seq_len
    n1, n2 = _fft_plan(seq_len)
    h1 = n1 // 2
    i2 = lax.broadcasted_iota(jnp.int32, (n2, n1, n1), 0)
    k1 = lax.broadcasted_iota(jnp.int32, (n2, n1, n1), 1)
    i1 = lax.broadcasted_iota(jnp.int32, (n2, n1, n1), 2)
    phase = (i2 * k1 + n2 * (i1 * k1)) % n
    ang = phase.astype(_F32) * (2.0 * math.pi / n)
    cos, sin = jnp.cos(ang), jnp.sin(ang)
    fwd_a = jnp.concatenate([cos, -sin], axis=1).astype(_BF16)
    cos_t, sin_t = cos.transpose(0, 2, 1)[:, :h1], sin.transpose(0, 2, 1)[:, :h1]
    inv_a = (jnp.concatenate([cos_t, -sin_t], axis=2) * (1.0 / n)).astype(_BF16)
    idx = np.arange(n2)
    ang_b = 2.0 * np.pi * ((idx[:, None] * idx[None, :]) % n2) / n2
    cb, sb = np.cos(ang_b), np.sin(ang_b)
    fwd_b = jnp.asarray(np.block([[cb, sb], [-sb, cb]]), _BF16)
    inv_b = jnp.asarray(np.block([[cb, -sb], [sb, cb]]), _BF16)
    return dict(fwd_a_full=fwd_a, fwd_a_half=fwd_a[:, :, :h1], inv_a=inv_a, fwd_b=fwd_b, inv_b=inv_b)


def _swap_stage_layout(x, groups, rows):
    b, _, _, d = x.shape
    return x.reshape(b, groups, 2, rows, d).transpose(0, 3, 2, 1, 4).reshape(b, rows, 2 * groups, d)


def _stage_a_fwd_body(x_ref, m_ref, o_ref):
    o_ref[...] = jnp.dot(m_ref[...], x_ref[...].astype(_BF16), preferred_element_type=_F32).astype(o_ref.dtype)


def _stage_a_fwd(xp, col_off, mats, d):
    b, n2, h1, _ = xp.shape
    td = _tile(d, 2048)
    off = col_off // td
    return pl.pallas_call(
        _stage_a_fwd_body,
        grid=(b, n2, d // td),
        in_specs=[pl.BlockSpec((None, None, h1, td), lambda bb, j, c: (bb, j, 0, off + c)),
                  pl.BlockSpec((None, 4 * h1, h1), lambda bb, j, c: (j, 0, 0))],
        out_specs=pl.BlockSpec((None, None, 4 * h1, td), lambda bb, j, c: (bb, j, 0, c)),
        out_shape=jax.ShapeDtypeStruct((b, n2, 4 * h1, d), _BF16),
        compiler_params=_cparams(("parallel", "parallel", "parallel")),
        name="hyena_stage_a_fwd",
    )(xp, mats)


def _stage_b_conv_body(g_ref, ks_ref, fb_ref, fbi_ref, o_ref):
    n2 = g_ref.shape[0] // 2
    x = jnp.dot(fb_ref[...], g_ref[...], preferred_element_type=_F32)
    xr, xi = x[:n2], x[n2:]
    kr, ki = ks_ref[:n2, :], ks_ref[n2:, :]
    y = jnp.concatenate([xr * kr - xi * ki, xr * ki + xi * kr], axis=0).astype(_BF16)
    o_ref[...] = jnp.dot(fbi_ref[...], y, preferred_element_type=_F32).astype(o_ref.dtype)


def _stage_b_conv(g, kspec, fwd_b, inv_b):
    b, n1, r, d = g.shape
    td = _tile(d, 2048)
    return pl.pallas_call(
        _stage_b_conv_body,
        grid=(b, n1, d // td),
        in_specs=[pl.BlockSpec((None, None, r, td), lambda bb, k, c: (bb, k, 0, c)),
                  pl.BlockSpec((None, r, td), lambda bb, k, c: (k, 0, c)),
                  pl.BlockSpec((r, r), lambda bb, k, c: (0, 0)),
                  pl.BlockSpec((r, r), lambda bb, k, c: (0, 0))],
        out_specs=pl.BlockSpec((None, None, r, td), lambda bb, k, c: (bb, k, 0, c)),
        out_shape=jax.ShapeDtypeStruct((b, n1, r, d), _BF16),
        compiler_params=_cparams(("parallel", "parallel", "parallel")),
        name="hyena_stage_b_conv",
    )(g, kspec, fwd_b, inv_b)


def _stage_a_inv_body(h_ref, m_ref, y_ref, g_ref, s_ref, o_ref):
    conv = jnp.dot(m_ref[...], h_ref[...], preferred_element_type=_F32)
    o_ref[...] = (g_ref[...] * (conv + y_ref[...] * s_ref[...])).astype(o_ref.dtype)


def _stage_a_inv(ht, mats, yp, y_off, gp, g_off, skip, out_dtype):
    b, n2, r, d = ht.shape
    h1 = r // 4
    td = _tile(d, 2048)
    yo, go = y_off // td, g_off // td
    return pl.pallas_call(
        _stage_a_inv_body,
        grid=(b, n2, d // td),
        in_specs=[pl.BlockSpec((None, None, r, td), lambda bb, j, c: (bb, j, 0, c)),
                  pl.BlockSpec((None, h1, r), lambda bb, j, c: (j, 0, 0)),
                  pl.BlockSpec((None, None, h1, td), lambda bb, j, c: (bb, j, 0, yo + c)),
                  pl.BlockSpec((None, None, h1, td), lambda bb, j, c: (bb, j, 0, go + c)),
                  pl.BlockSpec((1, td), lambda bb, j, c: (0, c))],
        out_specs=pl.BlockSpec((None, None, h1, td), lambda bb, j, c: (bb, j, 0, c)),
        out_shape=jax.ShapeDtypeStruct((b, n2, h1, d), out_dtype),
        compiler_params=_cparams(("parallel", "parallel", "parallel")),
        name="hyena_stage_a_inv",
    )(ht, mats, yp, gp, skip.reshape(1, d).astype(_F32))


def _filter_pos(shape, j0, n1, n2, seq_len):
    r = lax.broadcasted_iota(jnp.int32, shape, 0)
    i1 = jnp.bitwise_and(r, n1 - 1)
    j = j0 + lax.shift_right_logical(r, n1.bit_length() - 1)
    n = n2 * i1 + j
    return jnp.where(i1 < n1 // 2, n, 2 * seq_len - n)


def _filter_mlp_body(band_ref, w1_ref, b1_ref, w2_ref, b2_ref, w3_ref, b3_ref, fr_ref, o_ref,
                     *, jb, n1, n2, seq_len):
    hi = lax.Precision.HIGHEST
    rows = jb * n1
    p = _filter_pos((rows, 128), pl.program_id(0) * jb, n1, n2, seq_len).astype(_F32)
    lane = lax.broadcasted_iota(jnp.int32, (rows, 128), 1)
    ang = (2.0 * math.pi / seq_len) * p * band_ref[...]
    feat = jnp.where(lane == 0, p * (1.0 / (seq_len - 1)),
                     jnp.where(lane <= _HY_BANDS, jnp.cos(ang),
                               jnp.where(lane <= 2 * _HY_BANDS, -jnp.sin(ang), 0.0)))
    fr = fr_ref[...]
    z = jnp.sin(fr * (jnp.dot(feat, w1_ref[...], preferred_element_type=_F32, precision=hi) + b1_ref[...]))
    z = jnp.sin(fr * (jnp.dot(z, w2_ref[...], preferred_element_type=_F32, precision=hi) + b2_ref[...]))
    z = jnp.sin(fr * (jnp.dot(z, w3_ref[...], preferred_element_type=_F32, precision=hi) + b3_ref[...]))
    o_ref[...] = z.reshape(jb, n1, _HY_WIDTH)


def _filter_mlp(seq_len, fc1_w, fc1_b, fc2_w, fc2_b, fc3_w, fc3_b, sin_freq):
    n1, n2 = _fft_plan(seq_len)
    jb = _row_tile(n2, 8)
    wd = _HY_WIDTH
    bands = jnp.linspace(1e-4, _HY_BANDS - 1, _HY_BANDS, dtype=_F32)
    band_row = jnp.zeros((1, 128), _F32).at[0, 1:1 + _HY_BANDS].set(bands).at[0, 1 + _HY_BANDS:_HY_EMB].set(bands)
    w1 = jnp.zeros((128, wd), _F32).at[:_HY_EMB].set(fc1_w.astype(_F32))
    const = lambda shape: pl.BlockSpec(shape, lambda i: (0,) * len(shape))
    row = lambda a: a.reshape(1, wd).astype(_F32)
    body = functools.partial(_filter_mlp_body, jb=jb, n1=n1, n2=n2, seq_len=seq_len)
    return pl.pallas_call(
        body,
        grid=(n2 // jb,),
        in_specs=[const((1, 128)), const((128, wd)), const((1, wd)), const((wd, wd)), const((1, wd)),
                  const((wd, wd)), const((1, wd)), const((1, wd))],
        out_specs=pl.BlockSpec((jb, n1, wd), lambda i: (i, 0, 0)),
        out_shape=jax.ShapeDtypeStruct((n2, n1, wd), _F32),
        compiler_params=_cparams(("parallel",)),
        name="hyena_filter_mlp",
    )(band_row, w1, row(fc1_b), fc2_w.astype(_F32), row(fc2_b), fc3_w.astype(_F32), row(fc3_b), row(sin_freq))


def _kernel_stage_a_body(z_ref, wf_ref, wb_ref, dl_ref, m_ref, o_ref, l1_ref, *, n1, n2, seq_len):
    hi = lax.Precision.HIGHEST
    j = pl.program_id(1)
    h1 = n1 // 2
    td = wf_ref.shape[1]
    p = _filter_pos((n1, td), j, n1, n2, seq_len)
    window = jnp.exp(-(p.astype(_F32) * (1.0 / (seq_len - 1))) * dl_ref[...])
    top = jnp.dot(z_ref[:h1, :], wf_ref[...], preferred_element_type=_F32, precision=hi)
    bot = jnp.dot(z_ref[h1:, :], wb_ref[...], preferred_element_type=_F32, precision=hi)
    kern = jnp.concatenate([top, bot], axis=0) * window
    kern = jnp.where(p == seq_len, 0.0, kern)

    @pl.when(j == 0)
    def _():
        l1_ref[...] = jnp.zeros_like(l1_ref)

    l1_ref[...] += jnp.sum(jnp.abs(kern), axis=0, keepdims=True)
    o_ref[...] = jnp.dot(m_ref[...], kern.astype(_BF16), preferred_element_type=_F32).astype(o_ref.dtype)


def _kernel_stage_a(z, fc4_w, order, deltas, mats, seq_len, d):
    n1, n2 = _fft_plan(seq_len)
    td = _tile(d, 1024)
    nd = d // td
    fwd_off, bwd_off = order * 2 * nd, order * 2 * nd + nd
    body = functools.partial(_kernel_stage_a_body, n1=n1, n2=n2, seq_len=seq_len)
    return pl.pallas_call(
        body,
        grid=(nd, n2),
        in_specs=[pl.BlockSpec((None, n1, _HY_WIDTH), lambda c, j: (j, 0, 0)),
                  pl.BlockSpec((_HY_WIDTH, td), lambda c, j: (0, fwd_off + c)),
                  pl.BlockSpec((_HY_WIDTH, td), lambda c, j: (0, bwd_off + c)),
                  pl.BlockSpec((1, td), lambda c, j: (0, c)),
                  pl.BlockSpec((None, 2 * n1, n1), lambda c, j: (j, 0, 0))],
        out_specs=[pl.BlockSpec((None, 2 * n1, td), lambda c, j: (j, 0, c)),
                   pl.BlockSpec((1, td), lambda c, j: (0, c))],
        out_shape=[jax.ShapeDtypeStruct((n2, 2 * n1, d), _BF16), jax.ShapeDtypeStruct((1, d), _F32)],
        compiler_params=_cparams(("parallel", "arbitrary")),
        name="hyena_kernel_stage_a",
    )(z, fc4_w.astype(_F32), fc4_w.astype(_F32), deltas, mats)


def _kernel_stage_b_body(g_ref, l1_ref, fb_ref, o_ref):
    x = jnp.dot(fb_ref[...], g_ref[...], preferred_element_type=_F32)
    o_ref[...] = x / (l1_ref[...] + _EPS)


def _kernel_stage_b(g, l1, fwd_b):
    n1, r, d = g.shape
    td = _tile(d, 2048)
    return pl.pallas_call(
        _kernel_stage_b_body,
        grid=(n1, d // td),
        in_specs=[pl.BlockSpec((None, r, td), lambda k, c: (k, 0, c)),
                  pl.BlockSpec((1, td), lambda k, c: (0, c)),
                  pl.BlockSpec((r, r), lambda k, c: (0, 0))],
        out_specs=pl.BlockSpec((None, r, td), lambda k, c: (k, 0, c)),
        out_shape=jax.ShapeDtypeStruct((n1, r, d), _F32),
        compiler_params=_cparams(("parallel", "parallel")),
        name="hyena_kernel_stage_b",
    )(g, l1, fwd_b)


def _hyena_long_conv(u2, row0, batch, seq_len, d, hp):
    n1, n2 = _fft_plan(seq_len)
    h1 = n1 // 2
    tabs = _dft_tables(seq_len)
    deltas = jnp.abs(jnp.linspace(_HY_MIN_DECAY, _HY_MAX_DECAY, d, dtype=_F32)).reshape(1, d)
    z = _filter_mlp(seq_len, hp["fc1_w"], hp["fc1_b"], hp["fc2_w"], hp["fc2_b"], hp["fc3_w"], hp["fc3_b"],
                    hp["sin_freq"])
    up = u2[row0:row0 + batch * seq_len].reshape(batch, h1, n2, 3 * d).transpose(0, 2, 1, 3)
    y, y_off = up, 0
    for order in range(_HY_ORDER):
        kg, l1 = _kernel_stage_a(z, hp["fc4_w"], order, deltas, tabs["fwd_a_full"], seq_len, d)
        kspec = _kernel_stage_b(_swap_stage_layout(kg[None], n2, n1)[0], l1, tabs["fwd_b"])
        g = _stage_a_fwd(y, y_off, tabs["fwd_a_half"], d)
        h = _stage_b_conv(_swap_stage_layout(g, n2, n1), kspec, tabs["fwd_b"], tabs["inv_b"])
        last = order == _HY_ORDER - 1
        y = _stage_a_inv(_swap_stage_layout(h, n1, n2), tabs["inv_a"], y, y_off, up, (order + 1) * d,
                         hp["skip"][order], _BF16 if last else _F32)
        y_off = 0
    return y.transpose(0, 2, 1, 3).reshape(batch * seq_len, d)


def _pad_ff(w_gate, w_up, w_down):
    d_ff = w_gate.shape[1]
    pad = (-d_ff) % _FF_PAD
    return (jnp.pad(w_gate, ((0, 0), (0, pad))).astype(_BF16),
            jnp.pad(w_up, ((0, 0), (0, pad))).astype(_BF16),
            jnp.pad(w_down, ((0, pad), (0, 0))).astype(_BF16))


def kernel(x_prompt, x_sample, norm_mix, norm_ffn, norm_final, hg_w_in, hg_lb_logits, hg_out_norm, hg_w_out,
           hy_w_in, hy_conv_w, hy_conv_b, hy_fc1_w, hy_fc1_b, hy_fc2_w, hy_fc2_b, hy_fc3_w, hy_fc3_b, hy_fc4_w,
           hy_sin_freq, hy_skip, hy_w_out, ffn_w_gate, ffn_w_up, ffn_w_down):
    b1, t1, d = x_prompt.shape
    b2, t2, _ = x_sample.shape
    groups = ((b1, t1), (b2, t2))
    seg_lens = tuple([t1] * b1 + [t2] * b2)
    x = jnp.concatenate([x_prompt.reshape(b1 * t1, d), x_sample.reshape(b2 * t2, d)], axis=0)
    depth = norm_mix.shape[0]

    for layer in range(depth):
        slot = layer // 2
        h = _rmsnorm(x, norm_mix[layer], _BF16)
        if layer % 2 == 0:
            proj = _matmul(h, (hg_w_in[slot].astype(_BF16),), mode="plain")
            o_f = _hgrn_scan(proj, hg_lb_logits[0], seg_lens, reverse=False, slot=slot)
            o = _hgrn_scan(proj, hg_lb_logits[1], seg_lens, reverse=True, slot=slot, o_fwd=o_f,
                           gain=hg_out_norm[slot])
            x = _matmul(o, (hg_w_out[slot].astype(_BF16),), mode="residual", residual=x)
        else:
            u = _matmul(h, (hy_w_in[slot].astype(_BF16),), mode="plain")
            u2 = _conv3(u, hy_conv_w[slot], hy_conv_b[slot], seg_lens)
            hp = dict(fc1_w=hy_fc1_w[slot], fc1_b=hy_fc1_b[slot], fc2_w=hy_fc2_w[slot], fc2_b=hy_fc2_b[slot],
                      fc3_w=hy_fc3_w[slot], fc3_b=hy_fc3_b[slot], fc4_w=hy_fc4_w[slot],
                      sin_freq=hy_sin_freq[slot], skip=hy_skip[slot])
            ys, row0 = [], 0
            for bg, tg in groups:
                ys.append(_hyena_long_conv(u2, row0, bg, tg, d, hp))
                row0 += bg * tg
            y = jnp.concatenate(ys, axis=0)
            x = _matmul(y, (hy_w_out[slot].astype(_BF16),), mode="residual", residual=x)
        wg, wu, wd = _pad_ff(ffn_w_gate[layer], ffn_w_up[layer], ffn_w_down[layer])
        h2 = _rmsnorm(x, norm_ffn[layer], _BF16)
        a = _matmul(h2, (wg, wu), mode="swiglu", out_dtype=_BF16)
        x = _matmul(a, (wd,), mode="residual", residual=x, tm=256)

    out = _rmsnorm(x, norm_final, _F32)
    return (out[:b1 * t1].reshape(b1, t1, d), out[b1 * t1:].reshape(b2, t2, d))
```

```python
import functools
import math

import numpy as np
import jax
import jax.numpy as jnp
from jax import lax
from jax.experimental import pallas as pl
from jax.experimental.pallas import tpu as pltpu

_F32 = jnp.float32
_BF16 = jnp.bfloat16

_EPS = 1e-6
_HEAD_DIM = 128
_CHUNK = 64
_SUB = 16
_EXP_CLAMP = 115.0
_HY_ORDER = 2
_HY_EMB = 33
_HY_BANDS = (_HY_EMB - 1) // 2
_HY_WIDTH = 64
_HY_MAX_DECAY = math.log(1e-2) / 0.3
_HY_MIN_DECAY = math.log(1e-2) / 1.5
_V7X_VMEM_LIMIT = 56 * 1024 * 1024


def _cparams(sem):
    return pltpu.CompilerParams(dimension_semantics=sem, vmem_limit_bytes=_V7X_VMEM_LIMIT)


def _tile(dim, want):
    if dim <= want:
        return dim
    t = want
    while t >= 128:
        if dim % t == 0:
            return t
        t -= 128
    return dim


def _row_tile(rows, want):
    t = min(rows, want)
    while rows % t:
        t //= 2
    return t


def _rmsnorm_body(x_ref, g_ref, o_ref):
    x = x_ref[...]
    ms = jnp.mean(x * x, axis=-1, keepdims=True)
    o_ref[...] = (x * lax.rsqrt(ms + _EPS) * g_ref[...]).astype(o_ref.dtype)


def _rmsnorm(x, gain, out_dtype, row0=0, rows=None):
    d = x.shape[1]
    m = x.shape[0] if rows is None else rows
    tm = _row_tile(math.gcd(m, row0) if row0 else m, 512)
    blk0 = row0 // tm
    return pl.pallas_call(
        _rmsnorm_body,
        grid=(m // tm,),
        in_specs=[pl.BlockSpec((tm, d), lambda i: (blk0 + i, 0)),
                  pl.BlockSpec((1, d), lambda i: (0, 0))],
        out_specs=pl.BlockSpec((tm, d), lambda i: (i, 0)),
        out_shape=jax.ShapeDtypeStruct((m, d), out_dtype),
        compiler_params=_cparams(("parallel",)),
        name="rmsnorm",
    )(x, gain.reshape(1, d).astype(_F32))


def _mm_body(*refs, mode, n_w):
    a_ref, w_refs, wb_refs = refs[0], refs[1:1 + n_w], refs[len(refs) - n_w:]

    @pl.when(pl.program_id(1) == 0)
    def _():
        for w_ref, wb_ref in zip(w_refs, wb_refs):
            wb_ref[...] = w_ref[...].astype(_BF16)

    a = a_ref[...]
    if mode == "swiglu":
        o_ref = refs[1 + n_w]
        g = jnp.dot(a, wb_refs[0][...], preferred_element_type=_F32)
        u = jnp.dot(a, wb_refs[1][...], preferred_element_type=_F32)
        o_ref[...] = (g * (0.5 * jnp.tanh(0.5 * g) + 0.5) * u).astype(o_ref.dtype)
    elif mode == "residual":
        r_ref, o_ref = refs[1 + n_w], refs[2 + n_w]
        o_ref[...] = r_ref[...] + jnp.dot(a, wb_refs[0][...], preferred_element_type=_F32)
    else:
        o_ref = refs[1 + n_w]
        o_ref[...] = jnp.dot(a, wb_refs[0][...], preferred_element_type=_F32).astype(o_ref.dtype)


def _matmul(a, ws, layer, *, mode, residual=None, out_dtype=_F32, tm=1024, tn=512, single_buffer_w=False):
    m, k = a.shape
    n = ws[0].shape[2]
    tm = _row_tile(m, tm)
    tn = _tile(n, tn)
    grid = (n // tn, m // tm)
    w_mode = dict(pipeline_mode=pl.Buffered(1)) if single_buffer_w else {}
    in_specs = [pl.BlockSpec((tm, k), lambda j, i: (i, 0))]
    in_specs += [pl.BlockSpec((None, k, tn), lambda j, i: (layer, 0, j), **w_mode) for _ in ws]
    args = [a, *[w.astype(_F32) for w in ws]]
    if mode == "residual":
        in_specs.append(pl.BlockSpec((tm, tn), lambda j, i: (i, j)))
        args.append(residual)
    return pl.pallas_call(
        functools.partial(_mm_body, mode=mode, n_w=len(ws)),
        grid=grid,
        in_specs=in_specs,
        out_specs=pl.BlockSpec((tm, tn), lambda j, i: (i, j)),
        out_shape=jax.ShapeDtypeStruct((m, n), out_dtype),
        scratch_shapes=[pltpu.VMEM((k, tn), _BF16) for _ in ws],
        compiler_params=_cparams(("parallel", "arbitrary")),
        name="matmul_" + mode,
    )(*args)


def _scan_body(*refs, reverse, n_chunks, hb, reset_blocks, n_tblocks, slot, epilogue):
    if epilogue:
        q_ref, fl_ref, v_ref, lbl_ref, of_ref, gate_ref, gain_ref, o_ref, st_ref = refs
    else:
        q_ref, fl_ref, v_ref, lbl_ref, o_ref, st_ref = refs
    c_len, hd = _CHUNK, _HEAD_DIM

    t = pl.program_id(1)
    blk = (n_tblocks - 1 - t) if reverse else t
    reset = blk == reset_blocks[0]
    for r in reset_blocks[1:]:
        reset = jnp.logical_or(reset, blk == r)

    @pl.when(reset)
    def _():
        st_ref[...] = jnp.zeros_like(st_ref)

    lbl = lbl_ref[...]
    e = jnp.exp(lbl - jnp.max(lbl, axis=0, keepdims=True))
    lb_all = jnp.sum(e[: slot + 1], axis=0, keepdims=True) / jnp.sum(e, axis=0, keepdims=True)

    row = lax.broadcasted_iota(jnp.int32, (c_len, c_len), 0)
    col = lax.broadcasted_iota(jnp.int32, (c_len, c_len), 1)
    keep = (row <= col) if reverse else (row >= col)
    tri = keep.astype(_BF16)
    n_sub = c_len // _SUB
    w = hb * hd
    nt = (((1,), (1,)), ((), ()))
    tn = (((0,), (0,)), ((), ()))
    heads = [slice(h * hd, (h + 1) * hd) for h in range(hb)]

    def sigmoid(z):
        return 0.5 * jnp.tanh(0.5 * z) + 0.5

    def pad_rows(part, lo_r, dtype):
        pads = [jnp.zeros((lo_r, part.shape[1]), dtype), part,
                jnp.zeros((c_len - lo_r - part.shape[0], part.shape[1]), dtype)]
        return jnp.concatenate([p for p in pads if p.shape[0]], axis=0)

    def chunk_step(ci, carry):
        c = (n_chunks - 1 - ci) if reverse else ci
        r0 = pl.multiple_of(c * c_len, c_len)
        qr = q_ref[pl.ds(r0, c_len), :]
        q = qr * sigmoid(qr)
        f = lb_all + (1.0 - lb_all) * sigmoid(fl_ref[pl.ds(r0, c_len), :])
        k = 1.0 - f
        vb = v_ref[pl.ds(r0, c_len), :].astype(_BF16)
        lf = jnp.log2(f)
        hi = lf.astype(_BF16)
        rem = lf - hi.astype(_F32)
        mid = rem.astype(_BF16)
        lo = (rem - mid.astype(_F32)).astype(_BF16)
        cum = (jnp.dot(tri, hi, preferred_element_type=_F32) + jnp.dot(tri, mid, preferred_element_type=_F32)
               + jnp.dot(tri, lo, preferred_element_type=_F32))
        edge = cum[0:1] if reverse else cum[c_len - 1:c_len]
        q_in = (q * jnp.exp2(cum)).astype(_BF16)
        k_end = (k * jnp.exp2(edge - cum)).astype(_BF16)
        st_decay = jnp.exp2(edge)
        refs = []
        for i in range(n_sub):
            if reverse:
                refs.append(cum[(i + 1) * _SUB:(i + 1) * _SUB + 1] if i < n_sub - 1 else jnp.zeros((1, w), _F32))
            else:
                refs.append(cum[i * _SUB - 1:i * _SUB] if i > 0 else jnp.zeros((1, w), _F32))
        ref_rows = jnp.concatenate([jnp.broadcast_to(r, (_SUB, w)) for r in refs], axis=0)
        a = (q * jnp.exp2(cum - ref_rows)).astype(_BF16)
        bs = []
        for i in range(n_sub):
            lo_r, hi_r = (i * _SUB, c_len) if reverse else (0, (i + 1) * _SUB)
            part = k[lo_r:hi_r] * jnp.exp2(jnp.minimum(refs[i] - cum[lo_r:hi_r], _EXP_CLAMP))
            bs.append(pad_rows(part.astype(_BF16), lo_r, _BF16))

        sts = [st_ref[h] for h in range(hb)]
        o_inter = [lax.dot_general(q_in[:, ln], sts[h].astype(_BF16), nt, preferred_element_type=_F32)
                   for h, ln in enumerate(heads)]
        a_cat = [jnp.concatenate([pad_rows(a[i * _SUB:(i + 1) * _SUB, ln], i * _SUB, _BF16) for i in range(n_sub)],
                                 axis=1) for ln in heads]
        b_cat = [jnp.concatenate([bs[i][:, ln] for i in range(n_sub)], axis=1) for ln in heads]
        scores = [lax.dot_general(a_cat[h], b_cat[h], nt, preferred_element_type=_F32) for h in range(hb)]
        scores = [jnp.where(keep, s, 0.0).astype(_BF16) for s in scores]
        o_intra = [jnp.dot(scores[h], vb[:, ln], preferred_element_type=_F32) for h, ln in enumerate(heads)]
        st_new = [lax.dot_general(vb[:, ln], k_end[:, ln], tn, preferred_element_type=_F32) for ln in heads]
        for h, ln in enumerate(heads):
            st_ref[h] = sts[h] * st_decay[:, ln] + st_new[h]
        o = jnp.concatenate([o_inter[h] + o_intra[h] for h in range(hb)], axis=1)
        if epilogue:
            tot = of_ref[pl.ds(r0, c_len), :] + o
            ys = []
            for ln in heads:
                th = tot[:, ln]
                ys.append(th * lax.rsqrt(jnp.mean(th * th, axis=-1, keepdims=True) + _EPS))
            g = gate_ref[pl.ds(r0, c_len), :]
            y = jnp.concatenate(ys, axis=1) * gain_ref[...]
            o_ref[pl.ds(r0, c_len), :] = (y * (g * sigmoid(g))).astype(o_ref.dtype)
        else:
            o_ref[pl.ds(r0, c_len), :] = o
        return carry

    lax.fori_loop(0, n_chunks, chunk_step, 0)


def _hgrn_scan(proj, lb_logits, seg_lens, *, reverse, slot, o_fwd=None, gain=None):
    m, d5 = proj.shape
    d = d5 // 5
    hb = next(n for n in (8, 4, 2, 1) if d % (n * _HEAD_DIM) == 0)
    w = hb * _HEAD_DIM
    nhb = d // w
    tb = _row_tile(math.gcd(*seg_lens) if len(seg_lens) > 1 else seg_lens[0], 512)
    n_t = m // tb
    bounds = np.cumsum([0] + list(seg_lens)) // tb
    reset_blocks = tuple(int(b) - 1 for b in bounds[1:]) if reverse else tuple(int(b) for b in bounds[:-1])
    epilogue = o_fwd is not None

    def tmap(t):
        return (n_t - 1 - t) if reverse else t

    def stream(s):
        return pl.BlockSpec((tb, w), lambda h, t, s=s: (tmap(t), s * nhb + h))

    n_slots = lb_logits.shape[0]
    in_specs = [stream(0), stream(2 if reverse else 1), stream(3),
                pl.BlockSpec((n_slots, w), lambda h, t: (0, h))]
    args = [proj, proj, proj, lb_logits.astype(_F32)]
    if epilogue:
        in_specs += [pl.BlockSpec((tb, w), lambda h, t: (tmap(t), h)), stream(4),
                     pl.BlockSpec((1, w), lambda h, t: (0, h))]
        args += [o_fwd, proj, gain.reshape(1, d).astype(_F32)]
    body = functools.partial(_scan_body, reverse=reverse, n_chunks=tb // _CHUNK, hb=hb,
                             reset_blocks=reset_blocks, n_tblocks=n_t, slot=slot, epilogue=epilogue)
    return pl.pallas_call(
        body,
        grid=(nhb, n_t),
        in_specs=in_specs,
        out_specs=pl.BlockSpec((tb, w), lambda h, t: (tmap(t), h)),
        out_shape=jax.ShapeDtypeStruct((m, d), _BF16 if epilogue else _F32),
        scratch_shapes=[pltpu.VMEM((hb, _HEAD_DIM, _HEAD_DIM), _F32)],
        compiler_params=_cparams(("parallel", "arbitrary")),
        name="hgrn_scan_bwd" if reverse else "hgrn_scan_fwd",
    )(*args)


def _conv3_body(prev_ref, cur_ref, nxt_ref, w_ref, b_ref, o_ref, *, jb, nj):
    j = pl.program_id(2)
    h1 = cur_ref.shape[1]
    w = w_ref[...]
    b = b_ref[...]
    rows = lax.broadcasted_iota(jnp.int32, prev_ref.shape, 0)
    prev = prev_ref[...]
    prev_wrapped = jnp.where(rows == 0, 0.0, pltpu.roll(prev, 1, 0))
    prev = jnp.where(j == 0, prev_wrapped, prev)
    nxt = nxt_ref[...]
    nxt_wrapped = jnp.where(rows == h1 - 1, 0.0, pltpu.roll(nxt, h1 - 1, 0))
    nxt = jnp.where(j == nj - 1, nxt_wrapped, nxt)
    for jj in range(jb):
        up = cur_ref[jj - 1] if jj > 0 else prev
        dn = cur_ref[jj + 1] if jj < jb - 1 else nxt
        o_ref[jj] = up * w[0:1] + cur_ref[jj] * w[1:2] + dn * w[2:3] + b


def _conv3(u, row0, batch, seq_len, conv_w, conv_b):
    m, n = u.shape
    n1, n2 = _fft_plan(seq_len)
    h1 = n1 // 2
    jb = _row_tile(n2, 8)
    nj = n2 // jb
    tn = _tile(n, 1024)
    uv = u.reshape(m // h1, h1, n)
    g0 = row0 // h1

    def slab(b, s):
        return g0 + b * n2 + s

    body = functools.partial(_conv3_body, jb=jb, nj=nj)
    return pl.pallas_call(
        body,
        grid=(n // tn, batch, nj),
        in_specs=[pl.BlockSpec((None, h1, tn), lambda c, b, j: (slab(b, jnp.where(j == 0, n2, j * jb) - 1), 0, c)),
                  pl.BlockSpec((jb, h1, tn), lambda c, b, j: ((g0 + b * n2) // jb + j, 0, c)),
                  pl.BlockSpec((None, h1, tn),
                               lambda c, b, j: (slab(b, jnp.where(j == nj - 1, 0, (j + 1) * jb)), 0, c)),
                  pl.BlockSpec((3, tn), lambda c, b, j: (0, c)),
                  pl.BlockSpec((1, tn), lambda c, b, j: (0, c))],
        out_specs=pl.BlockSpec((None, jb, h1, tn), lambda c, b, j: (b, j, 0, c)),
        out_shape=jax.ShapeDtypeStruct((batch, n2, h1, n), _F32),
        compiler_params=_cparams(("parallel", "parallel", "parallel")),
        name="hyena_conv3",
    )(uv, uv, uv, conv_w.astype(_F32), conv_b.reshape(1, n).astype(_F32))


def _permute_rows(x, groups, inverse=False):
    parts, row0 = [], 0
    for batch, seq_len in groups:
        n1, n2 = _fft_plan(seq_len)
        shape = (batch, n2, n1 // 2, -1) if inverse else (batch, n1 // 2, n2, -1)
        rows = batch * seq_len
        parts.append(x[row0:row0 + rows].reshape(shape).transpose(0, 2, 1, 3).reshape(rows, -1))
        row0 += rows
    return jnp.concatenate(parts, axis=0)


def _fft_plan(seq_len):
    n = 2 * seq_len
    n1 = 256 if n >= 32768 else 128
    while n // n1 < 8:
        n1 //= 2
    return n1, n // n1


def _dft_tables(seq_len):
    n = 2 * seq_len
    n1, n2 = _fft_plan(seq_len)
    h1 = n1 // 2
    i2 = lax.broadcasted_iota(jnp.int32, (n2, n1, n1), 0)
    k1 = lax.broadcasted_iota(jnp.int32, (n2, n1, n1), 1)
    i1 = lax.broadcasted_iota(jnp.int32, (n2, n1, n1), 2)
    phase = (i2 * k1 + n2 * (i1 * k1)) % n
    ang = phase.astype(_F32) * (2.0 * math.pi / n)
    cos, sin = jnp.cos(ang), jnp.sin(ang)
    fwd_a = jnp.concatenate([cos, -sin], axis=1).astype(_BF16)
    cos_t, sin_t = cos.transpose(0, 2, 1)[:, :h1], sin.transpose(0, 2, 1)[:, :h1]
    inv_a = (jnp.concatenate([cos_t, -sin_t], axis=2) * (1.0 / n)).astype(_BF16)
    idx = np.arange(n2)
    ang_b = 2.0 * np.pi * ((idx[:, None] * idx[None, :]) % n2) / n2
    cb, sb = np.cos(ang_b), np.sin(ang_b)
    fwd_b = jnp.asarray(np.block([[cb, sb], [-sb, cb]]), _BF16)
    inv_b = jnp.asarray(np.block([[cb, -sb], [sb, cb]]), _BF16)
    return dict(fwd_a_full=fwd_a, fwd_a_half=fwd_a[:, :, :h1], inv_a=inv_a, fwd_b=fwd_b, inv_b=inv_b)


def _swap_stage_layout(x, groups, rows):
    b, _, _, d = x.shape
    return x.reshape(b, groups, 2, rows, d).transpose(0, 3, 2, 1, 4).reshape(b, rows, 2 * groups, d)


def _stage_a_fwd_body(x_ref, m_ref, o_ref):
    o_ref[...] = jnp.dot(m_ref[...], x_ref[...].astype(_BF16), preferred_element_type=_F32).astype(o_ref.dtype)


def _stage_a_fwd(xp, col_off, mats, d):
    b, n2, h1, _ = xp.shape
    td = _tile(d, 4096)
    off = col_off // td
    return pl.pallas_call(
        _stage_a_fwd_body,
        grid=(b, n2, d // td),
        in_specs=[pl.BlockSpec((None, None, h1, td), lambda bb, j, c: (bb, j, 0, off + c)),
                  pl.BlockSpec((None, 4 * h1, h1), lambda bb, j, c: (j, 0, 0))],
        out_specs=pl.BlockSpec((None, None, 4 * h1, td), lambda bb, j, c: (bb, j, 0, c)),
        out_shape=jax.ShapeDtypeStruct((b, n2, 4 * h1, d), _BF16),
        compiler_params=_cparams(("parallel", "parallel", "parallel")),
        name="hyena_stage_a_fwd",
    )(xp, mats)


def _stage_b_conv_body(g_ref, kg_ref, l1_ref, fb_ref, fbi_ref, o_ref):
    n2 = g_ref.shape[0] // 2
    fb = fb_ref[...]
    x = jnp.dot(fb, g_ref[...], preferred_element_type=_F32)
    ks = jnp.dot(fb, kg_ref[...], preferred_element_type=_F32) * (1.0 / (l1_ref[...] + _EPS))
    xr, xi = x[:n2], x[n2:]
    kr, ki = ks[:n2], ks[n2:]
    y = jnp.concatenate([xr * kr - xi * ki, xr * ki + xi * kr], axis=0).astype(_BF16)
    o_ref[...] = jnp.dot(fbi_ref[...], y, preferred_element_type=_F32).astype(o_ref.dtype)


def _stage_b_conv(g, kg, l1, fwd_b, inv_b):
    b, n1, r, d = g.shape
    td = _tile(d, 4096)
    return pl.pallas_call(
        _stage_b_conv_body,
        grid=(b, n1, d // td),
        in_specs=[pl.BlockSpec((None, None, r, td), lambda bb, k, c: (bb, k, 0, c)),
                  pl.BlockSpec((None, r, td), lambda bb, k, c: (k, 0, c)),
                  pl.BlockSpec((1, td), lambda bb, k, c: (0, c)),
                  pl.BlockSpec((r, r), lambda bb, k, c: (0, 0)),
                  pl.BlockSpec((r, r), lambda bb, k, c: (0, 0))],
        out_specs=pl.BlockSpec((None, None, r, td), lambda bb, k, c: (bb, k, 0, c)),
        out_shape=jax.ShapeDtypeStruct((b, n1, r, d), _BF16),
        compiler_params=_cparams(("parallel", "parallel", "parallel")),
        name="hyena_stage_b_conv",
    )(g, kg, l1, fwd_b, inv_b)


def _stage_a_inv_body(h_ref, m_ref, y_ref, g_ref, s_ref, o_ref):
    conv = jnp.dot(m_ref[...], h_ref[...], preferred_element_type=_F32)
    o_ref[...] = (g_ref[...] * (conv + y_ref[...] * s_ref[...])).astype(o_ref.dtype)


def _stage_a_inv(ht, mats, yp, y_off, gp, g_off, skip, out_dtype):
    b, n2, r, d = ht.shape
    h1 = r // 4
    td = _tile(d, 4096)
    yo, go = y_off // td, g_off // td
    return pl.pallas_call(
        _stage_a_inv_body,
        grid=(b, n2, d // td),
        in_specs=[pl.BlockSpec((None, None, r, td), lambda bb, j, c: (bb, j, 0, c)),
                  pl.BlockSpec((None, h1, r), lambda bb, j, c: (j, 0, 0)),
                  pl.BlockSpec((None, None, h1, td), lambda bb, j, c: (bb, j, 0, yo + c)),
                  pl.BlockSpec((None, None, h1, td), lambda bb, j, c: (bb, j, 0, go + c)),
                  pl.BlockSpec((1, td), lambda bb, j, c: (0, c))],
        out_specs=pl.BlockSpec((None, None, h1, td), lambda bb, j, c: (bb, j, 0, c)),
        out_shape=jax.ShapeDtypeStruct((b, n2, h1, d), out_dtype),
        compiler_params=_cparams(("parallel", "parallel", "parallel")),
        name="hyena_stage_a_inv",
    )(ht, mats, yp, gp, skip.reshape(1, d).astype(_F32))


def _filter_pos(shape, j0, n1, n2, seq_len):
    r = lax.broadcasted_iota(jnp.int32, shape, 0)
    i1 = jnp.bitwise_and(r, n1 - 1)
    j = j0 + lax.shift_right_logical(r, n1.bit_length() - 1)
    n = n2 * i1 + j
    return jnp.where(i1 < n1 // 2, n, 2 * seq_len - n)


def _filter_mlp_body(band_ref, w1_ref, b1_ref, w2_ref, b2_ref, w3_ref, b3_ref, fr_ref, o_ref,
                     *, jb, n1, n2, seq_len):
    hi = lax.Precision.HIGHEST
    rows = jb * n1
    p = _filter_pos((rows, 128), pl.program_id(0) * jb, n1, n2, seq_len).astype(_F32)
    lane = lax.broadcasted_iota(jnp.int32, (rows, 128), 1)
    ang = (2.0 * math.pi / seq_len) * p * band_ref[...]
    feat = jnp.where(lane == 0, p * (1.0 / (seq_len - 1)),
                     jnp.where(lane <= _HY_BANDS, jnp.cos(ang),
                               jnp.where(lane <= 2 * _HY_BANDS, -jnp.sin(ang), 0.0)))
    fr = fr_ref[...]
    z = jnp.sin(fr * (jnp.dot(feat, w1_ref[...], preferred_element_type=_F32, precision=hi) + b1_ref[...]))
    z = jnp.sin(fr * (jnp.dot(z, w2_ref[...], preferred_element_type=_F32, precision=hi) + b2_ref[...]))
    z = jnp.sin(fr * (jnp.dot(z, w3_ref[...], preferred_element_type=_F32, precision=hi) + b3_ref[...]))
    o_ref[...] = z.reshape(jb, n1, _HY_WIDTH)


def _filter_mlp(seq_len, fc1_w, fc1_b, fc2_w, fc2_b, fc3_w, fc3_b, sin_freq):
    n1, n2 = _fft_plan(seq_len)
    jb = _row_tile(n2, 8)
    wd = _HY_WIDTH
    bands = jnp.linspace(1e-4, _HY_BANDS - 1, _HY_BANDS, dtype=_F32)
    band_row = jnp.zeros((1, 128), _F32).at[0, 1:1 + _HY_BANDS].set(bands).at[0, 1 + _HY_BANDS:_HY_EMB].set(bands)
    w1 = jnp.zeros((128, wd), _F32).at[:_HY_EMB].set(fc1_w.astype(_F32))
    const = lambda shape: pl.BlockSpec(shape, lambda i: (0,) * len(shape))
    row = lambda a: a.reshape(1, wd).astype(_F32)
    body = functools.partial(_filter_mlp_body, jb=jb, n1=n1, n2=n2, seq_len=seq_len)
    return pl.pallas_call(
        body,
        grid=(n2 // jb,),
        in_specs=[const((1, 128)), const((128, wd)), const((1, wd)), const((wd, wd)), const((1, wd)),
                  const((wd, wd)), const((1, wd)), const((1, wd))],
        out_specs=pl.BlockSpec((jb, n1, wd), lambda i: (i, 0, 0)),
        out_shape=jax.ShapeDtypeStruct((n2, n1, wd), _F32),
        compiler_params=_cparams(("parallel",)),
        name="hyena_filter_mlp",
    )(band_row, w1, row(fc1_b), fc2_w.astype(_F32), row(fc2_b), fc3_w.astype(_F32), row(fc3_b), row(sin_freq))


def _kernel_stage_a_body(z_ref, wf_ref, wb_ref, dl_ref, m_ref, o_ref, l1_ref, *, n1, n2, seq_len):
    j = pl.program_id(1)
    h1 = n1 // 2
    td = wf_ref.shape[1]
    p = _filter_pos((n1, td), j, n1, n2, seq_len)
    window = jnp.exp(-(p.astype(_F32) * (1.0 / (seq_len - 1))) * dl_ref[...])
    z = z_ref[...]
    zr = lax.broadcasted_iota(jnp.int32, (n1, 2 * _HY_WIDTH), 0)
    zc = lax.broadcasted_iota(jnp.int32, (n1, 2 * _HY_WIDTH), 1)
    zz = jnp.where((zr < h1) == (zc < _HY_WIDTH), jnp.concatenate([z, z], axis=1), 0.0)
    wcat = jnp.concatenate([wf_ref[...], wb_ref[...]], axis=0)
    z_hi = zz.astype(_BF16)
    z_lo = (zz - z_hi.astype(_F32)).astype(_BF16)
    w_hi = wcat.astype(_BF16)
    w_lo = (wcat - w_hi.astype(_F32)).astype(_BF16)
    kern = jnp.dot(jnp.concatenate([z_hi, z_hi, z_lo], axis=1), jnp.concatenate([w_hi, w_lo, w_hi], axis=0),
                   preferred_element_type=_F32) * window
    kern = jnp.where(p == seq_len, 0.0, kern)

    @pl.when(j == 0)
    def _():
        l1_ref[...] = jnp.zeros_like(l1_ref)

    l1_ref[...] += jnp.sum(jnp.abs(kern), axis=0, keepdims=True)
    o_ref[...] = jnp.dot(m_ref[...], kern.astype(_BF16), preferred_element_type=_F32).astype(o_ref.dtype)


def _kernel_stage_a(z, fc4_w, order, deltas, mats, seq_len, d):
    n1, n2 = _fft_plan(seq_len)
    td = _tile(d, 2048)
    nd = d // td
    fwd_off, bwd_off = order * 2 * nd, order * 2 * nd + nd
    body = functools.partial(_kernel_stage_a_body, n1=n1, n2=n2, seq_len=seq_len)
    return pl.pallas_call(
        body,
        grid=(nd, n2),
        in_specs=[pl.BlockSpec((None, n1, _HY_WIDTH), lambda c, j: (j, 0, 0)),
                  pl.BlockSpec((_HY_WIDTH, td), lambda c, j: (0, fwd_off + c)),
                  pl.BlockSpec((_HY_WIDTH, td), lambda c, j: (0, bwd_off + c)),
                  pl.BlockSpec((1, td), lambda c, j: (0, c)),
                  pl.BlockSpec((None, 2 * n1, n1), lambda c, j: (j, 0, 0))],
        out_specs=[pl.BlockSpec((None, 2 * n1, td), lambda c, j: (j, 0, c)),
                   pl.BlockSpec((1, td), lambda c, j: (0, c))],
        out_shape=[jax.ShapeDtypeStruct((n2, 2 * n1, d), _BF16), jax.ShapeDtypeStruct((1, d), _F32)],
        compiler_params=_cparams(("parallel", "arbitrary")),
        name="hyena_kernel_stage_a",
    )(z, fc4_w.astype(_F32), fc4_w.astype(_F32), deltas, mats)


def _hyena_long_conv(up, seq_len, d, hp):
    n1, n2 = _fft_plan(seq_len)
    tabs = _dft_tables(seq_len)
    deltas = jnp.abs(jnp.linspace(_HY_MIN_DECAY, _HY_MAX_DECAY, d, dtype=_F32)).reshape(1, d)
    z = _filter_mlp(seq_len, hp["fc1_w"], hp["fc1_b"], hp["fc2_w"], hp["fc2_b"], hp["fc3_w"], hp["fc3_b"],
                    hp["sin_freq"])
    y, y_off = up, 0
    for order in range(_HY_ORDER):
        kg, l1 = _kernel_stage_a(z, hp["fc4_w"], order, deltas, tabs["fwd_a_full"], seq_len, d)
        g = _stage_a_fwd(y, y_off, tabs["fwd_a_half"], d)
        h = _stage_b_conv(_swap_stage_layout(g, n2, n1), _swap_stage_layout(kg[None], n2, n1)[0], l1,
                          tabs["fwd_b"], tabs["inv_b"])
        last = order == _HY_ORDER - 1
        y = _stage_a_inv(_swap_stage_layout(h, n1, n2), tabs["inv_a"], y, y_off, up, (order + 1) * d,
                         hp["skip"][order], _BF16 if last else _F32)
        y_off = 0
    return y.reshape(-1, d)


def kernel(x_prompt, x_sample, norm_mix, norm_ffn, norm_final, hg_w_in, hg_lb_logits, hg_out_norm, hg_w_out,
           hy_w_in, hy_conv_w, hy_conv_b, hy_fc1_w, hy_fc1_b, hy_fc2_w, hy_fc2_b, hy_fc3_w, hy_fc3_b, hy_fc4_w,
           hy_sin_freq, hy_skip, hy_w_out, ffn_w_gate, ffn_w_up, ffn_w_down):
    b1, t1, d = x_prompt.shape
    b2, t2, _ = x_sample.shape
    groups = ((b1, t1), (b2, t2))
    seg_lens = tuple([t1] * b1 + [t2] * b2)
    x = jnp.concatenate([x_prompt.reshape(b1 * t1, d), x_sample.reshape(b2 * t2, d)], axis=0)
    depth = norm_mix.shape[0]

    for layer in range(depth):
        slot = layer // 2
        h = _rmsnorm(x, norm_mix[layer], _BF16)
        if layer % 2 == 0:
            proj = _matmul(h, (hg_w_in,), slot, mode="plain")
            o_f = _hgrn_scan(proj, hg_lb_logits[0], seg_lens, reverse=False, slot=slot)
            o = _hgrn_scan(proj, hg_lb_logits[1], seg_lens, reverse=True, slot=slot, o_fwd=o_f,
                           gain=hg_out_norm[slot])
            x = _matmul(o, (hg_w_out,), slot, mode="residual", residual=x)
        else:
            u = _matmul(_permute_rows(h, groups), (hy_w_in,), slot, mode="plain")
            hp = dict(fc1_w=hy_fc1_w[slot], fc1_b=hy_fc1_b[slot], fc2_w=hy_fc2_w[slot], fc2_b=hy_fc2_b[slot],
                      fc3_w=hy_fc3_w[slot], fc3_b=hy_fc3_b[slot], fc4_w=hy_fc4_w[slot],
                      sin_freq=hy_sin_freq[slot], skip=hy_skip[slot])
            ys, row0 = [], 0
            for bg, tg in groups:
                up = _conv3(u, row0, bg, tg, hy_conv_w[slot], hy_conv_b[slot])
                ys.append(_hyena_long_conv(up, tg, d, hp))
                row0 += bg * tg
            y = _permute_rows(jnp.concatenate(ys, axis=0), groups, inverse=True)
            x = _matmul(y, (hy_w_out,), slot, mode="residual", residual=x)
        h2 = _rmsnorm(x, norm_ffn[layer], _BF16)
        a = _matmul(h2, (ffn_w_gate, ffn_w_up), layer, mode="swiglu", out_dtype=_BF16, tn=256)
        x = _matmul(a, (ffn_w_down,), layer, mode="residual", residual=x, tm=256, single_buffer_w=True)

    return (_rmsnorm(x, norm_final, _F32, 0, b1 * t1).reshape(b1, t1, d),
            _rmsnorm(x, norm_final, _F32, b1 * t1, b2 * t2).reshape(b2, t2, d))
```

```python
import functools
import math

import numpy as np
import jax
import jax.numpy as jnp
from jax import lax
from jax.experimental import pallas as pl
from jax.experimental.pallas import tpu as pltpu

_F32 = jnp.float32
_BF16 = jnp.bfloat16

_EPS = 1e-6
_HEAD_DIM = 128
_CHUNK = 64
_SUB = 16
_EXP_CLAMP = 115.0
_HY_ORDER = 2
_HY_EMB = 33
_HY_BANDS = (_HY_EMB - 1) // 2
_HY_WIDTH = 64
_HY_MAX_DECAY = math.log(1e-2) / 0.3
_HY_MIN_DECAY = math.log(1e-2) / 1.5
_V7X_VMEM_LIMIT = 56 * 1024 * 1024


def _cparams(sem):
    return pltpu.CompilerParams(dimension_semantics=sem, vmem_limit_bytes=_V7X_VMEM_LIMIT)


def _tile(dim, want):
    if dim <= want:
        return dim
    t = want
    while t >= 128:
        if dim % t == 0:
            return t
        t -= 128
    return dim


def _row_tile(rows, want):
    t = min(rows, want)
    while rows % t:
        t //= 2
    return t


def _rmsnorm_body(x_ref, g_ref, o_ref):
    x = x_ref[...]
    ms = jnp.mean(x * x, axis=-1, keepdims=True)
    o_ref[...] = (x * lax.rsqrt(ms + _EPS) * g_ref[...]).astype(o_ref.dtype)


def _rmsnorm(x, gain, out_dtype, row0=0, rows=None):
    d = x.shape[1]
    m = x.shape[0] if rows is None else rows
    tm = _row_tile(math.gcd(m, row0) if row0 else m, 512)
    blk0 = row0 // tm
    return pl.pallas_call(
        _rmsnorm_body,
        grid=(m // tm,),
        in_specs=[pl.BlockSpec((tm, d), lambda i: (blk0 + i, 0)),
                  pl.BlockSpec((1, d), lambda i: (0, 0))],
        out_specs=pl.BlockSpec((tm, d), lambda i: (i, 0)),
        out_shape=jax.ShapeDtypeStruct((m, d), out_dtype),
        compiler_params=_cparams(("parallel",)),
        name="rmsnorm",
    )(x, gain.reshape(1, d).astype(_F32))


def _mm_body(*refs, mode, n_w):
    a_ref, w_refs, wb_refs = refs[0], refs[1:1 + n_w], refs[len(refs) - n_w:]

    @pl.when(pl.program_id(1) == 0)
    def _():
        for w_ref, wb_ref in zip(w_refs, wb_refs):
            wb_ref[...] = w_ref[...].astype(_BF16)

    a = a_ref[...]
    if mode == "swiglu":
        o_ref = refs[1 + n_w]
        g = jnp.dot(a, wb_refs[0][...], preferred_element_type=_F32)
        u = jnp.dot(a, wb_refs[1][...], preferred_element_type=_F32)
        o_ref[...] = (g * (0.5 * jnp.tanh(0.5 * g) + 0.5) * u).astype(o_ref.dtype)
    elif mode == "residual":
        r_ref, o_ref = refs[1 + n_w], refs[2 + n_w]
        o_ref[...] = r_ref[...] + jnp.dot(a, wb_refs[0][...], preferred_element_type=_F32)
    else:
        o_ref = refs[1 + n_w]
        o_ref[...] = jnp.dot(a, wb_refs[0][...], preferred_element_type=_F32).astype(o_ref.dtype)


def _matmul(a, ws, layer, *, mode, residual=None, out_dtype=_F32, tm=1024, tn=512, single_buffer_w=False):
    m, k = a.shape
    n = ws[0].shape[2]
    tm = _row_tile(m, tm)
    tn = _tile(n, tn)
    grid = (n // tn, m // tm)
    w_mode = dict(pipeline_mode=pl.Buffered(1)) if single_buffer_w else {}
    in_specs = [pl.BlockSpec((tm, k), lambda j, i: (i, 0))]
    in_specs += [pl.BlockSpec((None, k, tn), lambda j, i: (layer, 0, j), **w_mode) for _ in ws]
    args = [a, *[w.astype(_F32) for w in ws]]
    if mode == "residual":
        in_specs.append(pl.BlockSpec((tm, tn), lambda j, i: (i, j)))
        args.append(residual)
    return pl.pallas_call(
        functools.partial(_mm_body, mode=mode, n_w=len(ws)),
        grid=grid,
        in_specs=in_specs,
        out_specs=pl.BlockSpec((tm, tn), lambda j, i: (i, j)),
        out_shape=jax.ShapeDtypeStruct((m, n), out_dtype),
        scratch_shapes=[pltpu.VMEM((k, tn), _BF16) for _ in ws],
        compiler_params=_cparams(("parallel", "arbitrary")),
        name="matmul_" + mode,
    )(*args)


def _scan_body(*refs, reverse, n_chunks, hb, reset_blocks, n_tblocks, slot, epilogue):
    if epilogue:
        q_ref, fl_ref, v_ref, lbl_ref, of_ref, gate_ref, gain_ref, o_ref, st_ref = refs
    else:
        q_ref, fl_ref, v_ref, lbl_ref, o_ref, st_ref = refs
    c_len, hd = _CHUNK, _HEAD_DIM

    t = pl.program_id(1)
    blk = (n_tblocks - 1 - t) if reverse else t
    reset = blk == reset_blocks[0]
    for r in reset_blocks[1:]:
        reset = jnp.logical_or(reset, blk == r)

    @pl.when(reset)
    def _():
        st_ref[...] = jnp.zeros_like(st_ref)

    lbl = lbl_ref[...]
    e = jnp.exp(lbl - jnp.max(lbl, axis=0, keepdims=True))
    lb_all = jnp.sum(e[: slot + 1], axis=0, keepdims=True) / jnp.sum(e, axis=0, keepdims=True)

    row = lax.broadcasted_iota(jnp.int32, (c_len, c_len), 0)
    col = lax.broadcasted_iota(jnp.int32, (c_len, c_len), 1)
    keep = (row <= col) if reverse else (row >= col)
    tri = keep.astype(_BF16)
    n_sub = c_len // _SUB
    w = hb * hd
    nt = (((1,), (1,)), ((), ()))
    tn = (((0,), (0,)), ((), ()))
    heads = [slice(h * hd, (h + 1) * hd) for h in range(hb)]

    def sigmoid(z):
        return 0.5 * jnp.tanh(0.5 * z) + 0.5

    def pad_rows(part, lo_r, dtype):
        pads = [jnp.zeros((lo_r, part.shape[1]), dtype), part,
                jnp.zeros((c_len - lo_r - part.shape[0], part.shape[1]), dtype)]
        return jnp.concatenate([p for p in pads if p.shape[0]], axis=0)

    def chunk_step(ci, carry):
        c = (n_chunks - 1 - ci) if reverse else ci
        r0 = pl.multiple_of(c * c_len, c_len)
        qr = q_ref[pl.ds(r0, c_len), :]
        q = qr * sigmoid(qr)
        f = lb_all + (1.0 - lb_all) * sigmoid(fl_ref[pl.ds(r0, c_len), :])
        k = 1.0 - f
        vb = v_ref[pl.ds(r0, c_len), :].astype(_BF16)
        lf = jnp.log2(f)
        hi = lf.astype(_BF16)
        rem = lf - hi.astype(_F32)
        mid = rem.astype(_BF16)
        lo = (rem - mid.astype(_F32)).astype(_BF16)
        cum = (jnp.dot(tri, hi, preferred_element_type=_F32) + jnp.dot(tri, mid, preferred_element_type=_F32)
               + jnp.dot(tri, lo, preferred_element_type=_F32))
        edge = cum[0:1] if reverse else cum[c_len - 1:c_len]
        q_in = (q * jnp.exp2(cum)).astype(_BF16)
        k_end = (k * jnp.exp2(edge - cum)).astype(_BF16)
        st_decay = jnp.exp2(edge)
        refs = []
        for i in range(n_sub):
            if reverse:
                refs.append(cum[(i + 1) * _SUB:(i + 1) * _SUB + 1] if i < n_sub - 1 else jnp.zeros((1, w), _F32))
            else:
                refs.append(cum[i * _SUB - 1:i * _SUB] if i > 0 else jnp.zeros((1, w), _F32))
        ref_rows = jnp.concatenate([jnp.broadcast_to(r, (_SUB, w)) for r in refs], axis=0)
        a = (q * jnp.exp2(cum - ref_rows)).astype(_BF16)
        bs = []
        for i in range(n_sub):
            lo_r, hi_r = (i * _SUB, c_len) if reverse else (0, (i + 1) * _SUB)
            part = k[lo_r:hi_r] * jnp.exp2(jnp.minimum(refs[i] - cum[lo_r:hi_r], _EXP_CLAMP))
            bs.append(pad_rows(part.astype(_BF16), lo_r, _BF16))

        sts = [st_ref[h] for h in range(hb)]
        o_inter = [lax.dot_general(q_in[:, ln], sts[h].astype(_BF16), nt, preferred_element_type=_F32)
                   for h, ln in enumerate(heads)]
        a_cat = [jnp.concatenate([pad_rows(a[i * _SUB:(i + 1) * _SUB, ln], i * _SUB, _BF16) for i in range(n_sub)],
                                 axis=1) for ln in heads]
        b_cat = [jnp.concatenate([bs[i][:, ln] for i in range(n_sub)], axis=1) for ln in heads]
        scores = [lax.dot_general(a_cat[h], b_cat[h], nt, preferred_element_type=_F32) for h in range(hb)]
        scores = [jnp.where(keep, s, 0.0).astype(_BF16) for s in scores]
        o_intra = [jnp.dot(scores[h], vb[:, ln], preferred_element_type=_F32) for h, ln in enumerate(heads)]
        st_new = [lax.dot_general(vb[:, ln], k_end[:, ln], tn, preferred_element_type=_F32) for ln in heads]
        for h, ln in enumerate(heads):
            st_ref[h] = sts[h] * st_decay[:, ln] + st_new[h]
        o = jnp.concatenate([o_inter[h] + o_intra[h] for h in range(hb)], axis=1)
        if epilogue:
            tot = of_ref[pl.ds(r0, c_len), :] + o
            ys = []
            for ln in heads:
                th = tot[:, ln]
                ys.append(th * lax.rsqrt(jnp.mean(th * th, axis=-1, keepdims=True) + _EPS))
            g = gate_ref[pl.ds(r0, c_len), :]
            y = jnp.concatenate(ys, axis=1) * gain_ref[...]
            o_ref[pl.ds(r0, c_len), :] = (y * (g * sigmoid(g))).astype(o_ref.dtype)
        else:
            o_ref[pl.ds(r0, c_len), :] = o
        return carry

    lax.fori_loop(0, n_chunks, chunk_step, 0)


def _hgrn_scan(proj, lb_logits, seg_lens, *, reverse, slot, o_fwd=None, gain=None):
    m, d5 = proj.shape
    d = d5 // 5
    hb = next(n for n in (16, 8, 4, 2, 1) if d % (n * _HEAD_DIM) == 0)
    w = hb * _HEAD_DIM
    nhb = d // w
    tb = _row_tile(math.gcd(*seg_lens) if len(seg_lens) > 1 else seg_lens[0], 256)
    n_t = m // tb
    bounds = np.cumsum([0] + list(seg_lens)) // tb
    reset_blocks = tuple(int(b) - 1 for b in bounds[1:]) if reverse else tuple(int(b) for b in bounds[:-1])
    epilogue = o_fwd is not None

    def tmap(t):
        return (n_t - 1 - t) if reverse else t

    def stream(s):
        return pl.BlockSpec((tb, w), lambda h, t, s=s: (tmap(t), s * nhb + h))

    n_slots = lb_logits.shape[0]
    in_specs = [stream(0), stream(2 if reverse else 1), stream(3),
                pl.BlockSpec((n_slots, w), lambda h, t: (0, h))]
    args = [proj, proj, proj, lb_logits.astype(_F32)]
    if epilogue:
        in_specs += [pl.BlockSpec((tb, w), lambda h, t: (tmap(t), h)), stream(4),
                     pl.BlockSpec((1, w), lambda h, t: (0, h))]
        args += [o_fwd, proj, gain.reshape(1, d).astype(_F32)]
    body = functools.partial(_scan_body, reverse=reverse, n_chunks=tb // _CHUNK, hb=hb,
                             reset_blocks=reset_blocks, n_tblocks=n_t, slot=slot, epilogue=epilogue)
    return pl.pallas_call(
        body,
        grid=(nhb, n_t),
        in_specs=in_specs,
        out_specs=pl.BlockSpec((tb, w), lambda h, t: (tmap(t), h)),
        out_shape=jax.ShapeDtypeStruct((m, d), _BF16 if epilogue else _F32),
        scratch_shapes=[pltpu.VMEM((hb, _HEAD_DIM, _HEAD_DIM), _F32)],
        compiler_params=_cparams(("parallel", "arbitrary")),
        name="hgrn_scan_bwd" if reverse else "hgrn_scan_fwd",
    )(*args)


def _conv3_body(prev_ref, cur_ref, nxt_ref, w_ref, b_ref, o_ref, *, jb, nj):
    j = pl.program_id(2)
    h1 = cur_ref.shape[1]
    w = w_ref[...]
    b = b_ref[...]
    rows = lax.broadcasted_iota(jnp.int32, prev_ref.shape, 0)
    prev = prev_ref[...]
    prev_wrapped = jnp.where(rows == 0, 0.0, pltpu.roll(prev, 1, 0))
    prev = jnp.where(j == 0, prev_wrapped, prev)
    nxt = nxt_ref[...]
    nxt_wrapped = jnp.where(rows == h1 - 1, 0.0, pltpu.roll(nxt, h1 - 1, 0))
    nxt = jnp.where(j == nj - 1, nxt_wrapped, nxt)
    for jj in range(jb):
        up = cur_ref[jj - 1] if jj > 0 else prev
        dn = cur_ref[jj + 1] if jj < jb - 1 else nxt
        o_ref[jj] = up * w[0:1] + cur_ref[jj] * w[1:2] + dn * w[2:3] + b


def _conv3(u, row0, batch, seq_len, conv_w, conv_b):
    m, n = u.shape
    n1, n2 = _fft_plan(seq_len)
    h1 = n1 // 2
    jb = _row_tile(n2, 8)
    nj = n2 // jb
    tn = _tile(n, 1024)
    uv = u.reshape(m // h1, h1, n)
    g0 = row0 // h1

    def slab(b, s):
        return g0 + b * n2 + s

    body = functools.partial(_conv3_body, jb=jb, nj=nj)
    return pl.pallas_call(
        body,
        grid=(n // tn, batch, nj),
        in_specs=[pl.BlockSpec((None, h1, tn), lambda c, b, j: (slab(b, jnp.where(j == 0, n2, j * jb) - 1), 0, c)),
                  pl.BlockSpec((jb, h1, tn), lambda c, b, j: ((g0 + b * n2) // jb + j, 0, c)),
                  pl.BlockSpec((None, h1, tn),
                               lambda c, b, j: (slab(b, jnp.where(j == nj - 1, 0, (j + 1) * jb)), 0, c)),
                  pl.BlockSpec((3, tn), lambda c, b, j: (0, c)),
                  pl.BlockSpec((1, tn), lambda c, b, j: (0, c))],
        out_specs=pl.BlockSpec((None, jb, h1, tn), lambda c, b, j: (b, j, 0, c)),
        out_shape=jax.ShapeDtypeStruct((batch, n2, h1, n), _F32),
        compiler_params=_cparams(("parallel", "parallel", "parallel")),
        name="hyena_conv3",
    )(uv, uv, uv, conv_w.astype(_F32), conv_b.reshape(1, n).astype(_F32))


def _permute_rows(x, groups, inverse=False):
    parts, row0 = [], 0
    for batch, seq_len in groups:
        n1, n2 = _fft_plan(seq_len)
        shape = (batch, n2, n1 // 2, -1) if inverse else (batch, n1 // 2, n2, -1)
        rows = batch * seq_len
        parts.append(x[row0:row0 + rows].reshape(shape).transpose(0, 2, 1, 3).reshape(rows, -1))
        row0 += rows
    return jnp.concatenate(parts, axis=0)


def _fft_plan(seq_len):
    n = 2 * seq_len
    n1 = 256 if n >= 32768 else 128
    while n // n1 < 8:
        n1 //= 2
    return n1, n // n1


def _kept_k1(n1):
    return n1 // 2 + 8


def _dft_tables(seq_len):
    n = 2 * seq_len
    n1, n2 = _fft_plan(seq_len)
    h1 = n1 // 2
    kp = _kept_k1(n1)
    i2 = lax.broadcasted_iota(jnp.int32, (n2, kp, n1), 0)
    k1 = lax.broadcasted_iota(jnp.int32, (n2, kp, n1), 1)
    i1 = lax.broadcasted_iota(jnp.int32, (n2, kp, n1), 2)
    phase = (i2 * k1 + n2 * (i1 * k1)) % n
    ang = phase.astype(_F32) * (2.0 * math.pi / n)
    cos = jnp.where(k1 <= h1, jnp.cos(ang), 0.0)
    sin = jnp.where(k1 <= h1, jnp.sin(ang), 0.0)
    fwd_a = jnp.concatenate([cos, -sin], axis=1).astype(_BF16)
    weight = jnp.where((k1 == 0) | (k1 == h1), 1.0, 2.0) * (1.0 / n)
    cos_t, sin_t = (cos * weight).transpose(0, 2, 1)[:, :h1], (sin * weight).transpose(0, 2, 1)[:, :h1]
    inv_a = jnp.concatenate([cos_t, -sin_t], axis=2).astype(_BF16)
    idx = np.arange(n2)
    ang_b = 2.0 * np.pi * ((idx[:, None] * idx[None, :]) % n2) / n2
    cb, sb = np.cos(ang_b), np.sin(ang_b)
    fwd_b = jnp.asarray(np.block([[cb, sb], [-sb, cb]]), _BF16)
    inv_b = jnp.asarray(np.block([[cb, -sb], [sb, cb]]), _BF16)
    fwd_a_half = fwd_a if h1 % 128 == 0 else fwd_a[:, :, :h1]
    return dict(fwd_a_full=fwd_a, fwd_a_half=fwd_a_half, inv_a=inv_a, fwd_b=fwd_b, inv_b=inv_b)


def _swap_stage_layout(x, groups, rows):
    b, _, _, d = x.shape
    return x.reshape(b, groups, 2, rows, d).transpose(0, 3, 2, 1, 4).reshape(b, rows, 2 * groups, d)


def _stage_a_fwd_body(x_ref, m_ref, o_ref):
    o_ref[...] = jnp.dot(m_ref[...], x_ref[...].astype(_BF16), preferred_element_type=_F32).astype(o_ref.dtype)


def _stage_a_fwd(xp, col_off, mats, d):
    b, n2, h1, _ = xp.shape
    r = mats.shape[1]
    td = _tile(d, 4096)
    off = col_off // td
    return pl.pallas_call(
        _stage_a_fwd_body,
        grid=(b, n2, d // td),
        in_specs=[pl.BlockSpec((None, None, h1, td), lambda bb, j, c: (bb, j, 0, off + c)),
                  pl.BlockSpec((None, r, h1), lambda bb, j, c: (j, 0, 0))],
        out_specs=pl.BlockSpec((None, None, r, td), lambda bb, j, c: (bb, j, 0, c)),
        out_shape=jax.ShapeDtypeStruct((b, n2, r, d), _BF16),
        compiler_params=_cparams(("parallel", "parallel", "parallel")),
        name="hyena_stage_a_fwd",
    )(xp, mats)


def _stage_b_conv_body(g_ref, kg_ref, l1_ref, fb_ref, fbi_ref, o_ref):
    n2 = g_ref.shape[0] // 2
    fb = fb_ref[...]
    x = jnp.dot(fb, g_ref[...], preferred_element_type=_F32)
    ks = jnp.dot(fb, kg_ref[...], preferred_element_type=_F32) * (1.0 / (l1_ref[...] + _EPS))
    xr, xi = x[:n2], x[n2:]
    kr, ki = ks[:n2], ks[n2:]
    y = jnp.concatenate([xr * kr - xi * ki, xr * ki + xi * kr], axis=0).astype(_BF16)
    o_ref[...] = jnp.dot(fbi_ref[...], y, preferred_element_type=_F32).astype(o_ref.dtype)


def _stage_b_conv(g, kg, l1, fwd_b, inv_b):
    b, n1, r, d = g.shape
    td = _tile(d, 4096)
    return pl.pallas_call(
        _stage_b_conv_body,
        grid=(b, n1, d // td),
        in_specs=[pl.BlockSpec((None, None, r, td), lambda bb, k, c: (bb, k, 0, c)),
                  pl.BlockSpec((None, r, td), lambda bb, k, c: (k, 0, c)),
                  pl.BlockSpec((1, td), lambda bb, k, c: (0, c)),
                  pl.BlockSpec((r, r), lambda bb, k, c: (0, 0)),
                  pl.BlockSpec((r, r), lambda bb, k, c: (0, 0))],
        out_specs=pl.BlockSpec((None, None, r, td), lambda bb, k, c: (bb, k, 0, c)),
        out_shape=jax.ShapeDtypeStruct((b, n1, r, d), _BF16),
        compiler_params=_cparams(("parallel", "parallel", "parallel")),
        name="hyena_stage_b_conv",
    )(g, kg, l1, fwd_b, inv_b)


def _stage_a_inv_body(h_ref, m_ref, y_ref, g_ref, s_ref, *rest):
    conv = jnp.dot(m_ref[...], h_ref[...], preferred_element_type=_F32)
    y_new = g_ref[...] * (conv + y_ref[...] * s_ref[...])
    if len(rest) == 1:
        rest[0][...] = y_new.astype(rest[0].dtype)
    else:
        fwd_ref, o_ref, gt_ref = rest
        o_ref[...] = y_new
        gt_ref[...] = jnp.dot(fwd_ref[...], y_new.astype(_BF16), preferred_element_type=_F32).astype(gt_ref.dtype)


def _stage_a_inv(ht, mats, yp, y_off, gp, g_off, skip, out_dtype, next_mats=None):
    b, n2, r, d = ht.shape
    h1 = mats.shape[1]
    td = _tile(d, 4096)
    yo, go = y_off // td, g_off // td
    in_specs = [pl.BlockSpec((None, None, r, td), lambda bb, j, c: (bb, j, 0, c)),
                pl.BlockSpec((None, h1, r), lambda bb, j, c: (j, 0, 0)),
                pl.BlockSpec((None, None, h1, td), lambda bb, j, c: (bb, j, 0, yo + c)),
                pl.BlockSpec((None, None, h1, td), lambda bb, j, c: (bb, j, 0, go + c)),
                pl.BlockSpec((1, td), lambda bb, j, c: (0, c))]
    args = [ht, mats, yp, gp, skip.reshape(1, d).astype(_F32)]
    out_specs = pl.BlockSpec((None, None, h1, td), lambda bb, j, c: (bb, j, 0, c))
    out_shape = jax.ShapeDtypeStruct((b, n2, h1, d), out_dtype)
    if next_mats is not None:
        in_specs.append(pl.BlockSpec((None, r, h1), lambda bb, j, c: (j, 0, 0)))
        args.append(next_mats)
        out_specs = [out_specs, pl.BlockSpec((None, None, r, td), lambda bb, j, c: (bb, j, 0, c))]
        out_shape = [out_shape, jax.ShapeDtypeStruct((b, n2, r, d), _BF16)]
    return pl.pallas_call(
        _stage_a_inv_body,
        grid=(b, n2, d // td),
        in_specs=in_specs,
        out_specs=out_specs,
        out_shape=out_shape,
        compiler_params=_cparams(("parallel", "parallel", "parallel")),
        name="hyena_stage_a_inv",
    )(*args)


def _filter_pos(shape, j0, n1, n2, seq_len):
    r = lax.broadcasted_iota(jnp.int32, shape, 0)
    i1 = jnp.bitwise_and(r, n1 - 1)
    j = j0 + lax.shift_right_logical(r, n1.bit_length() - 1)
    n = n2 * i1 + j
    return jnp.where(i1 < n1 // 2, n, 2 * seq_len - n)


def _filter_mlp_body(band_ref, w1_ref, b1_ref, w2_ref, b2_ref, w3_ref, b3_ref, fr_ref, o_ref,
                     *, jb, n1, n2, seq_len):
    hi = lax.Precision.HIGHEST
    rows = jb * n1
    p = _filter_pos((rows, 128), pl.program_id(0) * jb, n1, n2, seq_len).astype(_F32)
    lane = lax.broadcasted_iota(jnp.int32, (rows, 128), 1)
    ang = (2.0 * math.pi / seq_len) * p * band_ref[...]
    feat = jnp.where(lane == 0, p * (1.0 / (seq_len - 1)),
                     jnp.where(lane <= _HY_BANDS, jnp.cos(ang),
                               jnp.where(lane <= 2 * _HY_BANDS, -jnp.sin(ang), 0.0)))
    fr = fr_ref[...]
    z = jnp.sin(fr * (jnp.dot(feat, w1_ref[...], preferred_element_type=_F32, precision=hi) + b1_ref[...]))
    z = jnp.sin(fr * (jnp.dot(z, w2_ref[...], preferred_element_type=_F32, precision=hi) + b2_ref[...]))
    z = jnp.sin(fr * (jnp.dot(z, w3_ref[...], preferred_element_type=_F32, precision=hi) + b3_ref[...]))
    o_ref[...] = z.reshape(jb, n1, _HY_WIDTH)


def _filter_mlp(seq_len, fc1_w, fc1_b, fc2_w, fc2_b, fc3_w, fc3_b, sin_freq):
    n1, n2 = _fft_plan(seq_len)
    jb = _row_tile(n2, 8)
    wd = _HY_WIDTH
    bands = jnp.linspace(1e-4, _HY_BANDS - 1, _HY_BANDS, dtype=_F32)
    band_row = jnp.zeros((1, 128), _F32).at[0, 1:1 + _HY_BANDS].set(bands).at[0, 1 + _HY_BANDS:_HY_EMB].set(bands)
    w1 = jnp.zeros((128, wd), _F32).at[:_HY_EMB].set(fc1_w.astype(_F32))
    const = lambda shape: pl.BlockSpec(shape, lambda i: (0,) * len(shape))
    row = lambda a: a.reshape(1, wd).astype(_F32)
    body = functools.partial(_filter_mlp_body, jb=jb, n1=n1, n2=n2, seq_len=seq_len)
    return pl.pallas_call(
        body,
        grid=(n2 // jb,),
        in_specs=[const((1, 128)), const((128, wd)), const((1, wd)), const((wd, wd)), const((1, wd)),
                  const((wd, wd)), const((1, wd)), const((1, wd))],
        out_specs=pl.BlockSpec((jb, n1, wd), lambda i: (i, 0, 0)),
        out_shape=jax.ShapeDtypeStruct((n2, n1, wd), _F32),
        compiler_params=_cparams(("parallel",)),
        name="hyena_filter_mlp",
    )(band_row, w1, row(fc1_b), fc2_w.astype(_F32), row(fc2_b), fc3_w.astype(_F32), row(fc3_b), row(sin_freq))


def _kernel_stage_a_body(z_ref, wf_ref, wb_ref, dl_ref, m_ref, o_ref, l1_ref, *, n1, n2, seq_len):
    j = pl.program_id(1)
    h1 = n1 // 2
    td = wf_ref.shape[1]
    p = _filter_pos((n1, td), j, n1, n2, seq_len)
    window = jnp.exp(-(p.astype(_F32) * (1.0 / (seq_len - 1))) * dl_ref[...])
    z = z_ref[...]
    zr = lax.broadcasted_iota(jnp.int32, (n1, 2 * _HY_WIDTH), 0)
    zc = lax.broadcasted_iota(jnp.int32, (n1, 2 * _HY_WIDTH), 1)
    zz = jnp.where((zr < h1) == (zc < _HY_WIDTH), jnp.concatenate([z, z], axis=1), 0.0)
    wcat = jnp.concatenate([wf_ref[...], wb_ref[...]], axis=0)
    z_hi = zz.astype(_BF16)
    z_lo = (zz - z_hi.astype(_F32)).astype(_BF16)
    w_hi = wcat.astype(_BF16)
    w_lo = (wcat - w_hi.astype(_F32)).astype(_BF16)
    kern = jnp.dot(jnp.concatenate([z_hi, z_hi, z_lo], axis=1), jnp.concatenate([w_hi, w_lo, w_hi], axis=0),
                   preferred_element_type=_F32) * window
    kern = jnp.where(p == seq_len, 0.0, kern)

    @pl.when(j == 0)
    def _():
        l1_ref[...] = jnp.zeros_like(l1_ref)

    l1_ref[...] += jnp.sum(jnp.abs(kern), axis=0, keepdims=True)
    o_ref[...] = jnp.dot(m_ref[...], kern.astype(_BF16), preferred_element_type=_F32).astype(o_ref.dtype)


def _kernel_stage_a(z, fc4_w, order, deltas, mats, seq_len, d):
    n1, n2 = _fft_plan(seq_len)
    r = mats.shape[1]
    td = _tile(d, 2048)
    nd = d // td
    fwd_off, bwd_off = order * 2 * nd, order * 2 * nd + nd
    body = functools.partial(_kernel_stage_a_body, n1=n1, n2=n2, seq_len=seq_len)
    return pl.pallas_call(
        body,
        grid=(nd, n2),
        in_specs=[pl.BlockSpec((None, n1, _HY_WIDTH), lambda c, j: (j, 0, 0)),
                  pl.BlockSpec((_HY_WIDTH, td), lambda c, j: (0, fwd_off + c)),
                  pl.BlockSpec((_HY_WIDTH, td), lambda c, j: (0, bwd_off + c)),
                  pl.BlockSpec((1, td), lambda c, j: (0, c)),
                  pl.BlockSpec((None, r, n1), lambda c, j: (j, 0, 0))],
        out_specs=[pl.BlockSpec((None, r, td), lambda c, j: (j, 0, c)),
                   pl.BlockSpec((1, td), lambda c, j: (0, c))],
        out_shape=[jax.ShapeDtypeStruct((n2, r, d), _BF16), jax.ShapeDtypeStruct((1, d), _F32)],
        compiler_params=_cparams(("parallel", "arbitrary")),
        name="hyena_kernel_stage_a",
    )(z, fc4_w.astype(_F32), fc4_w.astype(_F32), deltas, mats)


def _hyena_long_conv(up, seq_len, d, hp):
    n1, n2 = _fft_plan(seq_len)
    kp = _kept_k1(n1)
    tabs = _dft_tables(seq_len)
    deltas = jnp.abs(jnp.linspace(_HY_MIN_DECAY, _HY_MAX_DECAY, d, dtype=_F32)).reshape(1, d)
    z = _filter_mlp(seq_len, hp["fc1_w"], hp["fc1_b"], hp["fc2_w"], hp["fc2_b"], hp["fc3_w"], hp["fc3_b"],
                    hp["sin_freq"])
    y, y_off = up, 0
    g = _stage_a_fwd(y, y_off, tabs["fwd_a_half"], d)
    for order in range(_HY_ORDER):
        kg, l1 = _kernel_stage_a(z, hp["fc4_w"], order, deltas, tabs["fwd_a_full"], seq_len, d)
        h = _stage_b_conv(_swap_stage_layout(g, n2, kp), _swap_stage_layout(kg[None], n2, kp)[0], l1,
                          tabs["fwd_b"], tabs["inv_b"])
        ht = _swap_stage_layout(h, kp, n2)
        if order == _HY_ORDER - 1:
            y = _stage_a_inv(ht, tabs["inv_a"], y, y_off, up, (order + 1) * d, hp["skip"][order], _BF16)
        else:
            y, g = _stage_a_inv(ht, tabs["inv_a"], y, y_off, up, (order + 1) * d, hp["skip"][order], _F32,
                                next_mats=tabs["fwd_a_half"])
        y_off = 0
    return y.reshape(-1, d)


def kernel(x_prompt, x_sample, norm_mix, norm_ffn, norm_final, hg_w_in, hg_lb_logits, hg_out_norm, hg_w_out,
           hy_w_in, hy_conv_w, hy_conv_b, hy_fc1_w, hy_fc1_b, hy_fc2_w, hy_fc2_b, hy_fc3_w, hy_fc3_b, hy_fc4_w,
           hy_sin_freq, hy_skip, hy_w_out, ffn_w_gate, ffn_w_up, ffn_w_down):
    b1, t1, d = x_prompt.shape
    b2, t2, _ = x_sample.shape
    groups = ((b1, t1), (b2, t2))
    seg_lens = tuple([t1] * b1 + [t2] * b2)
    x = jnp.concatenate([x_prompt.reshape(b1 * t1, d), x_sample.reshape(b2 * t2, d)], axis=0)
    depth = norm_mix.shape[0]

    for layer in range(depth):
        slot = layer // 2
        h = _rmsnorm(x, norm_mix[layer], _BF16)
        if layer % 2 == 0:
            proj = _matmul(h, (hg_w_in,), slot, mode="plain")
            o_f = _hgrn_scan(proj, hg_lb_logits[0], seg_lens, reverse=False, slot=slot)
            o = _hgrn_scan(proj, hg_lb_logits[1], seg_lens, reverse=True, slot=slot, o_fwd=o_f,
                           gain=hg_out_norm[slot])
            x = _matmul(o, (hg_w_out,), slot, mode="residual", residual=x)
        else:
            u = _matmul(_permute_rows(h, groups), (hy_w_in,), slot, mode="plain")
            hp = dict(fc1_w=hy_fc1_w[slot], fc1_b=hy_fc1_b[slot], fc2_w=hy_fc2_w[slot], fc2_b=hy_fc2_b[slot],
                      fc3_w=hy_fc3_w[slot], fc3_b=hy_fc3_b[slot], fc4_w=hy_fc4_w[slot],
                      sin_freq=hy_sin_freq[slot], skip=hy_skip[slot])
            ys, row0 = [], 0
            for bg, tg in groups:
                up = _conv3(u, row0, bg, tg, hy_conv_w[slot], hy_conv_b[slot])
                ys.append(_hyena_long_conv(up, tg, d, hp))
                row0 += bg * tg
            y = _permute_rows(jnp.concatenate(ys, axis=0), groups, inverse=True)
            x = _matmul(y, (hy_w_out,), slot, mode="residual", residual=x)
        h2 = _rmsnorm(x, norm_ffn[layer], _BF16)
        a = _matmul(h2, (ffn_w_gate, ffn_w_up), layer, mode="swiglu", out_dtype=_BF16, tn=256)
        x = _matmul(a, (ffn_w_down,), layer, mode="residual", residual=x, tm=256, single_buffer_w=True)

    return (_rmsnorm(x, norm_final, _F32, 0, b1 * t1).reshape(b1, t1, d),
            _rmsnorm(x, norm_final, _F32, b1 * t1, b2 * t2).reshape(b2, t2, d))
```

```python
import functools
import math

import numpy as np
import jax
import jax.numpy as jnp
from jax import lax
from jax.experimental import pallas as pl
from jax.experimental.pallas import tpu as pltpu

_F32 = jnp.float32
_BF16 = jnp.bfloat16

_EPS = 1e-6
_HEAD_DIM = 128
_CHUNK = 64
_SUB = 16
_EXP_CLAMP = 115.0
_HY_ORDER = 2
_HY_EMB = 33
_HY_BANDS = (_HY_EMB - 1) // 2
_HY_WIDTH = 64
_HY_MAX_DECAY = math.log(1e-2) / 0.3
_HY_MIN_DECAY = math.log(1e-2) / 1.5
_V7X_VMEM_LIMIT = 56 * 1024 * 1024


def _cparams(sem):
    return pltpu.CompilerParams(dimension_semantics=sem, vmem_limit_bytes=_V7X_VMEM_LIMIT)


def _tile(dim, want):
    if dim <= want:
        return dim
    t = want
    while t >= 128:
        if dim % t == 0:
            return t
        t -= 128
    return dim


def _row_tile(rows, want):
    t = min(rows, want)
    while rows % t:
        t //= 2
    return t


def _rmsnorm_body(x_ref, g_ref, o_ref):
    x = x_ref[...]
    ms = jnp.mean(x * x, axis=-1, keepdims=True)
    o_ref[...] = (x * lax.rsqrt(ms + _EPS) * g_ref[...]).astype(o_ref.dtype)


def _rmsnorm(x, gain, out_dtype, row0=0, rows=None):
    d = x.shape[1]
    m = x.shape[0] if rows is None else rows
    tm = _row_tile(math.gcd(m, row0) if row0 else m, 512)
    blk0 = row0 // tm
    return pl.pallas_call(
        _rmsnorm_body,
        grid=(m // tm,),
        in_specs=[pl.BlockSpec((tm, d), lambda i: (blk0 + i, 0)),
                  pl.BlockSpec((1, d), lambda i: (0, 0))],
        out_specs=pl.BlockSpec((tm, d), lambda i: (i, 0)),
        out_shape=jax.ShapeDtypeStruct((m, d), out_dtype),
        compiler_params=_cparams(("parallel",)),
        name="rmsnorm",
    )(x, gain.reshape(1, d).astype(_F32))


def _mm_body(*refs, mode, n_w, rows_resident):
    a_ref, w_refs = refs[0], refs[1:1 + n_w]
    if rows_resident:
        ws = [w_ref[...].astype(_BF16) for w_ref in w_refs]
    else:
        wb_refs = refs[len(refs) - n_w:]

        @pl.when(pl.program_id(1) == 0)
        def _():
            for w_ref, wb_ref in zip(w_refs, wb_refs):
                wb_ref[...] = w_ref[...].astype(_BF16)

        ws = [wb_ref[...] for wb_ref in wb_refs]

    a = a_ref[...]
    if mode == "swiglu":
        o_ref = refs[1 + n_w]
        g = jnp.dot(a, ws[0], preferred_element_type=_F32)
        u = jnp.dot(a, ws[1], preferred_element_type=_F32)
        o_ref[...] = (g * (0.5 * jnp.tanh(0.5 * g) + 0.5) * u).astype(o_ref.dtype)
    elif mode == "residual":
        r_ref, o_ref = refs[1 + n_w], refs[2 + n_w]
        o_ref[...] = r_ref[...] + jnp.dot(a, ws[0], preferred_element_type=_F32)
    else:
        o_ref = refs[1 + n_w]
        o_ref[...] = jnp.dot(a, ws[0], preferred_element_type=_F32).astype(o_ref.dtype)


def _matmul(a, ws, layer, *, mode, residual=None, out_dtype=_F32, tm=1024, tn=512, rows_resident=False,
            single_buffer_w=False, k_part=(0, 1)):
    m = a.shape[0]
    kp, parts = k_part
    k = a.shape[1] // parts
    n = ws[0].shape[2]
    tm = _row_tile(m, tm)
    tn = _tile(n, tn)
    if rows_resident:
        grid = (m // tm, n // tn)
        ij = lambda g0, g1: (g0, g1)
        a_mode, w_mode = dict(pipeline_mode=pl.Buffered(1)), {}
        scratch = []
    else:
        grid = (n // tn, m // tm)
        ij = lambda g0, g1: (g1, g0)
        a_mode, w_mode = {}, (dict(pipeline_mode=pl.Buffered(1)) if single_buffer_w else {})
        scratch = [pltpu.VMEM((k, tn), _BF16) for _ in ws]
    in_specs = [pl.BlockSpec((tm, k), lambda g0, g1: (ij(g0, g1)[0], kp), **a_mode)]
    in_specs += [pl.BlockSpec((None, k, tn), lambda g0, g1: (layer, kp, ij(g0, g1)[1]), **w_mode) for _ in ws]
    args = [a, *[w.astype(_F32) for w in ws]]
    if mode == "residual":
        in_specs.append(pl.BlockSpec((tm, tn), lambda g0, g1: ij(g0, g1)))
        args.append(residual)
    return pl.pallas_call(
        functools.partial(_mm_body, mode=mode, n_w=len(ws), rows_resident=rows_resident),
        grid=grid,
        in_specs=in_specs,
        out_specs=pl.BlockSpec((tm, tn), lambda g0, g1: ij(g0, g1)),
        out_shape=jax.ShapeDtypeStruct((m, n), out_dtype),
        scratch_shapes=scratch,
        compiler_params=_cparams(("parallel", "arbitrary")),
        name="matmul_" + mode,
    )(*args)


def _scan_body(*refs, reverse, n_chunks, hb, reset_blocks, n_tblocks, slot, epilogue):
    if epilogue:
        q_ref, fl_ref, v_ref, lbl_ref, of_ref, gate_ref, gain_ref, o_ref, st_ref = refs
    else:
        q_ref, fl_ref, v_ref, lbl_ref, o_ref, st_ref = refs
    c_len, hd = _CHUNK, _HEAD_DIM

    t = pl.program_id(1)
    blk = (n_tblocks - 1 - t) if reverse else t
    reset = blk == reset_blocks[0]
    for r in reset_blocks[1:]:
        reset = jnp.logical_or(reset, blk == r)

    @pl.when(reset)
    def _():
        st_ref[...] = jnp.zeros_like(st_ref)

    lbl = lbl_ref[...]
    e = jnp.exp(lbl - jnp.max(lbl, axis=0, keepdims=True))
    lb_all = jnp.sum(e[: slot + 1], axis=0, keepdims=True) / jnp.sum(e, axis=0, keepdims=True)

    row = lax.broadcasted_iota(jnp.int32, (c_len, c_len), 0)
    col = lax.broadcasted_iota(jnp.int32, (c_len, c_len), 1)
    keep = (row <= col) if reverse else (row >= col)
    tri = keep.astype(_BF16)
    n_sub = c_len // _SUB
    w = hb * hd
    nt = (((1,), (1,)), ((), ()))
    tn = (((0,), (0,)), ((), ()))
    heads = [slice(h * hd, (h + 1) * hd) for h in range(hb)]

    def sigmoid(z):
        return 0.5 * jnp.tanh(0.5 * z) + 0.5

    def pad_rows(part, lo_r, dtype):
        pads = [jnp.zeros((lo_r, part.shape[1]), dtype), part,
                jnp.zeros((c_len - lo_r - part.shape[0], part.shape[1]), dtype)]
        return jnp.concatenate([p for p in pads if p.shape[0]], axis=0)

    def chunk_step(ci, carry):
        c = (n_chunks - 1 - ci) if reverse else ci
        r0 = pl.multiple_of(c * c_len, c_len)
        qr = q_ref[pl.ds(r0, c_len), :]
        q = qr * sigmoid(qr)
        f = lb_all + (1.0 - lb_all) * sigmoid(fl_ref[pl.ds(r0, c_len), :])
        k = 1.0 - f
        vb = v_ref[pl.ds(r0, c_len), :].astype(_BF16)
        lf = jnp.log2(f)
        hi = lf.astype(_BF16)
        rem = lf - hi.astype(_F32)
        mid = rem.astype(_BF16)
        lo = (rem - mid.astype(_F32)).astype(_BF16)
        cum = (jnp.dot(tri, hi, preferred_element_type=_F32) + jnp.dot(tri, mid, preferred_element_type=_F32)
               + jnp.dot(tri, lo, preferred_element_type=_F32))
        edge = cum[0:1] if reverse else cum[c_len - 1:c_len]
        q_in = (q * jnp.exp2(cum)).astype(_BF16)
        k_end = (k * jnp.exp2(edge - cum)).astype(_BF16)
        st_decay = jnp.exp2(edge)
        refs = []
        for i in range(n_sub):
            if reverse:
                refs.append(cum[(i + 1) * _SUB:(i + 1) * _SUB + 1] if i < n_sub - 1 else jnp.zeros((1, w), _F32))
            else:
                refs.append(cum[i * _SUB - 1:i * _SUB] if i > 0 else jnp.zeros((1, w), _F32))
        ref_rows = jnp.concatenate([jnp.broadcast_to(r, (_SUB, w)) for r in refs], axis=0)
        a = (q * jnp.exp2(cum - ref_rows)).astype(_BF16)
        bs = []
        for i in range(n_sub):
            lo_r, hi_r = (i * _SUB, c_len) if reverse else (0, (i + 1) * _SUB)
            part = k[lo_r:hi_r] * jnp.exp2(jnp.minimum(refs[i] - cum[lo_r:hi_r], _EXP_CLAMP))
            bs.append(pad_rows(part.astype(_BF16), lo_r, _BF16))

        sts = [st_ref[h] for h in range(hb)]
        o_inter = [lax.dot_general(q_in[:, ln], sts[h].astype(_BF16), nt, preferred_element_type=_F32)
                   for h, ln in enumerate(heads)]
        a_cat = [jnp.concatenate([pad_rows(a[i * _SUB:(i + 1) * _SUB, ln], i * _SUB, _BF16) for i in range(n_sub)],
                                 axis=1) for ln in heads]
        b_cat = [jnp.concatenate([bs[i][:, ln] for i in range(n_sub)], axis=1) for ln in heads]
        scores = [lax.dot_general(a_cat[h], b_cat[h], nt, preferred_element_type=_F32) for h in range(hb)]
        scores = [jnp.where(keep, s, 0.0).astype(_BF16) for s in scores]
        o_intra = [jnp.dot(scores[h], vb[:, ln], preferred_element_type=_F32) for h, ln in enumerate(heads)]
        st_new = [lax.dot_general(vb[:, ln], k_end[:, ln], tn, preferred_element_type=_F32) for ln in heads]
        for h, ln in enumerate(heads):
            st_ref[h] = sts[h] * st_decay[:, ln] + st_new[h]
        o = jnp.concatenate([o_inter[h] + o_intra[h] for h in range(hb)], axis=1)
        if epilogue:
            tot = of_ref[pl.ds(r0, c_len), :] + o
            ys = []
            for ln in heads:
                th = tot[:, ln]
                ys.append(th * lax.rsqrt(jnp.mean(th * th, axis=-1, keepdims=True) + _EPS))
            g = gate_ref[pl.ds(r0, c_len), :]
            y = jnp.concatenate(ys, axis=1) * gain_ref[...]
            o_ref[pl.ds(r0, c_len), :] = (y * (g * sigmoid(g))).astype(o_ref.dtype)
        else:
            o_ref[pl.ds(r0, c_len), :] = o
        return carry

    lax.fori_loop(0, n_chunks, chunk_step, 0)


def _hgrn_scan(proj, lb_logits, seg_lens, *, reverse, slot, o_fwd=None, gain=None):
    m, d5 = proj.shape
    d = d5 // 5
    hb = next(n for n in (16, 8, 4, 2, 1) if d % (n * _HEAD_DIM) == 0)
    w = hb * _HEAD_DIM
    nhb = d // w
    tb = _row_tile(math.gcd(*seg_lens) if len(seg_lens) > 1 else seg_lens[0], 256)
    n_t = m // tb
    bounds = np.cumsum([0] + list(seg_lens)) // tb
    reset_blocks = tuple(int(b) - 1 for b in bounds[1:]) if reverse else tuple(int(b) for b in bounds[:-1])
    epilogue = o_fwd is not None

    def tmap(t):
        return (n_t - 1 - t) if reverse else t

    def stream(s):
        return pl.BlockSpec((tb, w), lambda h, t, s=s: (tmap(t), s * nhb + h))

    n_slots = lb_logits.shape[0]
    in_specs = [stream(0), stream(2 if reverse else 1), stream(3),
                pl.BlockSpec((n_slots, w), lambda h, t: (0, h))]
    args = [proj, proj, proj, lb_logits.astype(_F32)]
    if epilogue:
        in_specs += [pl.BlockSpec((tb, w), lambda h, t: (tmap(t), h)), stream(4),
                     pl.BlockSpec((1, w), lambda h, t: (0, h))]
        args += [o_fwd, proj, gain.reshape(1, d).astype(_F32)]
    body = functools.partial(_scan_body, reverse=reverse, n_chunks=tb // _CHUNK, hb=hb,
                             reset_blocks=reset_blocks, n_tblocks=n_t, slot=slot, epilogue=epilogue)
    return pl.pallas_call(
        body,
        grid=(nhb, n_t),
        in_specs=in_specs,
        out_specs=pl.BlockSpec((tb, w), lambda h, t: (tmap(t), h)),
        out_shape=jax.ShapeDtypeStruct((m, d), _BF16 if epilogue else _F32),
        scratch_shapes=[pltpu.VMEM((hb, _HEAD_DIM, _HEAD_DIM), _F32)],
        compiler_params=_cparams(("parallel", "arbitrary")),
        name="hgrn_scan_bwd" if reverse else "hgrn_scan_fwd",
    )(*args)


def _conv3_body(prev_ref, cur_ref, nxt_ref, w_ref, b_ref, o_ref, *, jb, nj):
    j = pl.program_id(2)
    h1 = cur_ref.shape[1]
    w = w_ref[...]
    b = b_ref[...]
    rows = lax.broadcasted_iota(jnp.int32, prev_ref.shape, 0)
    prev = prev_ref[...]
    prev_wrapped = jnp.where(rows == 0, 0.0, pltpu.roll(prev, 1, 0))
    prev = jnp.where(j == 0, prev_wrapped, prev)
    nxt = nxt_ref[...]
    nxt_wrapped = jnp.where(rows == h1 - 1, 0.0, pltpu.roll(nxt, h1 - 1, 0))
    nxt = jnp.where(j == nj - 1, nxt_wrapped, nxt)
    for jj in range(jb):
        up = cur_ref[jj - 1] if jj > 0 else prev
        dn = cur_ref[jj + 1] if jj < jb - 1 else nxt
        o_ref[jj] = up * w[0:1] + cur_ref[jj] * w[1:2] + dn * w[2:3] + b


def _conv3(u, row0, batch, seq_len, conv_w, conv_b):
    m, n = u.shape
    n1, n2 = _fft_plan(seq_len)
    h1 = n1 // 2
    jb = _row_tile(n2, 8)
    nj = n2 // jb
    tn = _tile(n, 1024)
    uv = u.reshape(m // h1, h1, n)
    g0 = row0 // h1

    def slab(b, s):
        return g0 + b * n2 + s

    body = functools.partial(_conv3_body, jb=jb, nj=nj)
    return pl.pallas_call(
        body,
        grid=(n // tn, batch, nj),
        in_specs=[pl.BlockSpec((None, h1, tn), lambda c, b, j: (slab(b, jnp.where(j == 0, n2, j * jb) - 1), 0, c)),
                  pl.BlockSpec((jb, h1, tn), lambda c, b, j: ((g0 + b * n2) // jb + j, 0, c)),
                  pl.BlockSpec((None, h1, tn),
                               lambda c, b, j: (slab(b, jnp.where(j == nj - 1, 0, (j + 1) * jb)), 0, c)),
                  pl.BlockSpec((3, tn), lambda c, b, j: (0, c)),
                  pl.BlockSpec((1, tn), lambda c, b, j: (0, c))],
        out_specs=pl.BlockSpec((None, jb, h1, tn), lambda c, b, j: (b, j, 0, c)),
        out_shape=jax.ShapeDtypeStruct((batch, n2, h1, n), _F32),
        compiler_params=_cparams(("parallel", "parallel", "parallel")),
        name="hyena_conv3",
    )(uv, uv, uv, conv_w.astype(_F32), conv_b.reshape(1, n).astype(_F32))


def _permute_rows(x, groups, inverse=False):
    parts, row0 = [], 0
    for batch, seq_len in groups:
        n1, n2 = _fft_plan(seq_len)
        shape = (batch, n2, n1 // 2, -1) if inverse else (batch, n1 // 2, n2, -1)
        rows = batch * seq_len
        parts.append(x[row0:row0 + rows].reshape(shape).transpose(0, 2, 1, 3).reshape(rows, -1))
        row0 += rows
    return jnp.concatenate(parts, axis=0)


def _fft_plan(seq_len):
    n = 2 * seq_len
    n1 = 256 if n >= 32768 else 128
    while n // n1 < 8:
        n1 //= 2
    return n1, n // n1


def _kept_k1(n1):
    return n1 // 2 + 8


def _dft_tables(seq_len):
    n = 2 * seq_len
    n1, n2 = _fft_plan(seq_len)
    h1 = n1 // 2
    kp = _kept_k1(n1)
    i2 = lax.broadcasted_iota(jnp.int32, (n2, kp, n1), 0)
    k1 = lax.broadcasted_iota(jnp.int32, (n2, kp, n1), 1)
    i1 = lax.broadcasted_iota(jnp.int32, (n2, kp, n1), 2)
    phase = (i2 * k1 + n2 * (i1 * k1)) % n
    ang = phase.astype(_F32) * (2.0 * math.pi / n)
    cos = jnp.where(k1 <= h1, jnp.cos(ang), 0.0)
    sin = jnp.where(k1 <= h1, jnp.sin(ang), 0.0)
    fwd_a = jnp.concatenate([cos, -sin], axis=1).astype(_BF16)
    weight = jnp.where((k1 == 0) | (k1 == h1), 1.0, 2.0) * (1.0 / n)
    cos_t, sin_t = (cos * weight).transpose(0, 2, 1)[:, :h1], (sin * weight).transpose(0, 2, 1)[:, :h1]
    inv_a = jnp.concatenate([cos_t, -sin_t], axis=2).astype(_BF16)
    idx = np.arange(n2)
    ang_b = 2.0 * np.pi * ((idx[:, None] * idx[None, :]) % n2) / n2
    cb, sb = np.cos(ang_b), np.sin(ang_b)
    fwd_b = jnp.asarray(np.block([[cb, sb], [-sb, cb]]), _BF16)
    inv_b = jnp.asarray(np.block([[cb, -sb], [sb, cb]]), _BF16)
    fwd_a_half = fwd_a if h1 % 128 == 0 else fwd_a[:, :, :h1]
    return dict(fwd_a_full=fwd_a, fwd_a_half=fwd_a_half, inv_a=inv_a, fwd_b=fwd_b, inv_b=inv_b)


def _swap_stage_layout(x, groups, rows):
    b, _, _, d = x.shape
    return x.reshape(b, groups, 2, rows, d).transpose(0, 3, 2, 1, 4).reshape(b, rows, 2 * groups, d)


def _stage_a_fwd_body(x_ref, m_ref, o_ref):
    o_ref[...] = jnp.dot(m_ref[...], x_ref[...].astype(_BF16), preferred_element_type=_F32).astype(o_ref.dtype)


def _stage_a_fwd(xp, col_off, mats, d):
    b, n2, h1, _ = xp.shape
    r = mats.shape[1]
    td = _tile(d, 4096)
    off = col_off // td
    return pl.pallas_call(
        _stage_a_fwd_body,
        grid=(b, n2, d // td),
        in_specs=[pl.BlockSpec((None, None, h1, td), lambda bb, j, c: (bb, j, 0, off + c)),
                  pl.BlockSpec((None, r, h1), lambda bb, j, c: (j, 0, 0))],
        out_specs=pl.BlockSpec((None, None, r, td), lambda bb, j, c: (bb, j, 0, c)),
        out_shape=jax.ShapeDtypeStruct((b, n2, r, d), _BF16),
        compiler_params=_cparams(("parallel", "parallel", "parallel")),
        name="hyena_stage_a_fwd",
    )(xp, mats)


def _stage_b_conv_body(g_ref, kg_ref, l1_ref, fb_ref, fbi_ref, o_ref, ks_ref):
    n2 = g_ref.shape[0] // 2
    fb = fb_ref[...]
    x = jnp.dot(fb, g_ref[...], preferred_element_type=_F32)

    @pl.when(pl.program_id(2) == 0)
    def _():
        ks_ref[...] = jnp.dot(fb, kg_ref[...], preferred_element_type=_F32) * (1.0 / (l1_ref[...] + _EPS))

    xr, xi = x[:n2], x[n2:]
    kr, ki = ks_ref[:n2, :], ks_ref[n2:, :]
    y = jnp.concatenate([xr * kr - xi * ki, xr * ki + xi * kr], axis=0).astype(_BF16)
    o_ref[...] = jnp.dot(fbi_ref[...], y, preferred_element_type=_F32).astype(o_ref.dtype)


def _stage_b_conv(g, kg, l1, fwd_b, inv_b):
    b, n1, r, d = g.shape
    td = _tile(d, 4096)
    return pl.pallas_call(
        _stage_b_conv_body,
        grid=(n1, d // td, b),
        in_specs=[pl.BlockSpec((None, None, r, td), lambda k, c, bb: (bb, k, 0, c)),
                  pl.BlockSpec((None, r, td), lambda k, c, bb: (k, 0, c)),
                  pl.BlockSpec((1, td), lambda k, c, bb: (0, c)),
                  pl.BlockSpec((r, r), lambda k, c, bb: (0, 0)),
                  pl.BlockSpec((r, r), lambda k, c, bb: (0, 0))],
        out_specs=pl.BlockSpec((None, None, r, td), lambda k, c, bb: (bb, k, 0, c)),
        out_shape=jax.ShapeDtypeStruct((b, n1, r, d), _BF16),
        scratch_shapes=[pltpu.VMEM((r, td), _F32)],
        compiler_params=_cparams(("parallel", "parallel", "arbitrary")),
        name="hyena_stage_b_conv",
    )(g, kg, l1, fwd_b, inv_b)


def _stage_a_inv_body(h_ref, m_ref, y_ref, g_ref, s_ref, *rest):
    conv = jnp.dot(m_ref[...], h_ref[...], preferred_element_type=_F32)
    y_new = g_ref[...] * (conv + y_ref[...] * s_ref[...])
    if len(rest) == 1:
        rest[0][...] = y_new.astype(rest[0].dtype)
    else:
        fwd_ref, o_ref, gt_ref = rest
        o_ref[...] = y_new
        gt_ref[...] = jnp.dot(fwd_ref[...], y_new.astype(_BF16), preferred_element_type=_F32).astype(gt_ref.dtype)


def _stage_a_inv(ht, mats, yp, y_off, gp, g_off, skip, out_dtype, next_mats=None):
    b, n2, r, d = ht.shape
    h1 = mats.shape[1]
    td = _tile(d, 4096)
    yo, go = y_off // td, g_off // td
    in_specs = [pl.BlockSpec((None, None, r, td), lambda bb, j, c: (bb, j, 0, c)),
                pl.BlockSpec((None, h1, r), lambda bb, j, c: (j, 0, 0)),
                pl.BlockSpec((None, None, h1, td), lambda bb, j, c: (bb, j, 0, yo + c)),
                pl.BlockSpec((None, None, h1, td), lambda bb, j, c: (bb, j, 0, go + c)),
                pl.BlockSpec((1, td), lambda bb, j, c: (0, c))]
    args = [ht, mats, yp, gp, skip.reshape(1, d).astype(_F32)]
    out_specs = pl.BlockSpec((None, None, h1, td), lambda bb, j, c: (bb, j, 0, c))
    out_shape = jax.ShapeDtypeStruct((b, n2, h1, d), out_dtype)
    if next_mats is not None:
        in_specs.append(pl.BlockSpec((None, r, h1), lambda bb, j, c: (j, 0, 0)))
        args.append(next_mats)
        out_specs = [out_specs, pl.BlockSpec((None, None, r, td), lambda bb, j, c: (bb, j, 0, c))]
        out_shape = [out_shape, jax.ShapeDtypeStruct((b, n2, r, d), _BF16)]
    return pl.pallas_call(
        _stage_a_inv_body,
        grid=(b, n2, d // td),
        in_specs=in_specs,
        out_specs=out_specs,
        out_shape=out_shape,
        compiler_params=_cparams(("parallel", "parallel", "parallel")),
        name="hyena_stage_a_inv",
    )(*args)


def _filter_pos(shape, j0, n1, n2, seq_len):
    r = lax.broadcasted_iota(jnp.int32, shape, 0)
    i1 = jnp.bitwise_and(r, n1 - 1)
    j = j0 + lax.shift_right_logical(r, n1.bit_length() - 1)
    n = n2 * i1 + j
    return jnp.where(i1 < n1 // 2, n, 2 * seq_len - n)


def _filter_mlp_body(band_ref, w1_ref, b1_ref, w2_ref, b2_ref, w3_ref, b3_ref, fr_ref, o_ref,
                     *, jb, n1, n2, seq_len):
    hi = lax.Precision.HIGHEST
    rows = jb * n1
    p = _filter_pos((rows, 128), pl.program_id(0) * jb, n1, n2, seq_len).astype(_F32)
    lane = lax.broadcasted_iota(jnp.int32, (rows, 128), 1)
    ang = (2.0 * math.pi / seq_len) * p * band_ref[...]
    feat = jnp.where(lane == 0, p * (1.0 / (seq_len - 1)),
                     jnp.where(lane <= _HY_BANDS, jnp.cos(ang),
                               jnp.where(lane <= 2 * _HY_BANDS, -jnp.sin(ang), 0.0)))
    fr = fr_ref[...]
    z = jnp.sin(fr * (jnp.dot(feat, w1_ref[...], preferred_element_type=_F32, precision=hi) + b1_ref[...]))
    z = jnp.sin(fr * (jnp.dot(z, w2_ref[...], preferred_element_type=_F32, precision=hi) + b2_ref[...]))
    z = jnp.sin(fr * (jnp.dot(z, w3_ref[...], preferred_element_type=_F32, precision=hi) + b3_ref[...]))
    o_ref[...] = z.reshape(jb, n1, _HY_WIDTH)


def _filter_mlp(seq_len, fc1_w, fc1_b, fc2_w, fc2_b, fc3_w, fc3_b, sin_freq):
    n1, n2 = _fft_plan(seq_len)
    jb = _row_tile(n2, 8)
    wd = _HY_WIDTH
    bands = jnp.linspace(1e-4, _HY_BANDS - 1, _HY_BANDS, dtype=_F32)
    band_row = jnp.zeros((1, 128), _F32).at[0, 1:1 + _HY_BANDS].set(bands).at[0, 1 + _HY_BANDS:_HY_EMB].set(bands)
    w1 = jnp.zeros((128, wd), _F32).at[:_HY_EMB].set(fc1_w.astype(_F32))
    const = lambda shape: pl.BlockSpec(shape, lambda i: (0,) * len(shape))
    row = lambda a: a.reshape(1, wd).astype(_F32)
    body = functools.partial(_filter_mlp_body, jb=jb, n1=n1, n2=n2, seq_len=seq_len)
    return pl.pallas_call(
        body,
        grid=(n2 // jb,),
        in_specs=[const((1, 128)), const((128, wd)), const((1, wd)), const((wd, wd)), const((1, wd)),
                  const((wd, wd)), const((1, wd)), const((1, wd))],
        out_specs=pl.BlockSpec((jb, n1, wd), lambda i: (i, 0, 0)),
        out_shape=jax.ShapeDtypeStruct((n2, n1, wd), _F32),
        compiler_params=_cparams(("parallel",)),
        name="hyena_filter_mlp",
    )(band_row, w1, row(fc1_b), fc2_w.astype(_F32), row(fc2_b), fc3_w.astype(_F32), row(fc3_b), row(sin_freq))


def _kernel_stage_a_body(z_ref, wf_ref, wb_ref, dl_ref, m_ref, o_ref, l1_ref, *, n1, n2, seq_len):
    j = pl.program_id(1)
    h1 = n1 // 2
    td = wf_ref.shape[1]
    p = _filter_pos((n1, td), j, n1, n2, seq_len)
    window = jnp.exp(-(p.astype(_F32) * (1.0 / (seq_len - 1))) * dl_ref[...])
    z = z_ref[...]
    zr = lax.broadcasted_iota(jnp.int32, (n1, 2 * _HY_WIDTH), 0)
    zc = lax.broadcasted_iota(jnp.int32, (n1, 2 * _HY_WIDTH), 1)
    zz = jnp.where((zr < h1) == (zc < _HY_WIDTH), jnp.concatenate([z, z], axis=1), 0.0)
    wcat = jnp.concatenate([wf_ref[...], wb_ref[...]], axis=0)
    kern = jnp.dot(zz.astype(_BF16), wcat.astype(_BF16), preferred_element_type=_F32) * window
    kern = jnp.where(p == seq_len, 0.0, kern)

    @pl.when(j == 0)
    def _():
        l1_ref[...] = jnp.zeros_like(l1_ref)

    l1_ref[...] += jnp.sum(jnp.abs(kern), axis=0, keepdims=True)
    o_ref[...] = jnp.dot(m_ref[...], kern.astype(_BF16), preferred_element_type=_F32).astype(o_ref.dtype)


def _kernel_stage_a(z, fc4_w, order, deltas, mats, seq_len, d):
    n1, n2 = _fft_plan(seq_len)
    r = mats.shape[1]
    td = _tile(d, 2048)
    nd = d // td
    fwd_off, bwd_off = order * 2 * nd, order * 2 * nd + nd
    body = functools.partial(_kernel_stage_a_body, n1=n1, n2=n2, seq_len=seq_len)
    return pl.pallas_call(
        body,
        grid=(nd, n2),
        in_specs=[pl.BlockSpec((None, n1, _HY_WIDTH), lambda c, j: (j, 0, 0)),
                  pl.BlockSpec((_HY_WIDTH, td), lambda c, j: (0, fwd_off + c)),
                  pl.BlockSpec((_HY_WIDTH, td), lambda c, j: (0, bwd_off + c)),
                  pl.BlockSpec((1, td), lambda c, j: (0, c)),
                  pl.BlockSpec((None, r, n1), lambda c, j: (j, 0, 0))],
        out_specs=[pl.BlockSpec((None, r, td), lambda c, j: (j, 0, c)),
                   pl.BlockSpec((1, td), lambda c, j: (0, c))],
        out_shape=[jax.ShapeDtypeStruct((n2, r, d), _BF16), jax.ShapeDtypeStruct((1, d), _F32)],
        compiler_params=_cparams(("parallel", "arbitrary")),
        name="hyena_kernel_stage_a",
    )(z, fc4_w.astype(_F32), fc4_w.astype(_F32), deltas, mats)


def _hyena_long_conv(up, seq_len, d, hp):
    n1, n2 = _fft_plan(seq_len)
    kp = _kept_k1(n1)
    tabs = _dft_tables(seq_len)
    deltas = jnp.abs(jnp.linspace(_HY_MIN_DECAY, _HY_MAX_DECAY, d, dtype=_F32)).reshape(1, d)
    z = _filter_mlp(seq_len, hp["fc1_w"], hp["fc1_b"], hp["fc2_w"], hp["fc2_b"], hp["fc3_w"], hp["fc3_b"],
                    hp["sin_freq"])
    y, y_off = up, 0
    g = _stage_a_fwd(y, y_off, tabs["fwd_a_half"], d)
    for order in range(_HY_ORDER):
        kg, l1 = _kernel_stage_a(z, hp["fc4_w"], order, deltas, tabs["fwd_a_full"], seq_len, d)
        h = _stage_b_conv(_swap_stage_layout(g, n2, kp), _swap_stage_layout(kg[None], n2, kp)[0], l1,
                          tabs["fwd_b"], tabs["inv_b"])
        ht = _swap_stage_layout(h, kp, n2)
        if order == _HY_ORDER - 1:
            y = _stage_a_inv(ht, tabs["inv_a"], y, y_off, up, (order + 1) * d, hp["skip"][order], _BF16)
        else:
            y, g = _stage_a_inv(ht, tabs["inv_a"], y, y_off, up, (order + 1) * d, hp["skip"][order], _F32,
                                next_mats=tabs["fwd_a_half"])
        y_off = 0
    return y.reshape(-1, d)


def kernel(x_prompt, x_sample, norm_mix, norm_ffn, norm_final, hg_w_in, hg_lb_logits, hg_out_norm, hg_w_out,
           hy_w_in, hy_conv_w, hy_conv_b, hy_fc1_w, hy_fc1_b, hy_fc2_w, hy_fc2_b, hy_fc3_w, hy_fc3_b, hy_fc4_w,
           hy_sin_freq, hy_skip, hy_w_out, ffn_w_gate, ffn_w_up, ffn_w_down):
    b1, t1, d = x_prompt.shape
    b2, t2, _ = x_sample.shape
    groups = ((b1, t1), (b2, t2))
    seg_lens = tuple([t1] * b1 + [t2] * b2)
    x = jnp.concatenate([x_prompt.reshape(b1 * t1, d), x_sample.reshape(b2 * t2, d)], axis=0)
    depth = norm_mix.shape[0]

    for layer in range(depth):
        slot = layer // 2
        h = _rmsnorm(x, norm_mix[layer], _BF16)
        if layer % 2 == 0:
            proj = _matmul(h, (hg_w_in,), slot, mode="plain", tm=2048, rows_resident=True)
            o_f = _hgrn_scan(proj, hg_lb_logits[0], seg_lens, reverse=False, slot=slot)
            o = _hgrn_scan(proj, hg_lb_logits[1], seg_lens, reverse=True, slot=slot, o_fwd=o_f,
                           gain=hg_out_norm[slot])
            x = _matmul(o, (hg_w_out,), slot, mode="residual", residual=x, rows_resident=True)
        else:
            u = _matmul(_permute_rows(h, groups), (hy_w_in,), slot, mode="plain", tm=2048, rows_resident=True)
            hp = dict(fc1_w=hy_fc1_w[slot], fc1_b=hy_fc1_b[slot], fc2_w=hy_fc2_w[slot], fc2_b=hy_fc2_b[slot],
                      fc3_w=hy_fc3_w[slot], fc3_b=hy_fc3_b[slot], fc4_w=hy_fc4_w[slot],
                      sin_freq=hy_sin_freq[slot], skip=hy_skip[slot])
            ys, row0 = [], 0
            for bg, tg in groups:
                up = _conv3(u, row0, bg, tg, hy_conv_w[slot], hy_conv_b[slot])
                ys.append(_hyena_long_conv(up, tg, d, hp))
                row0 += bg * tg
            y = _permute_rows(jnp.concatenate(ys, axis=0), groups, inverse=True)
            x = _matmul(y, (hy_w_out,), slot, mode="residual", residual=x)
        h2 = _rmsnorm(x, norm_ffn[layer], _BF16)
        a = _matmul(h2, (ffn_w_gate, ffn_w_up), layer, mode="swiglu", out_dtype=_BF16, tm=2048, tn=256,
                    rows_resident=True)
        parts = 2 if a.shape[1] % 256 == 0 else 1
        for part in range(parts):
            x = _matmul(a, (ffn_w_down,), layer, mode="residual", residual=x, rows_resident=True,
                        k_part=(part, parts))

    return (_rmsnorm(x, norm_final, _F32, 0, b1 * t1).reshape(b1, t1, d),
            _rmsnorm(x, norm_final, _F32, b1 * t1, b2 * t2).reshape(b2, t2, d))
```

```python
import functools
import math

import numpy as np
import jax
import jax.numpy as jnp
from jax import lax
from jax.experimental import pallas as pl
from jax.experimental.pallas import tpu as pltpu

_F32 = jnp.float32
_BF16 = jnp.bfloat16

_EPS = 1e-6
_HEAD_DIM = 128
_CHUNK = 64
_SUB = 16
_EXP_CLAMP = 115.0
_HY_ORDER = 2
_HY_EMB = 33
_HY_BANDS = (_HY_EMB - 1) // 2
_HY_WIDTH = 64
_HY_MAX_DECAY = math.log(1e-2) / 0.3
_HY_MIN_DECAY = math.log(1e-2) / 1.5
_V7X_VMEM_LIMIT = 56 * 1024 * 1024


def _cparams(sem):
    return pltpu.CompilerParams(dimension_semantics=sem, vmem_limit_bytes=_V7X_VMEM_LIMIT)


def _tile(dim, want):
    if dim <= want:
        return dim
    t = want
    while t >= 128:
        if dim % t == 0:
            return t
        t -= 128
    return dim


def _row_tile(rows, want):
    t = min(rows, want)
    while rows % t:
        t //= 2
    return t


def _rmsnorm_body(x_ref, g_ref, o_ref):
    x = x_ref[...]
    ms = jnp.mean(x * x, axis=-1, keepdims=True)
    o_ref[...] = (x * lax.rsqrt(ms + _EPS) * g_ref[...]).astype(o_ref.dtype)


def _rmsnorm(x, gain, out_dtype, row0=0, rows=None):
    d = x.shape[1]
    m = x.shape[0] if rows is None else rows
    tm = _row_tile(math.gcd(m, row0) if row0 else m, 512)
    blk0 = row0 // tm
    return pl.pallas_call(
        _rmsnorm_body,
        grid=(m // tm,),
        in_specs=[pl.BlockSpec((tm, d), lambda i: (blk0 + i, 0)),
                  pl.BlockSpec((1, d), lambda i: (0, 0))],
        out_specs=pl.BlockSpec((tm, d), lambda i: (i, 0)),
        out_shape=jax.ShapeDtypeStruct((m, d), out_dtype),
        compiler_params=_cparams(("parallel",)),
        name="rmsnorm",
    )(x, gain.reshape(1, d).astype(_F32))


def _mm_body(*refs, mode, n_w, rows_resident):
    a_ref, w_refs = refs[0], refs[1:1 + n_w]
    if rows_resident:
        ws = [w_ref[...].astype(_BF16) for w_ref in w_refs]
    else:
        wb_refs = refs[len(refs) - n_w:]

        @pl.when(pl.program_id(1) == 0)
        def _():
            for w_ref, wb_ref in zip(w_refs, wb_refs):
                wb_ref[...] = w_ref[...].astype(_BF16)

        ws = [wb_ref[...] for wb_ref in wb_refs]

    a = a_ref[...]
    if mode == "swiglu":
        o_ref = refs[1 + n_w]
        g = jnp.dot(a, ws[0], preferred_element_type=_F32)
        u = jnp.dot(a, ws[1], preferred_element_type=_F32)
        o_ref[...] = (g * (0.5 * jnp.tanh(0.5 * g) + 0.5) * u).astype(o_ref.dtype)
    elif mode == "residual":
        r_ref, o_ref = refs[1 + n_w], refs[2 + n_w]
        o_ref[...] = r_ref[...] + jnp.dot(a, ws[0], preferred_element_type=_F32)
    else:
        o_ref = refs[1 + n_w]
        o_ref[...] = jnp.dot(a, ws[0], preferred_element_type=_F32).astype(o_ref.dtype)


def _matmul(a, ws, layer, *, mode, residual=None, res_row0=0, out_dtype=_F32, tm=1024, tn=512,
            rows_resident=False, single_buffer_w=False):
    m, k = a.shape
    n = ws[0].shape[2]
    tm = _row_tile(math.gcd(m, res_row0) if res_row0 else m, tm)
    tn = _tile(n, tn)
    if rows_resident:
        grid = (m // tm, n // tn)
        ij = lambda g0, g1: (g0, g1)
        a_mode, w_mode = dict(pipeline_mode=pl.Buffered(1)), {}
        scratch = []
    else:
        grid = (n // tn, m // tm)
        ij = lambda g0, g1: (g1, g0)
        a_mode, w_mode = {}, (dict(pipeline_mode=pl.Buffered(1)) if single_buffer_w else {})
        scratch = [pltpu.VMEM((k, tn), _BF16) for _ in ws]
    in_specs = [pl.BlockSpec((tm, k), lambda g0, g1: (ij(g0, g1)[0], 0), **a_mode)]
    in_specs += [pl.BlockSpec((None, k, tn), lambda g0, g1: (layer, 0, ij(g0, g1)[1]), **w_mode) for _ in ws]
    args = [a, *[w.astype(_F32) for w in ws]]
    out_spec = pl.BlockSpec((tm, tn), lambda g0, g1: ij(g0, g1))
    out_shape = jax.ShapeDtypeStruct((m, n), out_dtype)
    aliases = {}
    if mode == "residual":
        blk0 = res_row0 // tm
        out_spec = pl.BlockSpec((tm, tn), lambda g0, g1: (blk0 + ij(g0, g1)[0], ij(g0, g1)[1]))
        in_specs.append(out_spec)
        args.append(residual)
        out_shape = jax.ShapeDtypeStruct(residual.shape, residual.dtype)
        aliases = {len(args) - 1: 0}
    return pl.pallas_call(
        functools.partial(_mm_body, mode=mode, n_w=len(ws), rows_resident=rows_resident),
        grid=grid,
        in_specs=in_specs,
        out_specs=out_spec,
        out_shape=out_shape,
        scratch_shapes=scratch,
        input_output_aliases=aliases,
        compiler_params=_cparams(("parallel", "arbitrary")),
        name="matmul_" + mode,
    )(*args)


def _scan_body(*refs, reverse, n_chunks, hb, reset_blocks, n_tblocks, slot, epilogue):
    if epilogue:
        q_ref, fl_ref, v_ref, lbl_ref, of_ref, gate_ref, gain_ref, o_ref, st_ref = refs
    else:
        q_ref, fl_ref, v_ref, lbl_ref, o_ref, st_ref = refs
    c_len, hd = _CHUNK, _HEAD_DIM

    t = pl.program_id(1)
    blk = (n_tblocks - 1 - t) if reverse else t
    reset = blk == reset_blocks[0]
    for r in reset_blocks[1:]:
        reset = jnp.logical_or(reset, blk == r)

    @pl.when(reset)
    def _():
        st_ref[...] = jnp.zeros_like(st_ref)

    lbl = lbl_ref[...]
    e = jnp.exp(lbl - jnp.max(lbl, axis=0, keepdims=True))
    lb_all = jnp.sum(e[: slot + 1], axis=0, keepdims=True) / jnp.sum(e, axis=0, keepdims=True)

    row = lax.broadcasted_iota(jnp.int32, (c_len, c_len), 0)
    col = lax.broadcasted_iota(jnp.int32, (c_len, c_len), 1)
    keep = (row <= col) if reverse else (row >= col)
    tri = keep.astype(_BF16)
    n_sub = c_len // _SUB
    w = hb * hd
    nt = (((1,), (1,)), ((), ()))
    tn = (((0,), (0,)), ((), ()))
    heads = [slice(h * hd, (h + 1) * hd) for h in range(hb)]

    def sigmoid(z):
        return 0.5 * jnp.tanh(0.5 * z) + 0.5

    def pad_rows(part, lo_r, dtype):
        pads = [jnp.zeros((lo_r, part.shape[1]), dtype), part,
                jnp.zeros((c_len - lo_r - part.shape[0], part.shape[1]), dtype)]
        return jnp.concatenate([p for p in pads if p.shape[0]], axis=0)

    def chunk_step(ci, carry):
        c = (n_chunks - 1 - ci) if reverse else ci
        r0 = pl.multiple_of(c * c_len, c_len)
        qr = q_ref[pl.ds(r0, c_len), :]
        q = qr * sigmoid(qr)
        f = lb_all + (1.0 - lb_all) * sigmoid(fl_ref[pl.ds(r0, c_len), :])
        k = 1.0 - f
        vb = v_ref[pl.ds(r0, c_len), :].astype(_BF16)
        lf = jnp.log2(f)
        hi = lf.astype(_BF16)
        rem = lf - hi.astype(_F32)
        mid = rem.astype(_BF16)
        lo = (rem - mid.astype(_F32)).astype(_BF16)
        cum = (jnp.dot(tri, hi, preferred_element_type=_F32) + jnp.dot(tri, mid, preferred_element_type=_F32)
               + jnp.dot(tri, lo, preferred_element_type=_F32))
        edge = cum[0:1] if reverse else cum[c_len - 1:c_len]
        q_in = (q * jnp.exp2(cum)).astype(_BF16)
        k_end = (k * jnp.exp2(edge - cum)).astype(_BF16)
        st_decay = jnp.exp2(edge)
        refs = []
        for i in range(n_sub):
            if reverse:
                refs.append(cum[(i + 1) * _SUB:(i + 1) * _SUB + 1] if i < n_sub - 1 else jnp.zeros((1, w), _F32))
            else:
                refs.append(cum[i * _SUB - 1:i * _SUB] if i > 0 else jnp.zeros((1, w), _F32))
        ref_rows = jnp.concatenate([jnp.broadcast_to(r, (_SUB, w)) for r in refs], axis=0)
        a = (q * jnp.exp2(cum - ref_rows)).astype(_BF16)
        bs = []
        for i in range(n_sub):
            lo_r, hi_r = (i * _SUB, c_len) if reverse else (0, (i + 1) * _SUB)
            part = k[lo_r:hi_r] * jnp.exp2(jnp.minimum(refs[i] - cum[lo_r:hi_r], _EXP_CLAMP))
            bs.append(pad_rows(part.astype(_BF16), lo_r, _BF16))

        sts = [st_ref[h] for h in range(hb)]
        o_inter = [lax.dot_general(q_in[:, ln], sts[h].astype(_BF16), nt, preferred_element_type=_F32)
                   for h, ln in enumerate(heads)]
        a_cat = [jnp.concatenate([pad_rows(a[i * _SUB:(i + 1) * _SUB, ln], i * _SUB, _BF16) for i in range(n_sub)],
                                 axis=1) for ln in heads]
        b_cat = [jnp.concatenate([bs[i][:, ln] for i in range(n_sub)], axis=1) for ln in heads]
        scores = [lax.dot_general(a_cat[h], b_cat[h], nt, preferred_element_type=_F32) for h in range(hb)]
        scores = [jnp.where(keep, s, 0.0).astype(_BF16) for s in scores]
        o_intra = [jnp.dot(scores[h], vb[:, ln], preferred_element_type=_F32) for h, ln in enumerate(heads)]
        st_new = [lax.dot_general(vb[:, ln], k_end[:, ln], tn, preferred_element_type=_F32) for ln in heads]
        for h, ln in enumerate(heads):
            st_ref[h] = sts[h] * st_decay[:, ln] + st_new[h]
        o = jnp.concatenate([o_inter[h] + o_intra[h] for h in range(hb)], axis=1)
        if epilogue:
            tot = of_ref[pl.ds(r0, c_len), :] + o
            ys = []
            for ln in heads:
                th = tot[:, ln]
                ys.append(th * lax.rsqrt(jnp.mean(th * th, axis=-1, keepdims=True) + _EPS))
            g = gate_ref[pl.ds(r0, c_len), :]
            y = jnp.concatenate(ys, axis=1) * gain_ref[...]
            o_ref[pl.ds(r0, c_len), :] = (y * (g * sigmoid(g))).astype(o_ref.dtype)
        else:
            o_ref[pl.ds(r0, c_len), :] = o
        return carry

    lax.fori_loop(0, n_chunks, chunk_step, 0)


def _hgrn_scan(proj, lb_logits, seg_lens, *, reverse, slot, o_fwd=None, gain=None):
    m, d5 = proj.shape
    d = d5 // 5
    hb = next(n for n in (16, 8, 4, 2, 1) if d % (n * _HEAD_DIM) == 0)
    w = hb * _HEAD_DIM
    nhb = d // w
    tb = _row_tile(math.gcd(*seg_lens) if len(seg_lens) > 1 else seg_lens[0], 256)
    n_t = m // tb
    bounds = np.cumsum([0] + list(seg_lens)) // tb
    reset_blocks = tuple(int(b) - 1 for b in bounds[1:]) if reverse else tuple(int(b) for b in bounds[:-1])
    epilogue = o_fwd is not None

    def tmap(t):
        return (n_t - 1 - t) if reverse else t

    def stream(s):
        return pl.BlockSpec((tb, w), lambda h, t, s=s: (tmap(t), s * nhb + h))

    n_slots = lb_logits.shape[0]
    in_specs = [stream(0), stream(2 if reverse else 1), stream(3),
                pl.BlockSpec((n_slots, w), lambda h, t: (0, h))]
    args = [proj, proj, proj, lb_logits.astype(_F32)]
    if epilogue:
        in_specs += [pl.BlockSpec((tb, w), lambda h, t: (tmap(t), h)), stream(4),
                     pl.BlockSpec((1, w), lambda h, t: (0, h))]
        args += [o_fwd, proj, gain.reshape(1, d).astype(_F32)]
    body = functools.partial(_scan_body, reverse=reverse, n_chunks=tb // _CHUNK, hb=hb,
                             reset_blocks=reset_blocks, n_tblocks=n_t, slot=slot, epilogue=epilogue)
    return pl.pallas_call(
        body,
        grid=(nhb, n_t),
        in_specs=in_specs,
        out_specs=pl.BlockSpec((tb, w), lambda h, t: (tmap(t), h)),
        out_shape=jax.ShapeDtypeStruct((m, d), _BF16 if epilogue else _F32),
        scratch_shapes=[pltpu.VMEM((hb, _HEAD_DIM, _HEAD_DIM), _F32)],
        compiler_params=_cparams(("parallel", "arbitrary")),
        name="hgrn_scan_bwd" if reverse else "hgrn_scan_fwd",
    )(*args)


def _conv3_body(prev_ref, cur_ref, nxt_ref, w_ref, b_ref, o_ref, *, jb, nj):
    j = pl.program_id(2)
    h1 = cur_ref.shape[1]
    w = w_ref[...]
    b = b_ref[...]
    rows = lax.broadcasted_iota(jnp.int32, prev_ref.shape, 0)
    prev = prev_ref[...]
    prev_wrapped = jnp.where(rows == 0, 0.0, pltpu.roll(prev, 1, 0))
    prev = jnp.where(j == 0, prev_wrapped, prev)
    nxt = nxt_ref[...]
    nxt_wrapped = jnp.where(rows == h1 - 1, 0.0, pltpu.roll(nxt, h1 - 1, 0))
    nxt = jnp.where(j == nj - 1, nxt_wrapped, nxt)
    for jj in range(jb):
        up = cur_ref[jj - 1] if jj > 0 else prev
        dn = cur_ref[jj + 1] if jj < jb - 1 else nxt
        o_ref[jj] = up * w[0:1] + cur_ref[jj] * w[1:2] + dn * w[2:3] + b


def _conv3(u, row0, batch, seq_len, conv_w, conv_b):
    m, n = u.shape
    n1, n2 = _fft_plan(seq_len)
    h1 = n1 // 2
    jb = _row_tile(n2, 8)
    nj = n2 // jb
    tn = _tile(n, 1024)
    uv = u.reshape(m // h1, h1, n)
    g0 = row0 // h1

    def slab(b, s):
        return g0 + b * n2 + s

    body = functools.partial(_conv3_body, jb=jb, nj=nj)
    return pl.pallas_call(
        body,
        grid=(n // tn, batch, nj),
        in_specs=[pl.BlockSpec((None, h1, tn), lambda c, b, j: (slab(b, jnp.where(j == 0, n2, j * jb) - 1), 0, c)),
                  pl.BlockSpec((jb, h1, tn), lambda c, b, j: ((g0 + b * n2) // jb + j, 0, c)),
                  pl.BlockSpec((None, h1, tn),
                               lambda c, b, j: (slab(b, jnp.where(j == nj - 1, 0, (j + 1) * jb)), 0, c)),
                  pl.BlockSpec((3, tn), lambda c, b, j: (0, c)),
                  pl.BlockSpec((1, tn), lambda c, b, j: (0, c))],
        out_specs=pl.BlockSpec((None, jb, h1, tn), lambda c, b, j: (b, j, 0, c)),
        out_shape=jax.ShapeDtypeStruct((batch, n2, h1, n), _F32),
        compiler_params=_cparams(("parallel", "parallel", "parallel")),
        name="hyena_conv3",
    )(uv, uv, uv, conv_w.astype(_F32), conv_b.reshape(1, n).astype(_F32))


def _permute_rows(x, groups, inverse=False):
    parts, row0 = [], 0
    for batch, seq_len in groups:
        n1, n2 = _fft_plan(seq_len)
        shape = (batch, n2, n1 // 2, -1) if inverse else (batch, n1 // 2, n2, -1)
        rows = batch * seq_len
        parts.append(x[row0:row0 + rows].reshape(shape).transpose(0, 2, 1, 3).reshape(rows, -1))
        row0 += rows
    return jnp.concatenate(parts, axis=0)


def _fft_plan(seq_len):
    n = 2 * seq_len
    n1 = 256 if n >= 32768 else 128
    while n // n1 < 8:
        n1 //= 2
    return n1, n // n1


def _kept_k1(n1):
    return n1 // 2 + 8


def _dft_tables(seq_len):
    n = 2 * seq_len
    n1, n2 = _fft_plan(seq_len)
    h1 = n1 // 2
    kp = _kept_k1(n1)
    i2 = lax.broadcasted_iota(jnp.int32, (n2, kp, n1), 0)
    k1 = lax.broadcasted_iota(jnp.int32, (n2, kp, n1), 1)
    i1 = lax.broadcasted_iota(jnp.int32, (n2, kp, n1), 2)
    phase = (i2 * k1 + n2 * (i1 * k1)) % n
    ang = phase.astype(_F32) * (2.0 * math.pi / n)
    cos = jnp.where(k1 <= h1, jnp.cos(ang), 0.0)
    sin = jnp.where(k1 <= h1, jnp.sin(ang), 0.0)
    fwd_a = jnp.concatenate([cos, -sin], axis=1).astype(_BF16)
    weight = jnp.where((k1 == 0) | (k1 == h1), 1.0, 2.0) * (1.0 / n)
    cos_t, sin_t = (cos * weight).transpose(0, 2, 1)[:, :h1], (sin * weight).transpose(0, 2, 1)[:, :h1]
    inv_a = jnp.concatenate([cos_t, -sin_t], axis=2).astype(_BF16)
    idx = np.arange(n2)
    ang_b = 2.0 * np.pi * ((idx[:, None] * idx[None, :]) % n2) / n2
    cb, sb = np.cos(ang_b), np.sin(ang_b)
    fwd_b = jnp.asarray(np.block([[cb, sb], [-sb, cb]]), _BF16)
    inv_b = jnp.asarray(np.block([[cb, -sb], [sb, cb]]), _BF16)
    fwd_a_half = fwd_a if h1 % 128 == 0 else fwd_a[:, :, :h1]
    return dict(fwd_a_full=fwd_a, fwd_a_half=fwd_a_half, inv_a=inv_a, fwd_b=fwd_b, inv_b=inv_b)


def _swap_stage_layout(x, groups, rows):
    b, _, _, d = x.shape
    return x.reshape(b, groups, 2, rows, d).transpose(0, 3, 2, 1, 4).reshape(b, rows, 2 * groups, d)


def _stage_a_fwd_body(x_ref, m_ref, o_ref):
    o_ref[...] = jnp.dot(m_ref[...], x_ref[...].astype(_BF16), preferred_element_type=_F32).astype(o_ref.dtype)


def _stage_a_fwd(xp, col_off, mats, d):
    b, n2, h1, _ = xp.shape
    r = mats.shape[1]
    td = _tile(d, 4096)
    off = col_off // td
    return pl.pallas_call(
        _stage_a_fwd_body,
        grid=(b, n2, d // td),
        in_specs=[pl.BlockSpec((None, None, h1, td), lambda bb, j, c: (bb, j, 0, off + c)),
                  pl.BlockSpec((None, r, h1), lambda bb, j, c: (j, 0, 0))],
        out_specs=pl.BlockSpec((None, None, r, td), lambda bb, j, c: (bb, j, 0, c)),
        out_shape=jax.ShapeDtypeStruct((b, n2, r, d), _BF16),
        compiler_params=_cparams(("parallel", "parallel", "parallel")),
        name="hyena_stage_a_fwd",
    )(xp, mats)


def _stage_b_conv_body(g_ref, kg_ref, l1_ref, fb_ref, fbi_ref, o_ref, ks_ref):
    n2 = g_ref.shape[0] // 2
    fb = fb_ref[...]
    x = jnp.dot(fb, g_ref[...], preferred_element_type=_F32)

    @pl.when(pl.program_id(2) == 0)
    def _():
        ks_ref[...] = jnp.dot(fb, kg_ref[...], preferred_element_type=_F32) * (1.0 / (l1_ref[...] + _EPS))

    xr, xi = x[:n2], x[n2:]
    kr, ki = ks_ref[:n2, :], ks_ref[n2:, :]
    y = jnp.concatenate([xr * kr - xi * ki, xr * ki + xi * kr], axis=0).astype(_BF16)
    o_ref[...] = jnp.dot(fbi_ref[...], y, preferred_element_type=_F32).astype(o_ref.dtype)


def _stage_b_conv(g, kg, l1, fwd_b, inv_b):
    b, n1, r, d = g.shape
    td = _tile(d, 4096)
    return pl.pallas_call(
        _stage_b_conv_body,
        grid=(n1, d // td, b),
        in_specs=[pl.BlockSpec((None, None, r, td), lambda k, c, bb: (bb, k, 0, c)),
                  pl.BlockSpec((None, r, td), lambda k, c, bb: (k, 0, c)),
                  pl.BlockSpec((1, td), lambda k, c, bb: (0, c)),
                  pl.BlockSpec((r, r), lambda k, c, bb: (0, 0)),
                  pl.BlockSpec((r, r), lambda k, c, bb: (0, 0))],
        out_specs=pl.BlockSpec((None, None, r, td), lambda k, c, bb: (bb, k, 0, c)),
        out_shape=jax.ShapeDtypeStruct((b, n1, r, d), _BF16),
        scratch_shapes=[pltpu.VMEM((r, td), _F32)],
        compiler_params=_cparams(("parallel", "parallel", "arbitrary")),
        name="hyena_stage_b_conv",
    )(g, kg, l1, fwd_b, inv_b)


def _stage_a_inv_body(h_ref, m_ref, y_ref, g_ref, s_ref, *rest):
    conv = jnp.dot(m_ref[...], h_ref[...], preferred_element_type=_F32)
    y_new = g_ref[...] * (conv + y_ref[...] * s_ref[...])
    if len(rest) == 1:
        rest[0][...] = y_new.astype(rest[0].dtype)
    else:
        fwd_ref, o_ref, gt_ref = rest
        o_ref[...] = y_new
        gt_ref[...] = jnp.dot(fwd_ref[...], y_new.astype(_BF16), preferred_element_type=_F32).astype(gt_ref.dtype)


def _stage_a_inv(ht, mats, yp, y_off, gp, g_off, skip, out_dtype, next_mats=None):
    b, n2, r, d = ht.shape
    h1 = mats.shape[1]
    td = _tile(d, 4096)
    yo, go = y_off // td, g_off // td
    in_specs = [pl.BlockSpec((None, None, r, td), lambda bb, j, c: (bb, j, 0, c)),
                pl.BlockSpec((None, h1, r), lambda bb, j, c: (j, 0, 0)),
                pl.BlockSpec((None, None, h1, td), lambda bb, j, c: (bb, j, 0, yo + c)),
                pl.BlockSpec((None, None, h1, td), lambda bb, j, c: (bb, j, 0, go + c)),
                pl.BlockSpec((1, td), lambda bb, j, c: (0, c))]
    args = [ht, mats, yp, gp, skip.reshape(1, d).astype(_F32)]
    out_specs = pl.BlockSpec((None, None, h1, td), lambda bb, j, c: (bb, j, 0, c))
    out_shape = jax.ShapeDtypeStruct((b, n2, h1, d), out_dtype)
    if next_mats is not None:
        in_specs.append(pl.BlockSpec((None, r, h1), lambda bb, j, c: (j, 0, 0)))
        args.append(next_mats)
        out_specs = [out_specs, pl.BlockSpec((None, None, r, td), lambda bb, j, c: (bb, j, 0, c))]
        out_shape = [out_shape, jax.ShapeDtypeStruct((b, n2, r, d), _BF16)]
    return pl.pallas_call(
        _stage_a_inv_body,
        grid=(b, n2, d // td),
        in_specs=in_specs,
        out_specs=out_specs,
        out_shape=out_shape,
        compiler_params=_cparams(("parallel", "parallel", "parallel")),
        name="hyena_stage_a_inv",
    )(*args)


def _filter_pos(shape, j0, n1, n2, seq_len):
    r = lax.broadcasted_iota(jnp.int32, shape, 0)
    i1 = jnp.bitwise_and(r, n1 - 1)
    j = j0 + lax.shift_right_logical(r, n1.bit_length() - 1)
    n = n2 * i1 + j
    return jnp.where(i1 < n1 // 2, n, 2 * seq_len - n)


def _filter_mlp_body(band_ref, w1_ref, b1_ref, w2_ref, b2_ref, w3_ref, b3_ref, fr_ref, o_ref,
                     *, jb, n1, n2, seq_len):
    hi = lax.Precision.HIGHEST
    rows = jb * n1
    p = _filter_pos((rows, 128), pl.program_id(0) * jb, n1, n2, seq_len).astype(_F32)
    lane = lax.broadcasted_iota(jnp.int32, (rows, 128), 1)
    ang = (2.0 * math.pi / seq_len) * p * band_ref[...]
    feat = jnp.where(lane == 0, p * (1.0 / (seq_len - 1)),
                     jnp.where(lane <= _HY_BANDS, jnp.cos(ang),
                               jnp.where(lane <= 2 * _HY_BANDS, -jnp.sin(ang), 0.0)))
    fr = fr_ref[...]
    z = jnp.sin(fr * (jnp.dot(feat, w1_ref[...], preferred_element_type=_F32, precision=hi) + b1_ref[...]))
    z = jnp.sin(fr * (jnp.dot(z, w2_ref[...], preferred_element_type=_F32, precision=hi) + b2_ref[...]))
    z = jnp.sin(fr * (jnp.dot(z, w3_ref[...], preferred_element_type=_F32, precision=hi) + b3_ref[...]))
    o_ref[...] = z.reshape(jb, n1, _HY_WIDTH)


def _filter_mlp(seq_len, fc1_w, fc1_b, fc2_w, fc2_b, fc3_w, fc3_b, sin_freq):
    n1, n2 = _fft_plan(seq_len)
    jb = _row_tile(n2, 8)
    wd = _HY_WIDTH
    bands = jnp.linspace(1e-4, _HY_BANDS - 1, _HY_BANDS, dtype=_F32)
    band_row = jnp.zeros((1, 128), _F32).at[0, 1:1 + _HY_BANDS].set(bands).at[0, 1 + _HY_BANDS:_HY_EMB].set(bands)
    w1 = jnp.zeros((128, wd), _F32).at[:_HY_EMB].set(fc1_w.astype(_F32))
    const = lambda shape: pl.BlockSpec(shape, lambda i: (0,) * len(shape))
    row = lambda a: a.reshape(1, wd).astype(_F32)
    body = functools.partial(_filter_mlp_body, jb=jb, n1=n1, n2=n2, seq_len=seq_len)
    return pl.pallas_call(
        body,
        grid=(n2 // jb,),
        in_specs=[const((1, 128)), const((128, wd)), const((1, wd)), const((wd, wd)), const((1, wd)),
                  const((wd, wd)), const((1, wd)), const((1, wd))],
        out_specs=pl.BlockSpec((jb, n1, wd), lambda i: (i, 0, 0)),
        out_shape=jax.ShapeDtypeStruct((n2, n1, wd), _F32),
        compiler_params=_cparams(("parallel",)),
        name="hyena_filter_mlp",
    )(band_row, w1, row(fc1_b), fc2_w.astype(_F32), row(fc2_b), fc3_w.astype(_F32), row(fc3_b), row(sin_freq))


def _kernel_stage_a_body(z_ref, wf_ref, wb_ref, dl_ref, m_ref, o_ref, l1_ref, *, n1, n2, seq_len):
    j = pl.program_id(1)
    h1 = n1 // 2
    td = wf_ref.shape[1]
    p = _filter_pos((n1, td), j, n1, n2, seq_len)
    window = jnp.exp(-(p.astype(_F32) * (1.0 / (seq_len - 1))) * dl_ref[...])
    z = z_ref[...]
    zr = lax.broadcasted_iota(jnp.int32, (n1, 2 * _HY_WIDTH), 0)
    zc = lax.broadcasted_iota(jnp.int32, (n1, 2 * _HY_WIDTH), 1)
    zz = jnp.where((zr < h1) == (zc < _HY_WIDTH), jnp.concatenate([z, z], axis=1), 0.0)
    wcat = jnp.concatenate([wf_ref[...], wb_ref[...]], axis=0)
    kern = jnp.dot(zz.astype(_BF16), wcat.astype(_BF16), preferred_element_type=_F32) * window
    kern = jnp.where(p == seq_len, 0.0, kern)

    @pl.when(j == 0)
    def _():
        l1_ref[...] = jnp.zeros_like(l1_ref)

    l1_ref[...] += jnp.sum(jnp.abs(kern), axis=0, keepdims=True)
    o_ref[...] = jnp.dot(m_ref[...], kern.astype(_BF16), preferred_element_type=_F32).astype(o_ref.dtype)


def _kernel_stage_a(z, fc4_w, order, deltas, mats, seq_len, d):
    n1, n2 = _fft_plan(seq_len)
    r = mats.shape[1]
    td = _tile(d, 2048)
    nd = d // td
    fwd_off, bwd_off = order * 2 * nd, order * 2 * nd + nd
    body = functools.partial(_kernel_stage_a_body, n1=n1, n2=n2, seq_len=seq_len)
    return pl.pallas_call(
        body,
        grid=(nd, n2),
        in_specs=[pl.BlockSpec((None, n1, _HY_WIDTH), lambda c, j: (j, 0, 0)),
                  pl.BlockSpec((_HY_WIDTH, td), lambda c, j: (0, fwd_off + c)),
                  pl.BlockSpec((_HY_WIDTH, td), lambda c, j: (0, bwd_off + c)),
                  pl.BlockSpec((1, td), lambda c, j: (0, c)),
                  pl.BlockSpec((None, r, n1), lambda c, j: (j, 0, 0))],
        out_specs=[pl.BlockSpec((None, r, td), lambda c, j: (j, 0, c)),
                   pl.BlockSpec((1, td), lambda c, j: (0, c))],
        out_shape=[jax.ShapeDtypeStruct((n2, r, d), _BF16), jax.ShapeDtypeStruct((1, d), _F32)],
        compiler_params=_cparams(("parallel", "arbitrary")),
        name="hyena_kernel_stage_a",
    )(z, fc4_w.astype(_F32), fc4_w.astype(_F32), deltas, mats)


def _hyena_long_conv(up, seq_len, d, hp):
    n1, n2 = _fft_plan(seq_len)
    kp = _kept_k1(n1)
    tabs = _dft_tables(seq_len)
    deltas = jnp.abs(jnp.linspace(_HY_MIN_DECAY, _HY_MAX_DECAY, d, dtype=_F32)).reshape(1, d)
    z = _filter_mlp(seq_len, hp["fc1_w"], hp["fc1_b"], hp["fc2_w"], hp["fc2_b"], hp["fc3_w"], hp["fc3_b"],
                    hp["sin_freq"])
    y, y_off = up, 0
    g = _stage_a_fwd(y, y_off, tabs["fwd_a_half"], d)
    for order in range(_HY_ORDER):
        kg, l1 = _kernel_stage_a(z, hp["fc4_w"], order, deltas, tabs["fwd_a_full"], seq_len, d)
        h = _stage_b_conv(_swap_stage_layout(g, n2, kp), _swap_stage_layout(kg[None], n2, kp)[0], l1,
                          tabs["fwd_b"], tabs["inv_b"])
        ht = _swap_stage_layout(h, kp, n2)
        if order == _HY_ORDER - 1:
            y = _stage_a_inv(ht, tabs["inv_a"], y, y_off, up, (order + 1) * d, hp["skip"][order], _BF16)
        else:
            y, g = _stage_a_inv(ht, tabs["inv_a"], y, y_off, up, (order + 1) * d, hp["skip"][order], _F32,
                                next_mats=tabs["fwd_a_half"])
        y_off = 0
    return y.reshape(-1, d)


def kernel(x_prompt, x_sample, norm_mix, norm_ffn, norm_final, hg_w_in, hg_lb_logits, hg_out_norm, hg_w_out,
           hy_w_in, hy_conv_w, hy_conv_b, hy_fc1_w, hy_fc1_b, hy_fc2_w, hy_fc2_b, hy_fc3_w, hy_fc3_b, hy_fc4_w,
           hy_sin_freq, hy_skip, hy_w_out, ffn_w_gate, ffn_w_up, ffn_w_down):
    b1, t1, d = x_prompt.shape
    b2, t2, _ = x_sample.shape
    groups = ((b1, t1), (b2, t2))
    seg_lens = tuple([t1] * b1 + [t2] * b2)
    x = jnp.concatenate([x_prompt.reshape(b1 * t1, d), x_sample.reshape(b2 * t2, d)], axis=0)
    depth = norm_mix.shape[0]

    for layer in range(depth):
        slot = layer // 2
        h = _rmsnorm(x, norm_mix[layer], _BF16)
        if layer % 2 == 0:
            proj = _matmul(h, (hg_w_in,), slot, mode="plain", tm=2048, rows_resident=True)
            o_f = _hgrn_scan(proj, hg_lb_logits[0], seg_lens, reverse=False, slot=slot)
            o = _hgrn_scan(proj, hg_lb_logits[1], seg_lens, reverse=True, slot=slot, o_fwd=o_f,
                           gain=hg_out_norm[slot])
            x = _matmul(o, (hg_w_out,), slot, mode="residual", residual=x)
        else:
            hp = dict(fc1_w=hy_fc1_w[slot], fc1_b=hy_fc1_b[slot], fc2_w=hy_fc2_w[slot], fc2_b=hy_fc2_b[slot],
                      fc3_w=hy_fc3_w[slot], fc3_b=hy_fc3_b[slot], fc4_w=hy_fc4_w[slot],
                      sin_freq=hy_sin_freq[slot], skip=hy_skip[slot])
            row0 = 0
            for bg, tg in groups:
                rows = bg * tg
                hg = _permute_rows(h[row0:row0 + rows], ((bg, tg),))
                u = _matmul(hg, (hy_w_in,), slot, mode="plain", tm=2048, rows_resident=True)
                up = _conv3(u, 0, bg, tg, hy_conv_w[slot], hy_conv_b[slot])
                y = _permute_rows(_hyena_long_conv(up, tg, d, hp), ((bg, tg),), inverse=True)
                x = _matmul(y, (hy_w_out,), slot, mode="residual", residual=x, res_row0=row0)
                row0 += rows
        h2 = _rmsnorm(x, norm_ffn[layer], _BF16)
        a = _matmul(h2, (ffn_w_gate, ffn_w_up), layer, mode="swiglu", out_dtype=_BF16, tm=2048, tn=256,
                    rows_resident=True)
        x = _matmul(a, (ffn_w_down,), layer, mode="residual", residual=x, tm=256, single_buffer_w=True)

    return (_rmsnorm(x, norm_final, _F32, 0, b1 * t1).reshape(b1, t1, d),
            _rmsnorm(x, norm_final, _F32, b1 * t1, b2 * t2).reshape(b2, t2, d))
```

```python
import functools
import math

import numpy as np
import jax
import jax.numpy as jnp
from jax import lax
from jax.experimental import pallas as pl
from jax.experimental.pallas import tpu as pltpu

_F32 = jnp.float32
_BF16 = jnp.bfloat16

_EPS = 1e-6
_HEAD_DIM = 128
_CHUNK = 64
_SUB = 16
_EXP_CLAMP = 115.0
_HY_ORDER = 2
_HY_EMB = 33
_HY_BANDS = (_HY_EMB - 1) // 2
_HY_WIDTH = 64
_HY_MAX_DECAY = math.log(1e-2) / 0.3
_HY_MIN_DECAY = math.log(1e-2) / 1.5
_V7X_VMEM_LIMIT = 56 * 1024 * 1024


def _cparams(sem):
    return pltpu.CompilerParams(dimension_semantics=sem, vmem_limit_bytes=_V7X_VMEM_LIMIT)


def _tile(dim, want):
    if dim <= want:
        return dim
    t = want
    while t >= 128:
        if dim % t == 0:
            return t
        t -= 128
    return dim


def _row_tile(rows, want):
    t = min(rows, want)
    while rows % t:
        t //= 2
    return t


def _rmsnorm_body(x_ref, g_ref, o_ref):
    x = x_ref[...]
    ms = jnp.mean(x * x, axis=-1, keepdims=True)
    o_ref[...] = (x * lax.rsqrt(ms + _EPS) * g_ref[...]).astype(o_ref.dtype)


def _rmsnorm(x, gain, out_dtype, row0=0, rows=None):
    d = x.shape[1]
    m = x.shape[0] if rows is None else rows
    tm = _row_tile(math.gcd(m, row0) if row0 else m, 512)
    blk0 = row0 // tm
    return pl.pallas_call(
        _rmsnorm_body,
        grid=(m // tm,),
        in_specs=[pl.BlockSpec((tm, d), lambda i: (blk0 + i, 0)),
                  pl.BlockSpec((1, d), lambda i: (0, 0))],
        out_specs=pl.BlockSpec((tm, d), lambda i: (i, 0)),
        out_shape=jax.ShapeDtypeStruct((m, d), out_dtype),
        compiler_params=_cparams(("parallel",)),
        name="rmsnorm",
    )(x, gain.reshape(1, d).astype(_F32))


def _mm_body(*refs, mode, n_w, rows_resident):
    a_ref, w_refs = refs[0], refs[1:1 + n_w]
    if rows_resident:
        ws = [w_ref[...].astype(_BF16) for w_ref in w_refs]
    else:
        wb_refs = refs[len(refs) - n_w:]

        @pl.when(pl.program_id(1) == 0)
        def _():
            for w_ref, wb_ref in zip(w_refs, wb_refs):
                wb_ref[...] = w_ref[...].astype(_BF16)

        ws = [wb_ref[...] for wb_ref in wb_refs]

    a = a_ref[...]
    if mode == "swiglu":
        o_ref = refs[1 + n_w]
        g = jnp.dot(a, ws[0], preferred_element_type=_F32)
        u = jnp.dot(a, ws[1], preferred_element_type=_F32)
        o_ref[...] = (g * (0.5 * jnp.tanh(0.5 * g) + 0.5) * u).astype(o_ref.dtype)
    elif mode == "residual":
        r_ref, o_ref = refs[1 + n_w], refs[2 + n_w]
        o_ref[...] = r_ref[...] + jnp.dot(a, ws[0], preferred_element_type=_F32)
    else:
        o_ref = refs[1 + n_w]
        o_ref[...] = jnp.dot(a, ws[0], preferred_element_type=_F32).astype(o_ref.dtype)


def _matmul(a, ws, layer, *, mode, residual=None, res_row0=0, out_dtype=_F32, tm=1024, tn=512,
            rows_resident=False, single_buffer_w=False):
    m, k = a.shape
    n = ws[0].shape[2]
    tm = _row_tile(math.gcd(m, res_row0) if res_row0 else m, tm)
    tn = _tile(n, tn)
    if rows_resident:
        grid = (m // tm, n // tn)
        ij = lambda g0, g1: (g0, g1)
        a_mode, w_mode = dict(pipeline_mode=pl.Buffered(1)), {}
        scratch = []
    else:
        grid = (n // tn, m // tm)
        ij = lambda g0, g1: (g1, g0)
        a_mode, w_mode = {}, (dict(pipeline_mode=pl.Buffered(1)) if single_buffer_w else {})
        scratch = [pltpu.VMEM((k, tn), _BF16) for _ in ws]
    in_specs = [pl.BlockSpec((tm, k), lambda g0, g1: (ij(g0, g1)[0], 0), **a_mode)]
    in_specs += [pl.BlockSpec((None, k, tn), lambda g0, g1: (layer, 0, ij(g0, g1)[1]), **w_mode) for _ in ws]
    args = [a, *[w.astype(_F32) for w in ws]]
    out_spec = pl.BlockSpec((tm, tn), lambda g0, g1: ij(g0, g1))
    out_shape = jax.ShapeDtypeStruct((m, n), out_dtype)
    aliases = {}
    if mode == "residual":
        blk0 = res_row0 // tm
        out_spec = pl.BlockSpec((tm, tn), lambda g0, g1: (blk0 + ij(g0, g1)[0], ij(g0, g1)[1]))
        in_specs.append(out_spec)
        args.append(residual)
        out_shape = jax.ShapeDtypeStruct(residual.shape, residual.dtype)
        aliases = {len(args) - 1: 0}
    return pl.pallas_call(
        functools.partial(_mm_body, mode=mode, n_w=len(ws), rows_resident=rows_resident),
        grid=grid,
        in_specs=in_specs,
        out_specs=out_spec,
        out_shape=out_shape,
        scratch_shapes=scratch,
        input_output_aliases=aliases,
        compiler_params=_cparams(("parallel", "arbitrary")),
        name="matmul_" + mode,
    )(*args)


def _scan_body(*refs, reverse, n_chunks, hb, reset_blocks, n_tblocks, slot, epilogue):
    if epilogue:
        q_ref, fl_ref, v_ref, lbl_ref, of_ref, gate_ref, gain_ref, o_ref, st_ref = refs
    else:
        q_ref, fl_ref, v_ref, lbl_ref, o_ref, st_ref = refs
    c_len, hd = _CHUNK, _HEAD_DIM

    t = pl.program_id(1)
    blk = (n_tblocks - 1 - t) if reverse else t
    reset = blk == reset_blocks[0]
    for r in reset_blocks[1:]:
        reset = jnp.logical_or(reset, blk == r)

    @pl.when(reset)
    def _():
        st_ref[...] = jnp.zeros_like(st_ref)

    lbl = lbl_ref[...]
    e = jnp.exp(lbl - jnp.max(lbl, axis=0, keepdims=True))
    lb_all = jnp.sum(e[: slot + 1], axis=0, keepdims=True) / jnp.sum(e, axis=0, keepdims=True)

    row = lax.broadcasted_iota(jnp.int32, (c_len, c_len), 0)
    col = lax.broadcasted_iota(jnp.int32, (c_len, c_len), 1)
    keep = (row <= col) if reverse else (row >= col)
    tri = keep.astype(_BF16)
    n_sub = c_len // _SUB
    w = hb * hd
    nt = (((1,), (1,)), ((), ()))
    tn = (((0,), (0,)), ((), ()))
    heads = [slice(h * hd, (h + 1) * hd) for h in range(hb)]

    def sigmoid(z):
        return 0.5 * jnp.tanh(0.5 * z) + 0.5

    def pad_rows(part, lo_r, dtype):
        pads = [jnp.zeros((lo_r, part.shape[1]), dtype), part,
                jnp.zeros((c_len - lo_r - part.shape[0], part.shape[1]), dtype)]
        return jnp.concatenate([p for p in pads if p.shape[0]], axis=0)

    def chunk_step(ci, carry):
        c = (n_chunks - 1 - ci) if reverse else ci
        r0 = pl.multiple_of(c * c_len, c_len)
        qr = q_ref[pl.ds(r0, c_len), :]
        q = qr * sigmoid(qr)
        f = lb_all + (1.0 - lb_all) * sigmoid(fl_ref[pl.ds(r0, c_len), :])
        k = 1.0 - f
        vb = v_ref[pl.ds(r0, c_len), :].astype(_BF16)
        lf = jnp.log2(f)
        hi = lf.astype(_BF16)
        rem = lf - hi.astype(_F32)
        mid = rem.astype(_BF16)
        lo = (rem - mid.astype(_F32)).astype(_BF16)
        cum = (jnp.dot(tri, hi, preferred_element_type=_F32) + jnp.dot(tri, mid, preferred_element_type=_F32)
               + jnp.dot(tri, lo, preferred_element_type=_F32))
        edge = cum[0:1] if reverse else cum[c_len - 1:c_len]
        q_in = (q * jnp.exp2(cum)).astype(_BF16)
        k_end = (k * jnp.exp2(edge - cum)).astype(_BF16)
        st_decay = jnp.exp2(edge)
        refs = []
        for i in range(n_sub):
            if reverse:
                refs.append(cum[(i + 1) * _SUB:(i + 1) * _SUB + 1] if i < n_sub - 1 else jnp.zeros((1, w), _F32))
            else:
                refs.append(cum[i * _SUB - 1:i * _SUB] if i > 0 else jnp.zeros((1, w), _F32))
        ref_rows = jnp.concatenate([jnp.broadcast_to(r, (_SUB, w)) for r in refs], axis=0)
        a = (q * jnp.exp2(cum - ref_rows)).astype(_BF16)
        bs = []
        for i in range(n_sub):
            lo_r, hi_r = (i * _SUB, c_len) if reverse else (0, (i + 1) * _SUB)
            part = k[lo_r:hi_r] * jnp.exp2(jnp.minimum(refs[i] - cum[lo_r:hi_r], _EXP_CLAMP))
            bs.append(pad_rows(part.astype(_BF16), lo_r, _BF16))

        sts = [st_ref[h] for h in range(hb)]
        o_inter = [lax.dot_general(q_in[:, ln], sts[h].astype(_BF16), nt, preferred_element_type=_F32)
                   for h, ln in enumerate(heads)]
        a_cat = [jnp.concatenate([pad_rows(a[i * _SUB:(i + 1) * _SUB, ln], i * _SUB, _BF16) for i in range(n_sub)],
                                 axis=1) for ln in heads]
        b_cat = [jnp.concatenate([bs[i][:, ln] for i in range(n_sub)], axis=1) for ln in heads]
        scores = [lax.dot_general(a_cat[h], b_cat[h], nt, preferred_element_type=_F32) for h in range(hb)]
        scores = [jnp.where(keep, s, 0.0).astype(_BF16) for s in scores]
        o_intra = [jnp.dot(scores[h], vb[:, ln], preferred_element_type=_F32) for h, ln in enumerate(heads)]
        st_new = [lax.dot_general(vb[:, ln], k_end[:, ln], tn, preferred_element_type=_F32) for ln in heads]
        for h, ln in enumerate(heads):
            st_ref[h] = sts[h] * st_decay[:, ln] + st_new[h]
        o = jnp.concatenate([o_inter[h] + o_intra[h] for h in range(hb)], axis=1)
        if epilogue:
            tot = of_ref[pl.ds(r0, c_len), :] + o
            ys = []
            for ln in heads:
                th = tot[:, ln]
                ys.append(th * lax.rsqrt(jnp.mean(th * th, axis=-1, keepdims=True) + _EPS))
            g = gate_ref[pl.ds(r0, c_len), :]
            y = jnp.concatenate(ys, axis=1) * gain_ref[...]
            o_ref[pl.ds(r0, c_len), :] = (y * (g * sigmoid(g))).astype(o_ref.dtype)
        else:
            o_ref[pl.ds(r0, c_len), :] = o
        return carry

    lax.fori_loop(0, n_chunks, chunk_step, 0)


def _hgrn_scan(proj, lb_logits, seg_lens, *, reverse, slot, o_fwd=None, gain=None):
    m, d5 = proj.shape
    d = d5 // 5
    hb = next(n for n in (32, 16, 8, 4, 2, 1) if d % (n * _HEAD_DIM) == 0)
    w = hb * _HEAD_DIM
    nhb = d // w
    tb = _row_tile(math.gcd(*seg_lens) if len(seg_lens) > 1 else seg_lens[0], 128)
    n_t = m // tb
    bounds = np.cumsum([0] + list(seg_lens)) // tb
    reset_blocks = tuple(int(b) - 1 for b in bounds[1:]) if reverse else tuple(int(b) for b in bounds[:-1])
    epilogue = o_fwd is not None

    def tmap(t):
        return (n_t - 1 - t) if reverse else t

    def stream(s):
        return pl.BlockSpec((tb, w), lambda h, t, s=s: (tmap(t), s * nhb + h))

    n_slots = lb_logits.shape[0]
    in_specs = [stream(0), stream(2 if reverse else 1), stream(3),
                pl.BlockSpec((n_slots, w), lambda h, t: (0, h))]
    args = [proj, proj, proj, lb_logits.astype(_F32)]
    if epilogue:
        in_specs += [pl.BlockSpec((tb, w), lambda h, t: (tmap(t), h)), stream(4),
                     pl.BlockSpec((1, w), lambda h, t: (0, h))]
        args += [o_fwd, proj, gain.reshape(1, d).astype(_F32)]
    body = functools.partial(_scan_body, reverse=reverse, n_chunks=tb // _CHUNK, hb=hb,
                             reset_blocks=reset_blocks, n_tblocks=n_t, slot=slot, epilogue=epilogue)
    return pl.pallas_call(
        body,
        grid=(nhb, n_t),
        in_specs=in_specs,
        out_specs=pl.BlockSpec((tb, w), lambda h, t: (tmap(t), h)),
        out_shape=jax.ShapeDtypeStruct((m, d), _BF16 if epilogue else _F32),
        scratch_shapes=[pltpu.VMEM((hb, _HEAD_DIM, _HEAD_DIM), _F32)],
        compiler_params=_cparams(("parallel", "arbitrary")),
        name="hgrn_scan_bwd" if reverse else "hgrn_scan_fwd",
    )(*args)


def _inproj_conv3_body(prev_ref, cur_ref, nxt_ref, w_ref, cw_ref, cb_ref, o_ref, *, h1, tiles_per_seq):
    i_loc = lax.rem(pl.program_id(0), tiles_per_seq)
    wb = w_ref[...].astype(_BF16)
    u = jnp.dot(cur_ref[...], wb, preferred_element_type=_F32)
    u_prev = jnp.dot(prev_ref[...], wb, preferred_element_type=_F32)
    u_next = jnp.dot(nxt_ref[...], wb, preferred_element_type=_F32)
    rows = lax.broadcasted_iota(jnp.int32, u_prev.shape, 0)
    u_prev = jnp.where(i_loc == 0, jnp.where(rows == 0, 0.0, pltpu.roll(u_prev, 1, 0)), u_prev)
    u_next = jnp.where(i_loc == tiles_per_seq - 1,
                       jnp.where(rows == h1 - 1, 0.0, pltpu.roll(u_next, h1 - 1, 0)), u_next)
    if u.shape[0] > h1:
        up = jnp.concatenate([u_prev, u[:-h1]], axis=0)
        dn = jnp.concatenate([u[h1:], u_next], axis=0)
    else:
        up, dn = u_prev, u_next
    cw = cw_ref[...]
    o_ref[...] = up * cw[0:1] + u * cw[1:2] + dn * cw[2:3] + cb_ref[...]


def _inproj_conv3(hg, w, layer, conv_w, conv_b, batch, seq_len, tn=512):
    m, k = hg.shape
    n = w.shape[2]
    n1, n2 = _fft_plan(seq_len)
    h1 = n1 // 2
    jb = _row_tile(n2, max(1, 2048 // h1))
    tm = jb * h1
    tps = n2 // jb
    tn = _tile(n, tn)

    def seq_slab(i, s):
        return lax.div(i, tps) * n2 + s

    def prev_slab(i, j):
        il = lax.rem(i, tps)
        return (seq_slab(i, jnp.where(il == 0, n2, il * jb) - 1), 0)

    def next_slab(i, j):
        il = lax.rem(i, tps)
        return (seq_slab(i, jnp.where(il == tps - 1, 0, (il + 1) * jb)), 0)

    body = functools.partial(_inproj_conv3_body, h1=h1, tiles_per_seq=tps)
    out = pl.pallas_call(
        body,
        grid=(m // tm, n // tn),
        in_specs=[pl.BlockSpec((h1, k), prev_slab),
                  pl.BlockSpec((tm, k), lambda i, j: (i, 0), pipeline_mode=pl.Buffered(1)),
                  pl.BlockSpec((h1, k), next_slab),
                  pl.BlockSpec((None, k, tn), lambda i, j: (layer, 0, j)),
                  pl.BlockSpec((3, tn), lambda i, j: (0, j)),
                  pl.BlockSpec((1, tn), lambda i, j: (0, j))],
        out_specs=pl.BlockSpec((tm, tn), lambda i, j: (i, j)),
        out_shape=jax.ShapeDtypeStruct((m, n), _F32),
        compiler_params=_cparams(("parallel", "arbitrary")),
        name="hyena_inproj_conv3",
    )(hg, hg, hg, w.astype(_F32), conv_w.astype(_F32), conv_b.reshape(1, n).astype(_F32))
    return out.reshape(batch, n2, h1, n)


def _permute_rows(x, groups, inverse=False):
    parts, row0 = [], 0
    for batch, seq_len in groups:
        n1, n2 = _fft_plan(seq_len)
        shape = (batch, n2, n1 // 2, -1) if inverse else (batch, n1 // 2, n2, -1)
        rows = batch * seq_len
        parts.append(x[row0:row0 + rows].reshape(shape).transpose(0, 2, 1, 3).reshape(rows, -1))
        row0 += rows
    return jnp.concatenate(parts, axis=0)


def _fft_plan(seq_len):
    n = 2 * seq_len
    n1 = 256 if n >= 32768 else 128
    while n // n1 < 8:
        n1 //= 2
    return n1, n // n1


def _kept_k1(n1):
    return n1 // 2 + 8


def _dft_tables(seq_len):
    n = 2 * seq_len
    n1, n2 = _fft_plan(seq_len)
    h1 = n1 // 2
    kp = _kept_k1(n1)
    i2 = lax.broadcasted_iota(jnp.int32, (n2, kp, n1), 0)
    k1 = lax.broadcasted_iota(jnp.int32, (n2, kp, n1), 1)
    i1 = lax.broadcasted_iota(jnp.int32, (n2, kp, n1), 2)
    phase = (i2 * k1 + n2 * (i1 * k1)) % n
    ang = phase.astype(_F32) * (2.0 * math.pi / n)
    cos = jnp.where(k1 <= h1, jnp.cos(ang), 0.0)
    sin = jnp.where(k1 <= h1, jnp.sin(ang), 0.0)
    fwd_a = jnp.concatenate([cos, -sin], axis=1).astype(_BF16)
    weight = jnp.where((k1 == 0) | (k1 == h1), 1.0, 2.0) * (1.0 / n)
    cos_t, sin_t = (cos * weight).transpose(0, 2, 1)[:, :h1], (sin * weight).transpose(0, 2, 1)[:, :h1]
    inv_a = jnp.concatenate([cos_t, -sin_t], axis=2).astype(_BF16)
    idx = np.arange(n2)
    ang_b = 2.0 * np.pi * ((idx[:, None] * idx[None, :]) % n2) / n2
    cb, sb = np.cos(ang_b), np.sin(ang_b)
    fwd_b = jnp.asarray(np.block([[cb, sb], [-sb, cb]]), _BF16)
    inv_b = jnp.asarray(np.block([[cb, -sb], [sb, cb]]), _BF16)
    fwd_a_half = fwd_a if h1 % 128 == 0 else fwd_a[:, :, :h1]
    return dict(fwd_a_full=fwd_a, fwd_a_half=fwd_a_half, inv_a=inv_a, fwd_b=fwd_b, inv_b=inv_b)


def _swap_stage_layout(x, groups, rows):
    b, _, _, d = x.shape
    return x.reshape(b, groups, 2, rows, d).transpose(0, 3, 2, 1, 4).reshape(b, rows, 2 * groups, d)


def _stage_a_fwd_body(x_ref, m_ref, o_ref):
    o_ref[...] = jnp.dot(m_ref[...], x_ref[...].astype(_BF16), preferred_element_type=_F32).astype(o_ref.dtype)


def _stage_a_fwd(xp, col_off, mats, d):
    b, n2, h1, _ = xp.shape
    r = mats.shape[1]
    td = _tile(d, 4096)
    off = col_off // td
    return pl.pallas_call(
        _stage_a_fwd_body,
        grid=(b, n2, d // td),
        in_specs=[pl.BlockSpec((None, None, h1, td), lambda bb, j, c: (bb, j, 0, off + c)),
                  pl.BlockSpec((None, r, h1), lambda bb, j, c: (j, 0, 0))],
        out_specs=pl.BlockSpec((None, None, r, td), lambda bb, j, c: (bb, j, 0, c)),
        out_shape=jax.ShapeDtypeStruct((b, n2, r, d), _BF16),
        compiler_params=_cparams(("parallel", "parallel", "parallel")),
        name="hyena_stage_a_fwd",
    )(xp, mats)


def _stage_b_conv_body(g_ref, kg_ref, l1_ref, fb_ref, fbi_ref, o_ref, ks_ref):
    n2 = g_ref.shape[0] // 2
    fb = fb_ref[...]
    x = jnp.dot(fb, g_ref[...], preferred_element_type=_F32)

    @pl.when(pl.program_id(2) == 0)
    def _():
        ks_ref[...] = jnp.dot(fb, kg_ref[...], preferred_element_type=_F32) * (1.0 / (l1_ref[...] + _EPS))

    xr, xi = x[:n2], x[n2:]
    kr, ki = ks_ref[:n2, :], ks_ref[n2:, :]
    y = jnp.concatenate([xr * kr - xi * ki, xr * ki + xi * kr], axis=0).astype(_BF16)
    o_ref[...] = jnp.dot(fbi_ref[...], y, preferred_element_type=_F32).astype(o_ref.dtype)


def _stage_b_conv(g, kg, l1, fwd_b, inv_b):
    b, n1, r, d = g.shape
    td = _tile(d, 4096)
    return pl.pallas_call(
        _stage_b_conv_body,
        grid=(n1, d // td, b),
        in_specs=[pl.BlockSpec((None, None, r, td), lambda k, c, bb: (bb, k, 0, c)),
                  pl.BlockSpec((None, r, td), lambda k, c, bb: (k, 0, c)),
                  pl.BlockSpec((1, td), lambda k, c, bb: (0, c)),
                  pl.BlockSpec((r, r), lambda k, c, bb: (0, 0)),
                  pl.BlockSpec((r, r), lambda k, c, bb: (0, 0))],
        out_specs=pl.BlockSpec((None, None, r, td), lambda k, c, bb: (bb, k, 0, c)),
        out_shape=jax.ShapeDtypeStruct((b, n1, r, d), _BF16),
        scratch_shapes=[pltpu.VMEM((r, td), _F32)],
        compiler_params=_cparams(("parallel", "parallel", "arbitrary")),
        name="hyena_stage_b_conv",
    )(g, kg, l1, fwd_b, inv_b)


def _stage_a_inv_body(h_ref, m_ref, y_ref, g_ref, s_ref, *rest):
    conv = jnp.dot(m_ref[...], h_ref[...], preferred_element_type=_F32)
    y_new = g_ref[...] * (conv + y_ref[...] * s_ref[...])
    if len(rest) == 1:
        rest[0][...] = y_new.astype(rest[0].dtype)
    else:
        fwd_ref, o_ref, gt_ref = rest
        o_ref[...] = y_new
        gt_ref[...] = jnp.dot(fwd_ref[...], y_new.astype(_BF16), preferred_element_type=_F32).astype(gt_ref.dtype)


def _stage_a_inv(ht, mats, yp, y_off, gp, g_off, skip, out_dtype, next_mats=None):
    b, n2, r, d = ht.shape
    h1 = mats.shape[1]
    td = _tile(d, 4096)
    yo, go = y_off // td, g_off // td
    in_specs = [pl.BlockSpec((None, None, r, td), lambda bb, j, c: (bb, j, 0, c)),
                pl.BlockSpec((None, h1, r), lambda bb, j, c: (j, 0, 0)),
                pl.BlockSpec((None, None, h1, td), lambda bb, j, c: (bb, j, 0, yo + c)),
                pl.BlockSpec((None, None, h1, td), lambda bb, j, c: (bb, j, 0, go + c)),
                pl.BlockSpec((1, td), lambda bb, j, c: (0, c))]
    args = [ht, mats, yp, gp, skip.reshape(1, d).astype(_F32)]
    out_specs = pl.BlockSpec((None, None, h1, td), lambda bb, j, c: (bb, j, 0, c))
    out_shape = jax.ShapeDtypeStruct((b, n2, h1, d), out_dtype)
    if next_mats is not None:
        in_specs.append(pl.BlockSpec((None, r, h1), lambda bb, j, c: (j, 0, 0)))
        args.append(next_mats)
        out_specs = [out_specs, pl.BlockSpec((None, None, r, td), lambda bb, j, c: (bb, j, 0, c))]
        out_shape = [out_shape, jax.ShapeDtypeStruct((b, n2, r, d), _BF16)]
    return pl.pallas_call(
        _stage_a_inv_body,
        grid=(b, n2, d // td),
        in_specs=in_specs,
        out_specs=out_specs,
        out_shape=out_shape,
        compiler_params=_cparams(("parallel", "parallel", "parallel")),
        name="hyena_stage_a_inv",
    )(*args)


def _filter_pos(shape, j0, n1, n2, seq_len):
    r = lax.broadcasted_iota(jnp.int32, shape, 0)
    i1 = jnp.bitwise_and(r, n1 - 1)
    j = j0 + lax.shift_right_logical(r, n1.bit_length() - 1)
    n = n2 * i1 + j
    return jnp.where(i1 < n1 // 2, n, 2 * seq_len - n)


def _filter_mlp_body(band_ref, w1_ref, b1_ref, w2_ref, b2_ref, w3_ref, b3_ref, fr_ref, o_ref,
                     *, jb, n1, n2, seq_len):
    hi = lax.Precision.HIGHEST
    rows = jb * n1
    p = _filter_pos((rows, 128), pl.program_id(0) * jb, n1, n2, seq_len).astype(_F32)
    lane = lax.broadcasted_iota(jnp.int32, (rows, 128), 1)
    ang = (2.0 * math.pi / seq_len) * p * band_ref[...]
    feat = jnp.where(lane == 0, p * (1.0 / (seq_len - 1)),
                     jnp.where(lane <= _HY_BANDS, jnp.cos(ang),
                               jnp.where(lane <= 2 * _HY_BANDS, -jnp.sin(ang), 0.0)))
    fr = fr_ref[...]
    z = jnp.sin(fr * (jnp.dot(feat, w1_ref[...], preferred_element_type=_F32, precision=hi) + b1_ref[...]))
    z = jnp.sin(fr * (jnp.dot(z, w2_ref[...], preferred_element_type=_F32, precision=hi) + b2_ref[...]))
    z = jnp.sin(fr * (jnp.dot(z, w3_ref[...], preferred_element_type=_F32, precision=hi) + b3_ref[...]))
    o_ref[...] = z.reshape(jb, n1, _HY_WIDTH)


def _filter_mlp(seq_len, fc1_w, fc1_b, fc2_w, fc2_b, fc3_w, fc3_b, sin_freq):
    n1, n2 = _fft_plan(seq_len)
    jb = _row_tile(n2, 8)
    wd = _HY_WIDTH
    bands = jnp.linspace(1e-4, _HY_BANDS - 1, _HY_BANDS, dtype=_F32)
    band_row = jnp.zeros((1, 128), _F32).at[0, 1:1 + _HY_BANDS].set(bands).at[0, 1 + _HY_BANDS:_HY_EMB].set(bands)
    w1 = jnp.zeros((128, wd), _F32).at[:_HY_EMB].set(fc1_w.astype(_F32))
    const = lambda shape: pl.BlockSpec(shape, lambda i: (0,) * len(shape))
    row = lambda a: a.reshape(1, wd).astype(_F32)
    body = functools.partial(_filter_mlp_body, jb=jb, n1=n1, n2=n2, seq_len=seq_len)
    return pl.pallas_call(
        body,
        grid=(n2 // jb,),
        in_specs=[const((1, 128)), const((128, wd)), const((1, wd)), const((wd, wd)), const((1, wd)),
                  const((wd, wd)), const((1, wd)), const((1, wd))],
        out_specs=pl.BlockSpec((jb, n1, wd), lambda i: (i, 0, 0)),
        out_shape=jax.ShapeDtypeStruct((n2, n1, wd), _F32),
        compiler_params=_cparams(("parallel",)),
        name="hyena_filter_mlp",
    )(band_row, w1, row(fc1_b), fc2_w.astype(_F32), row(fc2_b), fc3_w.astype(_F32), row(fc3_b), row(sin_freq))


def _kernel_stage_a_body(z_ref, wf_ref, wb_ref, dl_ref, m_ref, o_ref, l1_ref, *, n1, n2, seq_len):
    j = pl.program_id(1)
    h1 = n1 // 2
    td = wf_ref.shape[1]
    p = _filter_pos((n1, td), j, n1, n2, seq_len)
    window = jnp.exp(-(p.astype(_F32) * (1.0 / (seq_len - 1))) * dl_ref[...])
    z = z_ref[...]
    zr = lax.broadcasted_iota(jnp.int32, (n1, 2 * _HY_WIDTH), 0)
    zc = lax.broadcasted_iota(jnp.int32, (n1, 2 * _HY_WIDTH), 1)
    zz = jnp.where((zr < h1) == (zc < _HY_WIDTH), jnp.concatenate([z, z], axis=1), 0.0)
    wcat = jnp.concatenate([wf_ref[...], wb_ref[...]], axis=0)
    kern = jnp.dot(zz.astype(_BF16), wcat.astype(_BF16), preferred_element_type=_F32) * window
    kern = jnp.where(p == seq_len, 0.0, kern)

    @pl.when(j == 0)
    def _():
        l1_ref[...] = jnp.zeros_like(l1_ref)

    l1_ref[...] += jnp.sum(jnp.abs(kern), axis=0, keepdims=True)
    o_ref[...] = jnp.dot(m_ref[...], kern.astype(_BF16), preferred_element_type=_F32).astype(o_ref.dtype)


def _kernel_stage_a(z, fc4_w, order, deltas, mats, seq_len, d):
    n1, n2 = _fft_plan(seq_len)
    r = mats.shape[1]
    td = _tile(d, 2048)
    nd = d // td
    fwd_off, bwd_off = order * 2 * nd, order * 2 * nd + nd
    body = functools.partial(_kernel_stage_a_body, n1=n1, n2=n2, seq_len=seq_len)
    return pl.pallas_call(
        body,
        grid=(nd, n2),
        in_specs=[pl.BlockSpec((None, n1, _HY_WIDTH), lambda c, j: (j, 0, 0)),
                  pl.BlockSpec((_HY_WIDTH, td), lambda c, j: (0, fwd_off + c)),
                  pl.BlockSpec((_HY_WIDTH, td), lambda c, j: (0, bwd_off + c)),
                  pl.BlockSpec((1, td), lambda c, j: (0, c)),
                  pl.BlockSpec((None, r, n1), lambda c, j: (j, 0, 0))],
        out_specs=[pl.BlockSpec((None, r, td), lambda c, j: (j, 0, c)),
                   pl.BlockSpec((1, td), lambda c, j: (0, c))],
        out_shape=[jax.ShapeDtypeStruct((n2, r, d), _BF16), jax.ShapeDtypeStruct((1, d), _F32)],
        compiler_params=_cparams(("parallel", "arbitrary")),
        name="hyena_kernel_stage_a",
    )(z, fc4_w.astype(_F32), fc4_w.astype(_F32), deltas, mats)


def _hyena_long_conv(up, seq_len, d, hp):
    n1, n2 = _fft_plan(seq_len)
    kp = _kept_k1(n1)
    tabs = _dft_tables(seq_len)
    deltas = jnp.abs(jnp.linspace(_HY_MIN_DECAY, _HY_MAX_DECAY, d, dtype=_F32)).reshape(1, d)
    z = _filter_mlp(seq_len, hp["fc1_w"], hp["fc1_b"], hp["fc2_w"], hp["fc2_b"], hp["fc3_w"], hp["fc3_b"],
                    hp["sin_freq"])
    y, y_off = up, 0
    g = _stage_a_fwd(y, y_off, tabs["fwd_a_half"], d)
    for order in range(_HY_ORDER):
        kg, l1 = _kernel_stage_a(z, hp["fc4_w"], order, deltas, tabs["fwd_a_full"], seq_len, d)
        h = _stage_b_conv(_swap_stage_layout(g, n2, kp), _swap_stage_layout(kg[None], n2, kp)[0], l1,
                          tabs["fwd_b"], tabs["inv_b"])
        ht = _swap_stage_layout(h, kp, n2)
        if order == _HY_ORDER - 1:
            y = _stage_a_inv(ht, tabs["inv_a"], y, y_off, up, (order + 1) * d, hp["skip"][order], _BF16)
        else:
            y, g = _stage_a_inv(ht, tabs["inv_a"], y, y_off, up, (order + 1) * d, hp["skip"][order], _F32,
                                next_mats=tabs["fwd_a_half"])
        y_off = 0
    return y.reshape(-1, d)


def kernel(x_prompt, x_sample, norm_mix, norm_ffn, norm_final, hg_w_in, hg_lb_logits, hg_out_norm, hg_w_out,
           hy_w_in, hy_conv_w, hy_conv_b, hy_fc1_w, hy_fc1_b, hy_fc2_w, hy_fc2_b, hy_fc3_w, hy_fc3_b, hy_fc4_w,
           hy_sin_freq, hy_skip, hy_w_out, ffn_w_gate, ffn_w_up, ffn_w_down):
    b1, t1, d = x_prompt.shape
    b2, t2, _ = x_sample.shape
    groups = ((b1, t1), (b2, t2))
    seg_lens = tuple([t1] * b1 + [t2] * b2)
    x = jnp.concatenate([x_prompt.reshape(b1 * t1, d), x_sample.reshape(b2 * t2, d)], axis=0)
    depth = norm_mix.shape[0]

    for layer in range(depth):
        slot = layer // 2
        h = _rmsnorm(x, norm_mix[layer], _BF16)
        if layer % 2 == 0:
            proj = _matmul(h, (hg_w_in,), slot, mode="plain", tm=2048, rows_resident=True)
            o_f = _hgrn_scan(proj, hg_lb_logits[0], seg_lens, reverse=False, slot=slot)
            o = _hgrn_scan(proj, hg_lb_logits[1], seg_lens, reverse=True, slot=slot, o_fwd=o_f,
                           gain=hg_out_norm[slot])
            x = _matmul(o, (hg_w_out,), slot, mode="residual", residual=x)
        else:
            hp = dict(fc1_w=hy_fc1_w[slot], fc1_b=hy_fc1_b[slot], fc2_w=hy_fc2_w[slot], fc2_b=hy_fc2_b[slot],
                      fc3_w=hy_fc3_w[slot], fc3_b=hy_fc3_b[slot], fc4_w=hy_fc4_w[slot],
                      sin_freq=hy_sin_freq[slot], skip=hy_skip[slot])
            row0 = 0
            for bg, tg in groups:
                rows = bg * tg
                hg = _permute_rows(h[row0:row0 + rows], ((bg, tg),))
                up = _inproj_conv3(hg, hy_w_in, slot, hy_conv_w[slot], hy_conv_b[slot], bg, tg)
                y = _permute_rows(_hyena_long_conv(up, tg, d, hp), ((bg, tg),), inverse=True)
                x = _matmul(y, (hy_w_out,), slot, mode="residual", residual=x, res_row0=row0)
                row0 += rows
        h2 = _rmsnorm(x, norm_ffn[layer], _BF16)
        a = _matmul(h2, (ffn_w_gate, ffn_w_up), layer, mode="swiglu", out_dtype=_BF16, tm=2048, tn=256,
                    rows_resident=True)
        x = _matmul(a, (ffn_w_down,), layer, mode="residual", residual=x, tm=256, single_buffer_w=True)

    return (_rmsnorm(x, norm_final, _F32, 0, b1 * t1).reshape(b1, t1, d),
            _rmsnorm(x, norm_final, _F32, b1 * t1, b2 * t2).reshape(b2, t2, d))
```

```python
import functools
import math

import numpy as np
import jax
import jax.numpy as jnp
from jax import lax
from jax.experimental import pallas as pl
from jax.experimental.pallas import tpu as pltpu

_F32 = jnp.float32
_BF16 = jnp.bfloat16

_EPS = 1e-6
_HEAD_DIM = 128
_CHUNK = 64
_SUB = 16
_EXP_CLAMP = 115.0
_HY_ORDER = 2
_HY_EMB = 33
_HY_BANDS = (_HY_EMB - 1) // 2
_HY_WIDTH = 64
_HY_MAX_DECAY = math.log(1e-2) / 0.3
_HY_MIN_DECAY = math.log(1e-2) / 1.5
_V7X_VMEM_LIMIT = 56 * 1024 * 1024


def _cparams(sem):
    return pltpu.CompilerParams(dimension_semantics=sem, vmem_limit_bytes=_V7X_VMEM_LIMIT)


def _tile(dim, want):
    if dim <= want:
        return dim
    t = want
    while t >= 128:
        if dim % t == 0:
            return t
        t -= 128
    return dim


def _row_tile(rows, want):
    t = min(rows, want)
    while rows % t:
        t //= 2
    return t


def _rmsnorm_body(x_ref, g_ref, o_ref):
    x = x_ref[...]
    ms = jnp.mean(x * x, axis=-1, keepdims=True)
    o_ref[...] = (x * lax.rsqrt(ms + _EPS) * g_ref[...]).astype(o_ref.dtype)


def _rmsnorm(x, gain, out_dtype, row0=0, rows=None):
    d = x.shape[1]
    m = x.shape[0] if rows is None else rows
    tm = _row_tile(math.gcd(m, row0) if row0 else m, 512)
    blk0 = row0 // tm
    return pl.pallas_call(
        _rmsnorm_body,
        grid=(m // tm,),
        in_specs=[pl.BlockSpec((tm, d), lambda i: (blk0 + i, 0)),
                  pl.BlockSpec((1, d), lambda i: (0, 0))],
        out_specs=pl.BlockSpec((tm, d), lambda i: (i, 0)),
        out_shape=jax.ShapeDtypeStruct((m, d), out_dtype),
        compiler_params=_cparams(("parallel",)),
        name="rmsnorm",
    )(x, gain.reshape(1, d).astype(_F32))


def _mm_body(*refs, mode, n_w, rows_resident, n_in):
    a_ref, w_refs, o_ref = refs[0], refs[1:1 + n_w], refs[n_in]
    if rows_resident:
        ws = [w_ref[...].astype(_BF16) for w_ref in w_refs]
    else:
        wb_refs = refs[len(refs) - n_w:]

        @pl.when(pl.program_id(1) == 0)
        def _():
            for w_ref, wb_ref in zip(w_refs, wb_refs):
                wb_ref[...] = w_ref[...].astype(_BF16)

        ws = [wb_ref[...] for wb_ref in wb_refs]

    a = a_ref[...]
    if mode == "swiglu":
        g = jnp.dot(a, ws[0], preferred_element_type=_F32)
        u = jnp.dot(a, ws[1], preferred_element_type=_F32)
        o_ref[...] = (g * (0.5 * jnp.tanh(0.5 * g) + 0.5) * u).astype(o_ref.dtype)
    elif mode == "residual":
        o_ref[...] = refs[1 + n_w][...] + jnp.dot(a, ws[0], preferred_element_type=_F32)
    else:
        o_ref[...] = jnp.dot(a, ws[0], preferred_element_type=_F32).astype(o_ref.dtype)


def _matmul(a, ws, layer, *, mode, residual=None, res_row0=0, out_into=None, out_dtype=_F32, tm=1024, tn=512,
            rows_resident=False, single_buffer_w=False):
    m, k = a.shape
    n = ws[0].shape[2]
    row_offsets = [r for r in (res_row0, out_into[1] if out_into else 0) if r]
    tm = _row_tile(math.gcd(m, *row_offsets), tm)
    tn = _tile(n, tn)
    if rows_resident:
        grid = (m // tm, n // tn)
        ij = lambda g0, g1: (g0, g1)
        a_mode, w_mode = dict(pipeline_mode=pl.Buffered(1)), {}
        scratch = []
    else:
        grid = (n // tn, m // tm)
        ij = lambda g0, g1: (g1, g0)
        a_mode, w_mode = {}, (dict(pipeline_mode=pl.Buffered(1)) if single_buffer_w else {})
        scratch = [pltpu.VMEM((k, tn), _BF16) for _ in ws]
    in_specs = [pl.BlockSpec((tm, k), lambda g0, g1: (ij(g0, g1)[0], 0), **a_mode)]
    in_specs += [pl.BlockSpec((None, k, tn), lambda g0, g1: (layer, 0, ij(g0, g1)[1]), **w_mode) for _ in ws]
    args = [a, *[w.astype(_F32) for w in ws]]
    out_spec = pl.BlockSpec((tm, tn), lambda g0, g1: ij(g0, g1))
    out_shape = jax.ShapeDtypeStruct((m, n), out_dtype)
    aliases = {}
    if mode == "residual":
        def offset_spec(row0):
            blk0 = row0 // tm
            return pl.BlockSpec((tm, tn), lambda g0, g1: (blk0 + ij(g0, g1)[0], ij(g0, g1)[1]))

        in_specs.append(offset_spec(res_row0))
        args.append(residual)
        if out_into is None:
            out_spec = offset_spec(res_row0)
            out_shape = jax.ShapeDtypeStruct(residual.shape, residual.dtype)
            aliases = {len(args) - 1: 0}
        else:
            total_rows, row0, buffer = out_into
            out_spec = offset_spec(row0)
            out_shape = jax.ShapeDtypeStruct((total_rows, n), out_dtype)
            if buffer is not None:
                in_specs.append(pl.BlockSpec(memory_space=pl.ANY))
                args.append(buffer)
                aliases = {len(args) - 1: 0}
    return pl.pallas_call(
        functools.partial(_mm_body, mode=mode, n_w=len(ws), rows_resident=rows_resident,
                          n_in=len(args)),
        grid=grid,
        in_specs=in_specs,
        out_specs=out_spec,
        out_shape=out_shape,
        scratch_shapes=scratch,
        input_output_aliases=aliases,
        compiler_params=_cparams(("parallel", "arbitrary")),
        name="matmul_" + mode,
    )(*args)


def _scan_body(*refs, reverse, n_chunks, hb, reset_blocks, n_tblocks, slot, epilogue):
    if epilogue:
        q_ref, fl_ref, v_ref, lbl_ref, of_ref, gate_ref, gain_ref, o_ref, st_ref = refs
    else:
        q_ref, fl_ref, v_ref, lbl_ref, o_ref, st_ref = refs
    c_len, hd = _CHUNK, _HEAD_DIM

    t = pl.program_id(1)
    blk = (n_tblocks - 1 - t) if reverse else t
    reset = blk == reset_blocks[0]
    for r in reset_blocks[1:]:
        reset = jnp.logical_or(reset, blk == r)

    @pl.when(reset)
    def _():
        st_ref[...] = jnp.zeros_like(st_ref)

    lbl = lbl_ref[...]
    e = jnp.exp(lbl - jnp.max(lbl, axis=0, keepdims=True))
    lb_all = jnp.sum(e[: slot + 1], axis=0, keepdims=True) / jnp.sum(e, axis=0, keepdims=True)

    row = lax.broadcasted_iota(jnp.int32, (c_len, c_len), 0)
    col = lax.broadcasted_iota(jnp.int32, (c_len, c_len), 1)
    keep = (row <= col) if reverse else (row >= col)
    tri = keep.astype(_BF16)
    n_sub = c_len // _SUB
    w = hb * hd
    nt = (((1,), (1,)), ((), ()))
    tn = (((0,), (0,)), ((), ()))
    heads = [slice(h * hd, (h + 1) * hd) for h in range(hb)]

    def sigmoid(z):
        return 0.5 * jnp.tanh(0.5 * z) + 0.5

    def pad_rows(part, lo_r, dtype):
        pads = [jnp.zeros((lo_r, part.shape[1]), dtype), part,
                jnp.zeros((c_len - lo_r - part.shape[0], part.shape[1]), dtype)]
        return jnp.concatenate([p for p in pads if p.shape[0]], axis=0)

    def chunk_step(ci, carry):
        c = (n_chunks - 1 - ci) if reverse else ci
        r0 = pl.multiple_of(c * c_len, c_len)
        qr = q_ref[pl.ds(r0, c_len), :]
        q = qr * sigmoid(qr)
        f = lb_all + (1.0 - lb_all) * sigmoid(fl_ref[pl.ds(r0, c_len), :])
        k = 1.0 - f
        vb = v_ref[pl.ds(r0, c_len), :].astype(_BF16)
        lf = jnp.log2(f)
        hi = lf.astype(_BF16)
        rem = lf - hi.astype(_F32)
        mid = rem.astype(_BF16)
        lo = (rem - mid.astype(_F32)).astype(_BF16)
        cum = (jnp.dot(tri, hi, preferred_element_type=_F32) + jnp.dot(tri, mid, preferred_element_type=_F32)
               + jnp.dot(tri, lo, preferred_element_type=_F32))
        edge = cum[0:1] if reverse else cum[c_len - 1:c_len]
        q_in = (q * jnp.exp2(cum)).astype(_BF16)
        k_end = (k * jnp.exp2(edge - cum)).astype(_BF16)
        st_decay = jnp.exp2(edge)
        refs = []
        for i in range(n_sub):
            if reverse:
                refs.append(cum[(i + 1) * _SUB:(i + 1) * _SUB + 1] if i < n_sub - 1 else jnp.zeros((1, w), _F32))
            else:
                refs.append(cum[i * _SUB - 1:i * _SUB] if i > 0 else jnp.zeros((1, w), _F32))
        ref_rows = jnp.concatenate([jnp.broadcast_to(r, (_SUB, w)) for r in refs], axis=0)
        a = (q * jnp.exp2(cum - ref_rows)).astype(_BF16)
        bs = []
        for i in range(n_sub):
            lo_r, hi_r = (i * _SUB, c_len) if reverse else (0, (i + 1) * _SUB)
            part = k[lo_r:hi_r] * jnp.exp2(jnp.minimum(refs[i] - cum[lo_r:hi_r], _EXP_CLAMP))
            bs.append(pad_rows(part.astype(_BF16), lo_r, _BF16))

        sts = [st_ref[h] for h in range(hb)]
        o_inter = [lax.dot_general(q_in[:, ln], sts[h].astype(_BF16), nt, preferred_element_type=_F32)
                   for h, ln in enumerate(heads)]
        a_cat = [jnp.concatenate([pad_rows(a[i * _SUB:(i + 1) * _SUB, ln], i * _SUB, _BF16) for i in range(n_sub)],
                                 axis=1) for ln in heads]
        b_cat = [jnp.concatenate([bs[i][:, ln] for i in range(n_sub)], axis=1) for ln in heads]
        scores = [lax.dot_general(a_cat[h], b_cat[h], nt, preferred_element_type=_F32) for h in range(hb)]
        scores = [jnp.where(keep, s, 0.0).astype(_BF16) for s in scores]
        o_intra = [jnp.dot(scores[h], vb[:, ln], preferred_element_type=_F32) for h, ln in enumerate(heads)]
        st_new = [lax.dot_general(vb[:, ln], k_end[:, ln], tn, preferred_element_type=_F32) for ln in heads]
        for h, ln in enumerate(heads):
            st_ref[h] = sts[h] * st_decay[:, ln] + st_new[h]
        o = jnp.concatenate([o_inter[h] + o_intra[h] for h in range(hb)], axis=1)
        if epilogue:
            tot = of_ref[pl.ds(r0, c_len), :] + o
            ys = []
            for ln in heads:
                th = tot[:, ln]
                ys.append(th * lax.rsqrt(jnp.mean(th * th, axis=-1, keepdims=True) + _EPS))
            g = gate_ref[pl.ds(r0, c_len), :]
            y = jnp.concatenate(ys, axis=1) * gain_ref[...]
            o_ref[pl.ds(r0, c_len), :] = (y * (g * sigmoid(g))).astype(o_ref.dtype)
        else:
            o_ref[pl.ds(r0, c_len), :] = o
        return carry

    lax.fori_loop(0, n_chunks, chunk_step, 0)


def _hgrn_scan(proj, lb_logits, seg_lens, *, reverse, slot, o_fwd=None, gain=None):
    m, d5 = proj.shape
    d = d5 // 5
    hb = next(n for n in (32, 16, 8, 4, 2, 1) if d % (n * _HEAD_DIM) == 0)
    w = hb * _HEAD_DIM
    nhb = d // w
    tb = _row_tile(math.gcd(*seg_lens) if len(seg_lens) > 1 else seg_lens[0], 128)
    n_t = m // tb
    bounds = np.cumsum([0] + list(seg_lens)) // tb
    reset_blocks = tuple(int(b) - 1 for b in bounds[1:]) if reverse else tuple(int(b) for b in bounds[:-1])
    epilogue = o_fwd is not None

    def tmap(t):
        return (n_t - 1 - t) if reverse else t

    def stream(s):
        return pl.BlockSpec((tb, w), lambda h, t, s=s: (tmap(t), s * nhb + h))

    n_slots = lb_logits.shape[0]
    in_specs = [stream(0), stream(2 if reverse else 1), stream(3),
                pl.BlockSpec((n_slots, w), lambda h, t: (0, h))]
    args = [proj, proj, proj, lb_logits.astype(_F32)]
    if epilogue:
        in_specs += [pl.BlockSpec((tb, w), lambda h, t: (tmap(t), h)), stream(4),
                     pl.BlockSpec((1, w), lambda h, t: (0, h))]
        args += [o_fwd, proj, gain.reshape(1, d).astype(_F32)]
    body = functools.partial(_scan_body, reverse=reverse, n_chunks=tb // _CHUNK, hb=hb,
                             reset_blocks=reset_blocks, n_tblocks=n_t, slot=slot, epilogue=epilogue)
    return pl.pallas_call(
        body,
        grid=(nhb, n_t),
        in_specs=in_specs,
        out_specs=pl.BlockSpec((tb, w), lambda h, t: (tmap(t), h)),
        out_shape=jax.ShapeDtypeStruct((m, d), _BF16 if epilogue else _F32),
        scratch_shapes=[pltpu.VMEM((hb, _HEAD_DIM, _HEAD_DIM), _F32)],
        compiler_params=_cparams(("parallel", "arbitrary")),
        name="hgrn_scan_bwd" if reverse else "hgrn_scan_fwd",
    )(*args)


def _inproj_conv3_body(prev_ref, cur_ref, nxt_ref, w_ref, cw_ref, cb_ref, o_ref, *, h1, tiles_per_seq):
    i_loc = lax.rem(pl.program_id(0), tiles_per_seq)
    wb = w_ref[...].astype(_BF16)
    u = jnp.dot(cur_ref[...], wb, preferred_element_type=_F32)
    u_prev = jnp.dot(prev_ref[...], wb, preferred_element_type=_F32)
    u_next = jnp.dot(nxt_ref[...], wb, preferred_element_type=_F32)
    rows = lax.broadcasted_iota(jnp.int32, u_prev.shape, 0)
    u_prev = jnp.where(i_loc == 0, jnp.where(rows == 0, 0.0, pltpu.roll(u_prev, 1, 0)), u_prev)
    u_next = jnp.where(i_loc == tiles_per_seq - 1,
                       jnp.where(rows == h1 - 1, 0.0, pltpu.roll(u_next, h1 - 1, 0)), u_next)
    if u.shape[0] > h1:
        up = jnp.concatenate([u_prev, u[:-h1]], axis=0)
        dn = jnp.concatenate([u[h1:], u_next], axis=0)
    else:
        up, dn = u_prev, u_next
    cw = cw_ref[...]
    o_ref[...] = up * cw[0:1] + u * cw[1:2] + dn * cw[2:3] + cb_ref[...]


def _inproj_conv3(hg, w, layer, conv_w, conv_b, batch, seq_len, tn=512):
    m, k = hg.shape
    n = w.shape[2]
    n1, n2 = _fft_plan(seq_len)
    h1 = n1 // 2
    jb = _row_tile(n2, max(1, 2048 // h1))
    tm = jb * h1
    tps = n2 // jb
    tn = _tile(n, tn)

    def seq_slab(i, s):
        return lax.div(i, tps) * n2 + s

    def prev_slab(i, j):
        il = lax.rem(i, tps)
        return (seq_slab(i, jnp.where(il == 0, n2, il * jb) - 1), 0)

    def next_slab(i, j):
        il = lax.rem(i, tps)
        return (seq_slab(i, jnp.where(il == tps - 1, 0, (il + 1) * jb)), 0)

    body = functools.partial(_inproj_conv3_body, h1=h1, tiles_per_seq=tps)
    out = pl.pallas_call(
        body,
        grid=(m // tm, n // tn),
        in_specs=[pl.BlockSpec((h1, k), prev_slab),
                  pl.BlockSpec((tm, k), lambda i, j: (i, 0), pipeline_mode=pl.Buffered(1)),
                  pl.BlockSpec((h1, k), next_slab),
                  pl.BlockSpec((None, k, tn), lambda i, j: (layer, 0, j)),
                  pl.BlockSpec((3, tn), lambda i, j: (0, j)),
                  pl.BlockSpec((1, tn), lambda i, j: (0, j))],
        out_specs=pl.BlockSpec((tm, tn), lambda i, j: (i, j)),
        out_shape=jax.ShapeDtypeStruct((m, n), _F32),
        compiler_params=_cparams(("parallel", "arbitrary")),
        name="hyena_inproj_conv3",
    )(hg, hg, hg, w.astype(_F32), conv_w.astype(_F32), conv_b.reshape(1, n).astype(_F32))
    return out.reshape(batch, n2, h1, n)


def _permute_rows(x, groups, inverse=False):
    parts, row0 = [], 0
    for batch, seq_len in groups:
        n1, n2 = _fft_plan(seq_len)
        shape = (batch, n2, n1 // 2, -1) if inverse else (batch, n1 // 2, n2, -1)
        rows = batch * seq_len
        parts.append(x[row0:row0 + rows].reshape(shape).transpose(0, 2, 1, 3).reshape(rows, -1))
        row0 += rows
    return jnp.concatenate(parts, axis=0)


def _fft_plan(seq_len):
    n = 2 * seq_len
    n1 = 256 if n >= 32768 else 128
    while n // n1 < 8:
        n1 //= 2
    return n1, n // n1


def _kept_k1(n1):
    return n1 // 2 + 8


def _dft_tables(seq_len):
    n = 2 * seq_len
    n1, n2 = _fft_plan(seq_len)
    h1 = n1 // 2
    kp = _kept_k1(n1)
    i2 = lax.broadcasted_iota(jnp.int32, (n2, kp, n1), 0)
    k1 = lax.broadcasted_iota(jnp.int32, (n2, kp, n1), 1)
    i1 = lax.broadcasted_iota(jnp.int32, (n2, kp, n1), 2)
    phase = (i2 * k1 + n2 * (i1 * k1)) % n
    ang = phase.astype(_F32) * (2.0 * math.pi / n)
    cos = jnp.where(k1 <= h1, jnp.cos(ang), 0.0)
    sin = jnp.where(k1 <= h1, jnp.sin(ang), 0.0)
    fwd_a = jnp.concatenate([cos, -sin], axis=1).astype(_BF16)
    weight = jnp.where((k1 == 0) | (k1 == h1), 1.0, 2.0) * (1.0 / n)
    cos_t, sin_t = (cos * weight).transpose(0, 2, 1)[:, :h1], (sin * weight).transpose(0, 2, 1)[:, :h1]
    inv_a = jnp.concatenate([cos_t, -sin_t], axis=2).astype(_BF16)
    idx = np.arange(n2)
    ang_b = 2.0 * np.pi * ((idx[:, None] * idx[None, :]) % n2) / n2
    cb, sb = np.cos(ang_b), np.sin(ang_b)
    fwd_b = jnp.asarray(np.block([[cb, sb], [-sb, cb]]), _BF16)
    inv_b = jnp.asarray(np.block([[cb, -sb], [sb, cb]]), _BF16)
    fwd_a_half = fwd_a if h1 % 128 == 0 else fwd_a[:, :, :h1]
    return dict(fwd_a_full=fwd_a, fwd_a_half=fwd_a_half, inv_a=inv_a, fwd_b=fwd_b, inv_b=inv_b)


def _swap_stage_layout(x, groups, rows):
    b, _, _, d = x.shape
    return x.reshape(b, groups, 2, rows, d).transpose(0, 3, 2, 1, 4).reshape(b, rows, 2 * groups, d)


def _stage_a_fwd_body(x_ref, m_ref, o_ref):
    o_ref[...] = jnp.dot(m_ref[...], x_ref[...].astype(_BF16), preferred_element_type=_F32).astype(o_ref.dtype)


def _stage_a_fwd(xp, col_off, mats, d):
    b, n2, h1, _ = xp.shape
    r = mats.shape[1]
    td = _tile(d, 4096)
    off = col_off // td
    return pl.pallas_call(
        _stage_a_fwd_body,
        grid=(b, n2, d // td),
        in_specs=[pl.BlockSpec((None, None, h1, td), lambda bb, j, c: (bb, j, 0, off + c)),
                  pl.BlockSpec((None, r, h1), lambda bb, j, c: (j, 0, 0))],
        out_specs=pl.BlockSpec((None, None, r, td), lambda bb, j, c: (bb, j, 0, c)),
        out_shape=jax.ShapeDtypeStruct((b, n2, r, d), _BF16),
        compiler_params=_cparams(("parallel", "parallel", "parallel")),
        name="hyena_stage_a_fwd",
    )(xp, mats)


def _stage_b_conv_body(g_ref, kg_ref, l1_ref, fb_ref, fbi_ref, o_ref, ks_ref):
    n2 = g_ref.shape[0] // 2
    fb = fb_ref[...]
    x = jnp.dot(fb, g_ref[...], preferred_element_type=_F32)

    @pl.when(pl.program_id(2) == 0)
    def _():
        ks_ref[...] = jnp.dot(fb, kg_ref[...], preferred_element_type=_F32) * (1.0 / (l1_ref[...] + _EPS))

    xr, xi = x[:n2], x[n2:]
    kr, ki = ks_ref[:n2, :], ks_ref[n2:, :]
    y = jnp.concatenate([xr * kr - xi * ki, xr * ki + xi * kr], axis=0).astype(_BF16)
    o_ref[...] = jnp.dot(fbi_ref[...], y, preferred_element_type=_F32).astype(o_ref.dtype)


def _stage_b_conv(g, kg, l1, fwd_b, inv_b):
    b, n1, r, d = g.shape
    td = _tile(d, 4096)
    return pl.pallas_call(
        _stage_b_conv_body,
        grid=(n1, d // td, b),
        in_specs=[pl.BlockSpec((None, None, r, td), lambda k, c, bb: (bb, k, 0, c)),
                  pl.BlockSpec((None, r, td), lambda k, c, bb: (k, 0, c)),
                  pl.BlockSpec((1, td), lambda k, c, bb: (0, c)),
                  pl.BlockSpec((r, r), lambda k, c, bb: (0, 0)),
                  pl.BlockSpec((r, r), lambda k, c, bb: (0, 0))],
        out_specs=pl.BlockSpec((None, None, r, td), lambda k, c, bb: (bb, k, 0, c)),
        out_shape=jax.ShapeDtypeStruct((b, n1, r, d), _BF16),
        scratch_shapes=[pltpu.VMEM((r, td), _F32)],
        compiler_params=_cparams(("parallel", "parallel", "arbitrary")),
        name="hyena_stage_b_conv",
    )(g, kg, l1, fwd_b, inv_b)


def _stage_a_inv_body(h_ref, m_ref, y_ref, g_ref, s_ref, *rest):
    conv = jnp.dot(m_ref[...], h_ref[...], preferred_element_type=_F32)
    y_new = g_ref[...] * (conv + y_ref[...] * s_ref[...])
    if len(rest) == 1:
        rest[0][...] = y_new.astype(rest[0].dtype)
    else:
        fwd_ref, o_ref, gt_ref = rest
        o_ref[...] = y_new
        gt_ref[...] = jnp.dot(fwd_ref[...], y_new.astype(_BF16), preferred_element_type=_F32).astype(gt_ref.dtype)


def _stage_a_inv(ht, mats, yp, y_off, gp, g_off, skip, out_dtype, next_mats=None):
    b, n2, r, d = ht.shape
    h1 = mats.shape[1]
    td = _tile(d, 4096)
    yo, go = y_off // td, g_off // td
    in_specs = [pl.BlockSpec((None, None, r, td), lambda bb, j, c: (bb, j, 0, c)),
                pl.BlockSpec((None, h1, r), lambda bb, j, c: (j, 0, 0)),
                pl.BlockSpec((None, None, h1, td), lambda bb, j, c: (bb, j, 0, yo + c)),
                pl.BlockSpec((None, None, h1, td), lambda bb, j, c: (bb, j, 0, go + c)),
                pl.BlockSpec((1, td), lambda bb, j, c: (0, c))]
    args = [ht, mats, yp, gp, skip.reshape(1, d).astype(_F32)]
    out_specs = pl.BlockSpec((None, None, h1, td), lambda bb, j, c: (bb, j, 0, c))
    out_shape = jax.ShapeDtypeStruct((b, n2, h1, d), out_dtype)
    if next_mats is not None:
        in_specs.append(pl.BlockSpec((None, r, h1), lambda bb, j, c: (j, 0, 0)))
        args.append(next_mats)
        out_specs = [out_specs, pl.BlockSpec((None, None, r, td), lambda bb, j, c: (bb, j, 0, c))]
        out_shape = [out_shape, jax.ShapeDtypeStruct((b, n2, r, d), _BF16)]
    return pl.pallas_call(
        _stage_a_inv_body,
        grid=(b, n2, d // td),
        in_specs=in_specs,
        out_specs=out_specs,
        out_shape=out_shape,
        compiler_params=_cparams(("parallel", "parallel", "parallel")),
        name="hyena_stage_a_inv",
    )(*args)


def _filter_pos(shape, j0, n1, n2, seq_len):
    r = lax.broadcasted_iota(jnp.int32, shape, 0)
    i1 = jnp.bitwise_and(r, n1 - 1)
    j = j0 + lax.shift_right_logical(r, n1.bit_length() - 1)
    n = n2 * i1 + j
    return jnp.where(i1 < n1 // 2, n, 2 * seq_len - n)


def _filter_mlp_body(band_ref, w1_ref, b1_ref, w2_ref, b2_ref, w3_ref, b3_ref, fr_ref, o_ref,
                     *, jb, n1, n2, seq_len):
    hi = lax.Precision.HIGHEST
    rows = jb * n1
    p = _filter_pos((rows, 128), pl.program_id(0) * jb, n1, n2, seq_len).astype(_F32)
    lane = lax.broadcasted_iota(jnp.int32, (rows, 128), 1)
    ang = (2.0 * math.pi / seq_len) * p * band_ref[...]
    feat = jnp.where(lane == 0, p * (1.0 / (seq_len - 1)),
                     jnp.where(lane <= _HY_BANDS, jnp.cos(ang),
                               jnp.where(lane <= 2 * _HY_BANDS, -jnp.sin(ang), 0.0)))
    fr = fr_ref[...]
    z = jnp.sin(fr * (jnp.dot(feat, w1_ref[...], preferred_element_type=_F32, precision=hi) + b1_ref[...]))
    z = jnp.sin(fr * (jnp.dot(z, w2_ref[...], preferred_element_type=_F32, precision=hi) + b2_ref[...]))
    z = jnp.sin(fr * (jnp.dot(z, w3_ref[...], preferred_element_type=_F32, precision=hi) + b3_ref[...]))
    o_ref[...] = z.reshape(jb, n1, _HY_WIDTH)


def _filter_mlp(seq_len, fc1_w, fc1_b, fc2_w, fc2_b, fc3_w, fc3_b, sin_freq):
    n1, n2 = _fft_plan(seq_len)
    jb = _row_tile(n2, 8)
    wd = _HY_WIDTH
    bands = jnp.linspace(1e-4, _HY_BANDS - 1, _HY_BANDS, dtype=_F32)
    band_row = jnp.zeros((1, 128), _F32).at[0, 1:1 + _HY_BANDS].set(bands).at[0, 1 + _HY_BANDS:_HY_EMB].set(bands)
    w1 = jnp.zeros((128, wd), _F32).at[:_HY_EMB].set(fc1_w.astype(_F32))
    const = lambda shape: pl.BlockSpec(shape, lambda i: (0,) * len(shape))
    row = lambda a: a.reshape(1, wd).astype(_F32)
    body = functools.partial(_filter_mlp_body, jb=jb, n1=n1, n2=n2, seq_len=seq_len)
    return pl.pallas_call(
        body,
        grid=(n2 // jb,),
        in_specs=[const((1, 128)), const((128, wd)), const((1, wd)), const((wd, wd)), const((1, wd)),
                  const((wd, wd)), const((1, wd)), const((1, wd))],
        out_specs=pl.BlockSpec((jb, n1, wd), lambda i: (i, 0, 0)),
        out_shape=jax.ShapeDtypeStruct((n2, n1, wd), _F32),
        compiler_params=_cparams(("parallel",)),
        name="hyena_filter_mlp",
    )(band_row, w1, row(fc1_b), fc2_w.astype(_F32), row(fc2_b), fc3_w.astype(_F32), row(fc3_b), row(sin_freq))


def _kernel_stage_a_body(z_ref, wf_ref, wb_ref, dl_ref, m_ref, o_ref, l1_ref, *, n1, n2, seq_len):
    j = pl.program_id(1)
    h1 = n1 // 2
    td = wf_ref.shape[1]
    p = _filter_pos((n1, td), j, n1, n2, seq_len)
    window = jnp.exp(-(p.astype(_F32) * (1.0 / (seq_len - 1))) * dl_ref[...])
    z = z_ref[...]
    zr = lax.broadcasted_iota(jnp.int32, (n1, 2 * _HY_WIDTH), 0)
    zc = lax.broadcasted_iota(jnp.int32, (n1, 2 * _HY_WIDTH), 1)
    zz = jnp.where((zr < h1) == (zc < _HY_WIDTH), jnp.concatenate([z, z], axis=1), 0.0)
    wcat = jnp.concatenate([wf_ref[...], wb_ref[...]], axis=0)
    kern = jnp.dot(zz.astype(_BF16), wcat.astype(_BF16), preferred_element_type=_F32) * window
    kern = jnp.where(p == seq_len, 0.0, kern)

    @pl.when(j == 0)
    def _():
        l1_ref[...] = jnp.zeros_like(l1_ref)

    l1_ref[...] += jnp.sum(jnp.abs(kern), axis=0, keepdims=True)
    o_ref[...] = jnp.dot(m_ref[...], kern.astype(_BF16), preferred_element_type=_F32).astype(o_ref.dtype)


def _kernel_stage_a(z, fc4_w, order, deltas, mats, seq_len, d):
    n1, n2 = _fft_plan(seq_len)
    r = mats.shape[1]
    td = _tile(d, 2048)
    nd = d // td
    fwd_off, bwd_off = order * 2 * nd, order * 2 * nd + nd
    body = functools.partial(_kernel_stage_a_body, n1=n1, n2=n2, seq_len=seq_len)
    return pl.pallas_call(
        body,
        grid=(nd, n2),
        in_specs=[pl.BlockSpec((None, n1, _HY_WIDTH), lambda c, j: (j, 0, 0)),
                  pl.BlockSpec((_HY_WIDTH, td), lambda c, j: (0, fwd_off + c)),
                  pl.BlockSpec((_HY_WIDTH, td), lambda c, j: (0, bwd_off + c)),
                  pl.BlockSpec((1, td), lambda c, j: (0, c)),
                  pl.BlockSpec((None, r, n1), lambda c, j: (j, 0, 0))],
        out_specs=[pl.BlockSpec((None, r, td), lambda c, j: (j, 0, c)),
                   pl.BlockSpec((1, td), lambda c, j: (0, c))],
        out_shape=[jax.ShapeDtypeStruct((n2, r, d), _BF16), jax.ShapeDtypeStruct((1, d), _F32)],
        compiler_params=_cparams(("parallel", "arbitrary")),
        name="hyena_kernel_stage_a",
    )(z, fc4_w.astype(_F32), fc4_w.astype(_F32), deltas, mats)


def _hyena_long_conv(up, seq_len, d, hp):
    n1, n2 = _fft_plan(seq_len)
    kp = _kept_k1(n1)
    tabs = _dft_tables(seq_len)
    deltas = jnp.abs(jnp.linspace(_HY_MIN_DECAY, _HY_MAX_DECAY, d, dtype=_F32)).reshape(1, d)
    z = _filter_mlp(seq_len, hp["fc1_w"], hp["fc1_b"], hp["fc2_w"], hp["fc2_b"], hp["fc3_w"], hp["fc3_b"],
                    hp["sin_freq"])
    y, y_off = up, 0
    g = _stage_a_fwd(y, y_off, tabs["fwd_a_half"], d)
    for order in range(_HY_ORDER):
        kg, l1 = _kernel_stage_a(z, hp["fc4_w"], order, deltas, tabs["fwd_a_full"], seq_len, d)
        h = _stage_b_conv(_swap_stage_layout(g, n2, kp), _swap_stage_layout(kg[None], n2, kp)[0], l1,
                          tabs["fwd_b"], tabs["inv_b"])
        ht = _swap_stage_layout(h, kp, n2)
        if order == _HY_ORDER - 1:
            y = _stage_a_inv(ht, tabs["inv_a"], y, y_off, up, (order + 1) * d, hp["skip"][order], _BF16)
        else:
            y, g = _stage_a_inv(ht, tabs["inv_a"], y, y_off, up, (order + 1) * d, hp["skip"][order], _F32,
                                next_mats=tabs["fwd_a_half"])
        y_off = 0
    return y.reshape(-1, d)


def kernel(x_prompt, x_sample, norm_mix, norm_ffn, norm_final, hg_w_in, hg_lb_logits, hg_out_norm, hg_w_out,
           hy_w_in, hy_conv_w, hy_conv_b, hy_fc1_w, hy_fc1_b, hy_fc2_w, hy_fc2_b, hy_fc3_w, hy_fc3_b, hy_fc4_w,
           hy_sin_freq, hy_skip, hy_w_out, ffn_w_gate, ffn_w_up, ffn_w_down):
    b1, t1, d = x_prompt.shape
    b2, t2, _ = x_sample.shape
    groups = ((b1, t1), (b2, t2))
    group_x = (x_prompt.reshape(b1 * t1, d), x_sample.reshape(b2 * t2, d))
    m_total = b1 * t1 + b2 * t2
    depth = norm_mix.shape[0]

    x = None
    for layer in range(depth):
        slot = layer // 2
        if layer % 2 == 1:
            hp = dict(fc1_w=hy_fc1_w[slot], fc1_b=hy_fc1_b[slot], fc2_w=hy_fc2_w[slot], fc2_b=hy_fc2_b[slot],
                      fc3_w=hy_fc3_w[slot], fc3_b=hy_fc3_b[slot], fc4_w=hy_fc4_w[slot],
                      sin_freq=hy_sin_freq[slot], skip=hy_skip[slot])
        row0, x_next = 0, x
        for (bg, tg), xg in zip(groups, group_x):
            rows = bg * tg
            h = (_rmsnorm(xg, norm_mix[layer], _BF16) if x is None
                 else _rmsnorm(x_next, norm_mix[layer], _BF16, row0, rows))
            if layer % 2 == 0:
                proj = _matmul(h, (hg_w_in,), slot, mode="plain", tm=2048, rows_resident=True)
                o_f = _hgrn_scan(proj, hg_lb_logits[0], (tg,) * bg, reverse=False, slot=slot)
                y = _hgrn_scan(proj, hg_lb_logits[1], (tg,) * bg, reverse=True, slot=slot, o_fwd=o_f,
                               gain=hg_out_norm[slot])
                w_out = hg_w_out
            else:
                up = _inproj_conv3(_permute_rows(h, ((bg, tg),)), hy_w_in, slot, hy_conv_w[slot], hy_conv_b[slot],
                                   bg, tg)
                y = _permute_rows(_hyena_long_conv(up, tg, d, hp), ((bg, tg),), inverse=True)
                w_out = hy_w_out
            if x is None:
                x_next = _matmul(y, (w_out,), slot, mode="residual", residual=xg,
                                 out_into=(m_total, row0, x_next))
            else:
                x_next = _matmul(y, (w_out,), slot, mode="residual", residual=x_next, res_row0=row0)
            row0 += rows
        x = x_next
        h2 = _rmsnorm(x, norm_ffn[layer], _BF16)
        a = _matmul(h2, (ffn_w_gate, ffn_w_up), layer, mode="swiglu", out_dtype=_BF16, tm=2048, tn=256,
                    rows_resident=True)
        x = _matmul(a, (ffn_w_down,), layer, mode="residual", residual=x, tm=256, single_buffer_w=True)

    return (_rmsnorm(x, norm_final, _F32, 0, b1 * t1).reshape(b1, t1, d),
            _rmsnorm(x, norm_final, _F32, b1 * t1, b2 * t2).reshape(b2, t2, d))
```

```python
import functools
import math

import numpy as np
import jax
import jax.numpy as jnp
from jax import lax
from jax.experimental import pallas as pl
from jax.experimental.pallas import tpu as pltpu

_F32 = jnp.float32
_BF16 = jnp.bfloat16

_EPS = 1e-6
_HEAD_DIM = 128
_CHUNK = 64
_SUB = 16
_EXP_CLAMP = 115.0
_HY_ORDER = 2
_HY_EMB = 33
_HY_BANDS = (_HY_EMB - 1) // 2
_HY_WIDTH = 64
_HY_MAX_DECAY = math.log(1e-2) / 0.3
_HY_MIN_DECAY = math.log(1e-2) / 1.5
_V7X_VMEM_LIMIT = 56 * 1024 * 1024


def _cparams(sem):
    return pltpu.CompilerParams(dimension_semantics=sem, vmem_limit_bytes=_V7X_VMEM_LIMIT)


def _tile(dim, want):
    if dim <= want:
        return dim
    t = want
    while t >= 128:
        if dim % t == 0:
            return t
        t -= 128
    return dim


def _row_tile(rows, want):
    t = min(rows, want)
    while rows % t:
        t //= 2
    return t


def _rmsnorm_body(x_ref, g_ref, o_ref):
    x = x_ref[...]
    ms = jnp.mean(x * x, axis=-1, keepdims=True)
    o_ref[...] = (x * lax.rsqrt(ms + _EPS) * g_ref[...]).astype(o_ref.dtype)


def _rmsnorm(x, gain, out_dtype, row0=0, rows=None):
    d = x.shape[1]
    m = x.shape[0] if rows is None else rows
    tm = _row_tile(math.gcd(m, row0) if row0 else m, 512)
    blk0 = row0 // tm
    return pl.pallas_call(
        _rmsnorm_body,
        grid=(m // tm,),
        in_specs=[pl.BlockSpec((tm, d), lambda i: (blk0 + i, 0)),
                  pl.BlockSpec((1, d), lambda i: (0, 0))],
        out_specs=pl.BlockSpec((tm, d), lambda i: (i, 0)),
        out_shape=jax.ShapeDtypeStruct((m, d), out_dtype),
        compiler_params=_cparams(("parallel",)),
        name="rmsnorm",
    )(x, gain.reshape(1, d).astype(_F32))


def _mm_body(*refs, mode, n_w, rows_resident, n_in):
    a_ref, w_refs, o_ref = refs[0], refs[1:1 + n_w], refs[n_in]
    if rows_resident:
        ws = [w_ref[...].astype(_BF16) for w_ref in w_refs]
    else:
        wb_refs = refs[len(refs) - n_w:]

        @pl.when(pl.program_id(1) == 0)
        def _():
            for w_ref, wb_ref in zip(w_refs, wb_refs):
                wb_ref[...] = w_ref[...].astype(_BF16)

        ws = [wb_ref[...] for wb_ref in wb_refs]

    a = a_ref[...]
    if mode == "swiglu":
        g = jnp.dot(a, ws[0], preferred_element_type=_F32)
        u = jnp.dot(a, ws[1], preferred_element_type=_F32)
        o_ref[...] = (g * (0.5 * jnp.tanh(0.5 * g) + 0.5) * u).astype(o_ref.dtype)
    elif mode == "residual":
        o_ref[...] = refs[1 + n_w][...] + jnp.dot(a, ws[0], preferred_element_type=_F32)
    else:
        o_ref[...] = jnp.dot(a, ws[0], preferred_element_type=_F32).astype(o_ref.dtype)


def _matmul(a, ws, layer, *, mode, residual=None, res_row0=0, out_into=None, out_dtype=_F32, tm=1024, tn=512,
            rows_resident=False, single_buffer_w=False):
    m, k = a.shape
    n = ws[0].shape[2]
    row_offsets = [r for r in (res_row0, out_into[1] if out_into else 0) if r]
    tm = _row_tile(math.gcd(m, *row_offsets), tm)
    tn = _tile(n, tn)
    if rows_resident:
        grid = (m // tm, n // tn)
        ij = lambda g0, g1: (g0, g1)
        a_mode, w_mode = dict(pipeline_mode=pl.Buffered(1)), {}
        scratch = []
    else:
        grid = (n // tn, m // tm)
        ij = lambda g0, g1: (g1, g0)
        a_mode, w_mode = {}, (dict(pipeline_mode=pl.Buffered(1)) if single_buffer_w else {})
        scratch = [pltpu.VMEM((k, tn), _BF16) for _ in ws]
    in_specs = [pl.BlockSpec((tm, k), lambda g0, g1: (ij(g0, g1)[0], 0), **a_mode)]
    in_specs += [pl.BlockSpec((None, k, tn), lambda g0, g1: (layer, 0, ij(g0, g1)[1]), **w_mode) for _ in ws]
    args = [a, *[w.astype(_F32) for w in ws]]
    out_spec = pl.BlockSpec((tm, tn), lambda g0, g1: ij(g0, g1))
    out_shape = jax.ShapeDtypeStruct((m, n), out_dtype)
    aliases = {}
    if mode == "residual":
        def offset_spec(row0):
            blk0 = row0 // tm
            return pl.BlockSpec((tm, tn), lambda g0, g1: (blk0 + ij(g0, g1)[0], ij(g0, g1)[1]))

        in_specs.append(offset_spec(res_row0))
        args.append(residual)
        if out_into is None:
            out_spec = offset_spec(res_row0)
            out_shape = jax.ShapeDtypeStruct(residual.shape, residual.dtype)
            aliases = {len(args) - 1: 0}
        else:
            total_rows, row0, buffer = out_into
            out_spec = offset_spec(row0)
            out_shape = jax.ShapeDtypeStruct((total_rows, n), out_dtype)
            if buffer is not None:
                in_specs.append(pl.BlockSpec(memory_space=pl.ANY))
                args.append(buffer)
                aliases = {len(args) - 1: 0}
    return pl.pallas_call(
        functools.partial(_mm_body, mode=mode, n_w=len(ws), rows_resident=rows_resident,
                          n_in=len(args)),
        grid=grid,
        in_specs=in_specs,
        out_specs=out_spec,
        out_shape=out_shape,
        scratch_shapes=scratch,
        input_output_aliases=aliases,
        compiler_params=_cparams(("parallel", "arbitrary")),
        name="matmul_" + mode,
    )(*args)


def _scan_body(*refs, reverse, n_chunks, hb, reset_blocks, n_tblocks, slot, epilogue):
    if epilogue:
        q_ref, fl_ref, v_ref, lbl_ref, of_ref, gate_ref, gain_ref, o_ref, st_ref = refs
    else:
        q_ref, fl_ref, v_ref, lbl_ref, o_ref, st_ref = refs
    c_len, hd = _CHUNK, _HEAD_DIM

    t = pl.program_id(1)
    blk = (n_tblocks - 1 - t) if reverse else t
    reset = blk == reset_blocks[0]
    for r in reset_blocks[1:]:
        reset = jnp.logical_or(reset, blk == r)

    @pl.when(reset)
    def _():
        st_ref[...] = jnp.zeros_like(st_ref)

    lbl = lbl_ref[...]
    e = jnp.exp(lbl - jnp.max(lbl, axis=0, keepdims=True))
    lb_all = jnp.sum(e[: slot + 1], axis=0, keepdims=True) / jnp.sum(e, axis=0, keepdims=True)

    row = lax.broadcasted_iota(jnp.int32, (c_len, c_len), 0)
    col = lax.broadcasted_iota(jnp.int32, (c_len, c_len), 1)
    keep = (row <= col) if reverse else (row >= col)
    tri = keep.astype(_BF16)
    n_sub = c_len // _SUB
    w = hb * hd
    nt = (((1,), (1,)), ((), ()))
    tn = (((0,), (0,)), ((), ()))
    heads = [slice(h * hd, (h + 1) * hd) for h in range(hb)]

    def sigmoid(z):
        return 0.5 * jnp.tanh(0.5 * z) + 0.5

    def pad_rows(part, lo_r, dtype):
        pads = [jnp.zeros((lo_r, part.shape[1]), dtype), part,
                jnp.zeros((c_len - lo_r - part.shape[0], part.shape[1]), dtype)]
        return jnp.concatenate([p for p in pads if p.shape[0]], axis=0)

    def chunk_step(ci, carry):
        c = (n_chunks - 1 - ci) if reverse else ci
        r0 = pl.multiple_of(c * c_len, c_len)
        qr = q_ref[pl.ds(r0, c_len), :]
        q = qr * sigmoid(qr)
        f = lb_all + (1.0 - lb_all) * sigmoid(fl_ref[pl.ds(r0, c_len), :])
        k = 1.0 - f
        vb = v_ref[pl.ds(r0, c_len), :].astype(_BF16)
        lf = jnp.log2(f)
        hi = lf.astype(_BF16)
        rem = lf - hi.astype(_F32)
        mid = rem.astype(_BF16)
        lo = (rem - mid.astype(_F32)).astype(_BF16)
        cum = (jnp.dot(tri, hi, preferred_element_type=_F32) + jnp.dot(tri, mid, preferred_element_type=_F32)
               + jnp.dot(tri, lo, preferred_element_type=_F32))
        edge = cum[0:1] if reverse else cum[c_len - 1:c_len]
        q_in = (q * jnp.exp2(cum)).astype(_BF16)
        k_end = (k * jnp.exp2(edge - cum)).astype(_BF16)
        st_decay = jnp.exp2(edge)
        refs = []
        for i in range(n_sub):
            if reverse:
                refs.append(cum[(i + 1) * _SUB:(i + 1) * _SUB + 1] if i < n_sub - 1 else jnp.zeros((1, w), _F32))
            else:
                refs.append(cum[i * _SUB - 1:i * _SUB] if i > 0 else jnp.zeros((1, w), _F32))
        ref_rows = jnp.concatenate([jnp.broadcast_to(r, (_SUB, w)) for r in refs], axis=0)
        a = (q * jnp.exp2(cum - ref_rows)).astype(_BF16)
        bs = []
        for i in range(n_sub):
            lo_r, hi_r = (i * _SUB, c_len) if reverse else (0, (i + 1) * _SUB)
            part = k[lo_r:hi_r] * jnp.exp2(jnp.minimum(refs[i] - cum[lo_r:hi_r], _EXP_CLAMP))
            bs.append(pad_rows(part.astype(_BF16), lo_r, _BF16))

        sts = [st_ref[h] for h in range(hb)]
        o_inter = [lax.dot_general(q_in[:, ln], sts[h].astype(_BF16), nt, preferred_element_type=_F32)
                   for h, ln in enumerate(heads)]
        a_cat = [jnp.concatenate([pad_rows(a[i * _SUB:(i + 1) * _SUB, ln], i * _SUB, _BF16) for i in range(n_sub)],
                                 axis=1) for ln in heads]
        b_cat = [jnp.concatenate([bs[i][:, ln] for i in range(n_sub)], axis=1) for ln in heads]
        scores = [lax.dot_general(a_cat[h], b_cat[h], nt, preferred_element_type=_F32) for h in range(hb)]
        scores = [jnp.where(keep, s, 0.0).astype(_BF16) for s in scores]
        o_intra = [jnp.dot(scores[h], vb[:, ln], preferred_element_type=_F32) for h, ln in enumerate(heads)]
        st_new = [lax.dot_general(vb[:, ln], k_end[:, ln], tn, preferred_element_type=_F32) for ln in heads]
        for h, ln in enumerate(heads):
            st_ref[h] = sts[h] * st_decay[:, ln] + st_new[h]
        o = jnp.concatenate([o_inter[h] + o_intra[h] for h in range(hb)], axis=1)
        if epilogue:
            tot = of_ref[pl.ds(r0, c_len), :] + o
            ys = []
            for ln in heads:
                th = tot[:, ln]
                ys.append(th * lax.rsqrt(jnp.mean(th * th, axis=-1, keepdims=True) + _EPS))
            g = gate_ref[pl.ds(r0, c_len), :]
            y = jnp.concatenate(ys, axis=1) * gain_ref[...]
            o_ref[pl.ds(r0, c_len), :] = (y * (g * sigmoid(g))).astype(o_ref.dtype)
        else:
            o_ref[pl.ds(r0, c_len), :] = o
        return carry

    lax.fori_loop(0, n_chunks, chunk_step, 0)


def _hgrn_scan(proj, lb_logits, seg_lens, *, reverse, slot, o_fwd=None, gain=None):
    m, d5 = proj.shape
    d = d5 // 5
    hb = next(n for n in (32, 16, 8, 4, 2, 1) if d % (n * _HEAD_DIM) == 0)
    w = hb * _HEAD_DIM
    nhb = d // w
    tb = _row_tile(math.gcd(*seg_lens) if len(seg_lens) > 1 else seg_lens[0], 128)
    n_t = m // tb
    bounds = np.cumsum([0] + list(seg_lens)) // tb
    reset_blocks = tuple(int(b) - 1 for b in bounds[1:]) if reverse else tuple(int(b) for b in bounds[:-1])
    epilogue = o_fwd is not None

    def tmap(t):
        return (n_t - 1 - t) if reverse else t

    def stream(s):
        return pl.BlockSpec((tb, w), lambda h, t, s=s: (tmap(t), s * nhb + h))

    n_slots = lb_logits.shape[0]
    in_specs = [stream(0), stream(2 if reverse else 1), stream(3),
                pl.BlockSpec((n_slots, w), lambda h, t: (0, h))]
    args = [proj, proj, proj, lb_logits.astype(_F32)]
    if epilogue:
        in_specs += [pl.BlockSpec((tb, w), lambda h, t: (tmap(t), h)), stream(4),
                     pl.BlockSpec((1, w), lambda h, t: (0, h))]
        args += [o_fwd, proj, gain.reshape(1, d).astype(_F32)]
    body = functools.partial(_scan_body, reverse=reverse, n_chunks=tb // _CHUNK, hb=hb,
                             reset_blocks=reset_blocks, n_tblocks=n_t, slot=slot, epilogue=epilogue)
    return pl.pallas_call(
        body,
        grid=(nhb, n_t),
        in_specs=in_specs,
        out_specs=pl.BlockSpec((tb, w), lambda h, t: (tmap(t), h)),
        out_shape=jax.ShapeDtypeStruct((m, d), _BF16 if epilogue else _F32),
        scratch_shapes=[pltpu.VMEM((hb, _HEAD_DIM, _HEAD_DIM), _F32)],
        compiler_params=_cparams(("parallel", "arbitrary")),
        name="hgrn_scan_bwd" if reverse else "hgrn_scan_fwd",
    )(*args)


def _inproj_conv3_body(prev_ref, cur_ref, nxt_ref, w_ref, cw_ref, cb_ref, o_ref, *, h1, tiles_per_seq):
    i_loc = lax.rem(pl.program_id(0), tiles_per_seq)
    wb = w_ref[...].astype(_BF16)
    u = jnp.dot(cur_ref[...], wb, preferred_element_type=_F32)
    u_prev = jnp.dot(prev_ref[...], wb, preferred_element_type=_F32)
    u_next = jnp.dot(nxt_ref[...], wb, preferred_element_type=_F32)
    rows = lax.broadcasted_iota(jnp.int32, u_prev.shape, 0)
    u_prev = jnp.where(i_loc == 0, jnp.where(rows == 0, 0.0, pltpu.roll(u_prev, 1, 0)), u_prev)
    u_next = jnp.where(i_loc == tiles_per_seq - 1,
                       jnp.where(rows == h1 - 1, 0.0, pltpu.roll(u_next, h1 - 1, 0)), u_next)
    if u.shape[0] > h1:
        up = jnp.concatenate([u_prev, u[:-h1]], axis=0)
        dn = jnp.concatenate([u[h1:], u_next], axis=0)
    else:
        up, dn = u_prev, u_next
    cw = cw_ref[...]
    o_ref[...] = up * cw[0:1] + u * cw[1:2] + dn * cw[2:3] + cb_ref[...]


def _inproj_conv3(hg, w, layer, conv_w, conv_b, batch, seq_len, tn=512):
    m, k = hg.shape
    n = w.shape[2]
    n1, n2 = _fft_plan(seq_len)
    h1 = n1 // 2
    jb = _row_tile(n2, max(1, 2048 // h1))
    tm = jb * h1
    tps = n2 // jb
    tn = _tile(n, tn)

    def seq_slab(i, s):
        return lax.div(i, tps) * n2 + s

    def prev_slab(i, j):
        il = lax.rem(i, tps)
        return (seq_slab(i, jnp.where(il == 0, n2, il * jb) - 1), 0)

    def next_slab(i, j):
        il = lax.rem(i, tps)
        return (seq_slab(i, jnp.where(il == tps - 1, 0, (il + 1) * jb)), 0)

    body = functools.partial(_inproj_conv3_body, h1=h1, tiles_per_seq=tps)
    out = pl.pallas_call(
        body,
        grid=(m // tm, n // tn),
        in_specs=[pl.BlockSpec((h1, k), prev_slab),
                  pl.BlockSpec((tm, k), lambda i, j: (i, 0), pipeline_mode=pl.Buffered(1)),
                  pl.BlockSpec((h1, k), next_slab),
                  pl.BlockSpec((None, k, tn), lambda i, j: (layer, 0, j)),
                  pl.BlockSpec((3, tn), lambda i, j: (0, j)),
                  pl.BlockSpec((1, tn), lambda i, j: (0, j))],
        out_specs=pl.BlockSpec((tm, tn), lambda i, j: (i, j)),
        out_shape=jax.ShapeDtypeStruct((m, n), _F32),
        compiler_params=_cparams(("parallel", "arbitrary")),
        name="hyena_inproj_conv3",
    )(hg, hg, hg, w.astype(_F32), conv_w.astype(_F32), conv_b.reshape(1, n).astype(_F32))
    return out.reshape(batch, n2, h1, n)


def _permute_rows(x, groups, inverse=False):
    parts, row0 = [], 0
    for batch, seq_len in groups:
        n1, n2 = _fft_plan(seq_len)
        shape = (batch, n2, n1 // 2, -1) if inverse else (batch, n1 // 2, n2, -1)
        rows = batch * seq_len
        parts.append(x[row0:row0 + rows].reshape(shape).transpose(0, 2, 1, 3).reshape(rows, -1))
        row0 += rows
    return jnp.concatenate(parts, axis=0)


def _fft_plan(seq_len):
    n = 2 * seq_len
    n1 = 256 if n >= 32768 else 128
    while n // n1 < 8:
        n1 //= 2
    return n1, n // n1


def _kept_k1(n1):
    return n1 // 2 + 8


def _dft_tables(seq_len):
    n = 2 * seq_len
    n1, n2 = _fft_plan(seq_len)
    h1 = n1 // 2
    kp = _kept_k1(n1)
    i2 = lax.broadcasted_iota(jnp.int32, (n2, kp, n1), 0)
    k1 = lax.broadcasted_iota(jnp.int32, (n2, kp, n1), 1)
    i1 = lax.broadcasted_iota(jnp.int32, (n2, kp, n1), 2)
    phase = (i2 * k1 + n2 * (i1 * k1)) % n
    ang = phase.astype(_F32) * (2.0 * math.pi / n)
    cos = jnp.where(k1 <= h1, jnp.cos(ang), 0.0)
    sin = jnp.where(k1 <= h1, jnp.sin(ang), 0.0)
    fwd_a = jnp.concatenate([cos, -sin], axis=1).astype(_BF16)
    weight = jnp.where((k1 == 0) | (k1 == h1), 1.0, 2.0) * (1.0 / n)
    cos_t, sin_t = (cos * weight).transpose(0, 2, 1)[:, :h1], (sin * weight).transpose(0, 2, 1)[:, :h1]
    inv_a = jnp.concatenate([cos_t, -sin_t], axis=2).astype(_BF16)
    idx = np.arange(n2)
    ang_b = 2.0 * np.pi * ((idx[:, None] * idx[None, :]) % n2) / n2
    cb, sb = np.cos(ang_b), np.sin(ang_b)
    fwd_b = jnp.asarray(np.block([[cb, sb], [-sb, cb]]), _BF16)
    inv_b = jnp.asarray(np.block([[cb, -sb], [sb, cb]]), _BF16)
    fwd_a_half = fwd_a if h1 % 128 == 0 else fwd_a[:, :, :h1]
    return dict(fwd_a_full=fwd_a, fwd_a_half=fwd_a_half, inv_a=inv_a, fwd_b=fwd_b, inv_b=inv_b)


def _swap_stage_layout(x, groups, rows):
    b, _, _, d = x.shape
    return x.reshape(b, groups, 2, rows, d).transpose(0, 3, 2, 1, 4).reshape(b, rows, 2 * groups, d)


def _stage_a_fwd_body(x_ref, m_ref, o_ref):
    o_ref[...] = jnp.dot(m_ref[...], x_ref[...].astype(_BF16), preferred_element_type=_F32).astype(o_ref.dtype)


def _stage_a_fwd(xp, col_off, mats, d):
    b, n2, h1, _ = xp.shape
    r = mats.shape[1]
    td = _tile(d, 4096)
    off = col_off // td
    return pl.pallas_call(
        _stage_a_fwd_body,
        grid=(b, n2, d // td),
        in_specs=[pl.BlockSpec((None, None, h1, td), lambda bb, j, c: (bb, j, 0, off + c)),
                  pl.BlockSpec((None, r, h1), lambda bb, j, c: (j, 0, 0))],
        out_specs=pl.BlockSpec((None, None, r, td), lambda bb, j, c: (bb, j, 0, c)),
        out_shape=jax.ShapeDtypeStruct((b, n2, r, d), _BF16),
        compiler_params=_cparams(("parallel", "parallel", "parallel")),
        name="hyena_stage_a_fwd",
    )(xp, mats)


def _stage_b_conv_body(g_ref, kg_ref, l1_ref, fb_ref, fbi_ref, o_ref, ks_ref):
    n2 = g_ref.shape[0] // 2
    fb = fb_ref[...]
    x = jnp.dot(fb, g_ref[...], preferred_element_type=_F32)

    @pl.when(pl.program_id(2) == 0)
    def _():
        ks_ref[...] = jnp.dot(fb, kg_ref[...], preferred_element_type=_F32) * (1.0 / (l1_ref[...] + _EPS))

    xr, xi = x[:n2], x[n2:]
    kr, ki = ks_ref[:n2, :], ks_ref[n2:, :]
    y = jnp.concatenate([xr * kr - xi * ki, xr * ki + xi * kr], axis=0).astype(_BF16)
    o_ref[...] = jnp.dot(fbi_ref[...], y, preferred_element_type=_F32).astype(o_ref.dtype)


def _stage_b_conv(g, kg, l1, fwd_b, inv_b):
    b, n1, r, d = g.shape
    td = _tile(d, 4096)
    return pl.pallas_call(
        _stage_b_conv_body,
        grid=(n1, d // td, b),
        in_specs=[pl.BlockSpec((None, None, r, td), lambda k, c, bb: (bb, k, 0, c)),
                  pl.BlockSpec((None, r, td), lambda k, c, bb: (k, 0, c)),
                  pl.BlockSpec((1, td), lambda k, c, bb: (0, c)),
                  pl.BlockSpec((r, r), lambda k, c, bb: (0, 0)),
                  pl.BlockSpec((r, r), lambda k, c, bb: (0, 0))],
        out_specs=pl.BlockSpec((None, None, r, td), lambda k, c, bb: (bb, k, 0, c)),
        out_shape=jax.ShapeDtypeStruct((b, n1, r, d), _BF16),
        scratch_shapes=[pltpu.VMEM((r, td), _F32)],
        compiler_params=_cparams(("parallel", "parallel", "arbitrary")),
        name="hyena_stage_b_conv",
    )(g, kg, l1, fwd_b, inv_b)


def _stage_a_inv_body(h_ref, m_ref, y_ref, g_ref, s_ref, *rest):
    conv = jnp.dot(m_ref[...], h_ref[...], preferred_element_type=_F32)
    y_new = g_ref[...] * (conv + y_ref[...] * s_ref[...])
    if len(rest) == 1:
        rest[0][...] = y_new.astype(rest[0].dtype)
    else:
        fwd_ref, o_ref, gt_ref = rest
        o_ref[...] = y_new
        gt_ref[...] = jnp.dot(fwd_ref[...], y_new.astype(_BF16), preferred_element_type=_F32).astype(gt_ref.dtype)


def _stage_a_inv(ht, mats, yp, y_off, gp, g_off, skip, out_dtype, next_mats=None):
    b, n2, r, d = ht.shape
    h1 = mats.shape[1]
    td = _tile(d, 4096)
    yo, go = y_off // td, g_off // td
    in_specs = [pl.BlockSpec((None, None, r, td), lambda bb, j, c: (bb, j, 0, c)),
                pl.BlockSpec((None, h1, r), lambda bb, j, c: (j, 0, 0)),
                pl.BlockSpec((None, None, h1, td), lambda bb, j, c: (bb, j, 0, yo + c)),
                pl.BlockSpec((None, None, h1, td), lambda bb, j, c: (bb, j, 0, go + c)),
                pl.BlockSpec((1, td), lambda bb, j, c: (0, c))]
    args = [ht, mats, yp, gp, skip.reshape(1, d).astype(_F32)]
    out_specs = pl.BlockSpec((None, None, h1, td), lambda bb, j, c: (bb, j, 0, c))
    out_shape = jax.ShapeDtypeStruct((b, n2, h1, d), out_dtype)
    if next_mats is not None:
        in_specs.append(pl.BlockSpec((None, r, h1), lambda bb, j, c: (j, 0, 0)))
        args.append(next_mats)
        out_specs = [out_specs, pl.BlockSpec((None, None, r, td), lambda bb, j, c: (bb, j, 0, c))]
        out_shape = [out_shape, jax.ShapeDtypeStruct((b, n2, r, d), _BF16)]
    return pl.pallas_call(
        _stage_a_inv_body,
        grid=(b, n2, d // td),
        in_specs=in_specs,
        out_specs=out_specs,
        out_shape=out_shape,
        compiler_params=_cparams(("parallel", "parallel", "parallel")),
        name="hyena_stage_a_inv",
    )(*args)


def _filter_pos(shape, j0, n1, n2, seq_len):
    r = lax.broadcasted_iota(jnp.int32, shape, 0)
    i1 = jnp.bitwise_and(r, n1 - 1)
    j = j0 + lax.shift_right_logical(r, n1.bit_length() - 1)
    n = n2 * i1 + j
    return jnp.where(i1 < n1 // 2, n, 2 * seq_len - n)


def _filter_mlp_body(band_ref, w1_ref, b1_ref, w2_ref, b2_ref, w3_ref, b3_ref, fr_ref, o_ref,
                     *, jb, n1, n2, seq_len):
    hi = lax.Precision.HIGHEST
    rows = jb * n1
    p = _filter_pos((rows, 128), pl.program_id(0) * jb, n1, n2, seq_len).astype(_F32)
    lane = lax.broadcasted_iota(jnp.int32, (rows, 128), 1)
    ang = (2.0 * math.pi / seq_len) * p * band_ref[...]
    feat = jnp.where(lane == 0, p * (1.0 / (seq_len - 1)),
                     jnp.where(lane <= _HY_BANDS, jnp.cos(ang),
                               jnp.where(lane <= 2 * _HY_BANDS, -jnp.sin(ang), 0.0)))
    fr = fr_ref[...]
    z = jnp.sin(fr * (jnp.dot(feat, w1_ref[...], preferred_element_type=_F32, precision=hi) + b1_ref[...]))
    z = jnp.sin(fr * (jnp.dot(z, w2_ref[...], preferred_element_type=_F32, precision=hi) + b2_ref[...]))
    z = jnp.sin(fr * (jnp.dot(z, w3_ref[...], preferred_element_type=_F32, precision=hi) + b3_ref[...]))
    o_ref[...] = z.reshape(jb, n1, _HY_WIDTH)


def _filter_mlp(seq_len, fc1_w, fc1_b, fc2_w, fc2_b, fc3_w, fc3_b, sin_freq):
    n1, n2 = _fft_plan(seq_len)
    jb = _row_tile(n2, 8)
    wd = _HY_WIDTH
    bands = jnp.linspace(1e-4, _HY_BANDS - 1, _HY_BANDS, dtype=_F32)
    band_row = jnp.zeros((1, 128), _F32).at[0, 1:1 + _HY_BANDS].set(bands).at[0, 1 + _HY_BANDS:_HY_EMB].set(bands)
    w1 = jnp.zeros((128, wd), _F32).at[:_HY_EMB].set(fc1_w.astype(_F32))
    const = lambda shape: pl.BlockSpec(shape, lambda i: (0,) * len(shape))
    row = lambda a: a.reshape(1, wd).astype(_F32)
    body = functools.partial(_filter_mlp_body, jb=jb, n1=n1, n2=n2, seq_len=seq_len)
    return pl.pallas_call(
        body,
        grid=(n2 // jb,),
        in_specs=[const((1, 128)), const((128, wd)), const((1, wd)), const((wd, wd)), const((1, wd)),
                  const((wd, wd)), const((1, wd)), const((1, wd))],
        out_specs=pl.BlockSpec((jb, n1, wd), lambda i: (i, 0, 0)),
        out_shape=jax.ShapeDtypeStruct((n2, n1, wd), _F32),
        compiler_params=_cparams(("parallel",)),
        name="hyena_filter_mlp",
    )(band_row, w1, row(fc1_b), fc2_w.astype(_F32), row(fc2_b), fc3_w.astype(_F32), row(fc3_b), row(sin_freq))


def _kernel_stage_a_body(z_ref, wf_ref, wb_ref, dl_ref, m_ref, o_ref, l1_ref, *, n1, n2, seq_len):
    j = pl.program_id(1)
    h1 = n1 // 2
    td = wf_ref.shape[1]
    p = _filter_pos((n1, td), j, n1, n2, seq_len)
    window = jnp.exp(-(p.astype(_F32) * (1.0 / (seq_len - 1))) * dl_ref[...])
    z = z_ref[...]
    zr = lax.broadcasted_iota(jnp.int32, (n1, 2 * _HY_WIDTH), 0)
    zc = lax.broadcasted_iota(jnp.int32, (n1, 2 * _HY_WIDTH), 1)
    zz = jnp.where((zr < h1) == (zc < _HY_WIDTH), jnp.concatenate([z, z], axis=1), 0.0)
    wcat = jnp.concatenate([wf_ref[...], wb_ref[...]], axis=0)
    kern = jnp.dot(zz.astype(_BF16), wcat.astype(_BF16), preferred_element_type=_F32) * window
    kern = jnp.where(p == seq_len, 0.0, kern)

    @pl.when(j == 0)
    def _():
        l1_ref[...] = jnp.zeros_like(l1_ref)

    l1_ref[...] += jnp.sum(jnp.abs(kern), axis=0, keepdims=True)
    o_ref[...] = jnp.dot(m_ref[...], kern.astype(_BF16), preferred_element_type=_F32).astype(o_ref.dtype)


def _kernel_stage_a(z, fc4_w, order, deltas, mats, seq_len, d):
    n1, n2 = _fft_plan(seq_len)
    r = mats.shape[1]
    td = _tile(d, 2048)
    nd = d // td
    fwd_off, bwd_off = order * 2 * nd, order * 2 * nd + nd
    body = functools.partial(_kernel_stage_a_body, n1=n1, n2=n2, seq_len=seq_len)
    return pl.pallas_call(
        body,
        grid=(nd, n2),
        in_specs=[pl.BlockSpec((None, n1, _HY_WIDTH), lambda c, j: (j, 0, 0)),
                  pl.BlockSpec((_HY_WIDTH, td), lambda c, j: (0, fwd_off + c)),
                  pl.BlockSpec((_HY_WIDTH, td), lambda c, j: (0, bwd_off + c)),
                  pl.BlockSpec((1, td), lambda c, j: (0, c)),
                  pl.BlockSpec((None, r, n1), lambda c, j: (j, 0, 0))],
        out_specs=[pl.BlockSpec((None, r, td), lambda c, j: (j, 0, c)),
                   pl.BlockSpec((1, td), lambda c, j: (0, c))],
        out_shape=[jax.ShapeDtypeStruct((n2, r, d), _BF16), jax.ShapeDtypeStruct((1, d), _F32)],
        compiler_params=_cparams(("parallel", "arbitrary")),
        name="hyena_kernel_stage_a",
    )(z, fc4_w.astype(_F32), fc4_w.astype(_F32), deltas, mats)


def _hyena_long_conv(up, seq_len, d, hp):
    n1, n2 = _fft_plan(seq_len)
    kp = _kept_k1(n1)
    tabs = _dft_tables(seq_len)
    deltas = jnp.abs(jnp.linspace(_HY_MIN_DECAY, _HY_MAX_DECAY, d, dtype=_F32)).reshape(1, d)
    z = _filter_mlp(seq_len, hp["fc1_w"], hp["fc1_b"], hp["fc2_w"], hp["fc2_b"], hp["fc3_w"], hp["fc3_b"],
                    hp["sin_freq"])
    y, y_off = up, 0
    g = _stage_a_fwd(y, y_off, tabs["fwd_a_half"], d)
    for order in range(_HY_ORDER):
        kg, l1 = _kernel_stage_a(z, hp["fc4_w"], order, deltas, tabs["fwd_a_full"], seq_len, d)
        h = _stage_b_conv(_swap_stage_layout(g, n2, kp), _swap_stage_layout(kg[None], n2, kp)[0], l1,
                          tabs["fwd_b"], tabs["inv_b"])
        ht = _swap_stage_layout(h, kp, n2)
        if order == _HY_ORDER - 1:
            y = _stage_a_inv(ht, tabs["inv_a"], y, y_off, up, (order + 1) * d, hp["skip"][order], _BF16)
        else:
            y, g = _stage_a_inv(ht, tabs["inv_a"], y, y_off, up, (order + 1) * d, hp["skip"][order], _F32,
                                next_mats=tabs["fwd_a_half"])
        y_off = 0
    return y.reshape(-1, d)


def kernel(x_prompt, x_sample, norm_mix, norm_ffn, norm_final, hg_w_in, hg_lb_logits, hg_out_norm, hg_w_out,
           hy_w_in, hy_conv_w, hy_conv_b, hy_fc1_w, hy_fc1_b, hy_fc2_w, hy_fc2_b, hy_fc3_w, hy_fc3_b, hy_fc4_w,
           hy_sin_freq, hy_skip, hy_w_out, ffn_w_gate, ffn_w_up, ffn_w_down):
    b1, t1, d = x_prompt.shape
    b2, t2, _ = x_sample.shape
    groups = ((b1, t1), (b2, t2))
    group_x = (x_prompt.reshape(b1 * t1, d), x_sample.reshape(b2 * t2, d))
    m_total = b1 * t1 + b2 * t2
    depth = norm_mix.shape[0]

    x = None
    for layer in range(depth):
        slot = layer // 2
        if layer % 2 == 1:
            hp = dict(fc1_w=hy_fc1_w[slot], fc1_b=hy_fc1_b[slot], fc2_w=hy_fc2_w[slot], fc2_b=hy_fc2_b[slot],
                      fc3_w=hy_fc3_w[slot], fc3_b=hy_fc3_b[slot], fc4_w=hy_fc4_w[slot],
                      sin_freq=hy_sin_freq[slot], skip=hy_skip[slot])
        row_starts = [0, groups[0][0] * groups[0][1]]
        hs = [(_rmsnorm(xg, norm_mix[layer], _BF16) if x is None
               else _rmsnorm(x, norm_mix[layer], _BF16, r0, bg * tg))
              for (bg, tg), xg, r0 in zip(groups, group_x, row_starts)]
        ys = []
        for (bg, tg), h in zip(groups, hs):
            if layer % 2 == 0:
                proj = _matmul(h, (hg_w_in,), slot, mode="plain", tm=2048, rows_resident=True)
                o_f = _hgrn_scan(proj, hg_lb_logits[0], (tg,) * bg, reverse=False, slot=slot)
                ys.append(_hgrn_scan(proj, hg_lb_logits[1], (tg,) * bg, reverse=True, slot=slot, o_fwd=o_f,
                                     gain=hg_out_norm[slot]))
            else:
                up = _inproj_conv3(_permute_rows(h, ((bg, tg),)), hy_w_in, slot, hy_conv_w[slot], hy_conv_b[slot],
                                   bg, tg)
                ys.append(_permute_rows(_hyena_long_conv(up, tg, d, hp), ((bg, tg),), inverse=True))
        w_out = hg_w_out if layer % 2 == 0 else hy_w_out
        for y, xg, r0 in zip(ys, group_x, row_starts):
            if layer == 0:
                x = _matmul(y, (w_out,), slot, mode="residual", residual=xg, out_into=(m_total, r0, x))
            else:
                x = _matmul(y, (w_out,), slot, mode="residual", residual=x, res_row0=r0)
        h2 = _rmsnorm(x, norm_ffn[layer], _BF16)
        a = _matmul(h2, (ffn_w_gate, ffn_w_up), layer, mode="swiglu", out_dtype=_BF16, tm=2048, tn=256,
                    rows_resident=True)
        x = _matmul(a, (ffn_w_down,), layer, mode="residual", residual=x, tm=256, single_buffer_w=True)

    return (_rmsnorm(x, norm_final, _F32, 0, b1 * t1).reshape(b1, t1, d),
            _rmsnorm(x, norm_final, _F32, b1 * t1, b2 * t2).reshape(b2, t2, d))
```

```python
import functools
import math

import numpy as np
import jax
import jax.numpy as jnp
from jax import lax
from jax.experimental import pallas as pl
from jax.experimental.pallas import tpu as pltpu

_F32 = jnp.float32
_BF16 = jnp.bfloat16

_EPS = 1e-6
_HEAD_DIM = 128
_CHUNK = 64
_SUB = 16
_EXP_CLAMP = 115.0
_HY_ORDER = 2
_HY_EMB = 33
_HY_BANDS = (_HY_EMB - 1) // 2
_HY_WIDTH = 64
_HY_MAX_DECAY = math.log(1e-2) / 0.3
_HY_MIN_DECAY = math.log(1e-2) / 1.5

_V7X_VMEM_LIMIT = 56 * 1024 * 1024
_NORM_ROWS = 512
_MM_ROWS = 1024
_MM_ROWS_RESIDENT = 2048
_MM_COLS = 512
_MM_ROWS_WIDE_K = 256
_SCAN_HEADS = 32
_SCAN_ROWS = 128
_DFT_COLS = 4096
_KERNEL_COLS = 2048


def _cparams(sem):
    return pltpu.CompilerParams(dimension_semantics=sem, vmem_limit_bytes=_V7X_VMEM_LIMIT)


def _tile(dim, want):
    if dim <= want:
        return dim
    t = want
    while t >= 128:
        if dim % t == 0:
            return t
        t -= 128
    return dim


def _row_tile(rows, want):
    t = min(rows, want)
    while rows % t:
        t //= 2
    return t


def _rmsnorm_body(x_ref, g_ref, o_ref):
    x = x_ref[...]
    ms = jnp.mean(x * x, axis=-1, keepdims=True)
    o_ref[...] = (x * lax.rsqrt(ms + _EPS) * g_ref[...]).astype(o_ref.dtype)


def _rmsnorm(x, gain, out_dtype, row0=0, rows=None):
    d = x.shape[1]
    m = x.shape[0] if rows is None else rows
    tm = _row_tile(math.gcd(m, row0) if row0 else m, _NORM_ROWS)
    blk0 = row0 // tm
    return pl.pallas_call(
        _rmsnorm_body,
        grid=(m // tm,),
        in_specs=[pl.BlockSpec((tm, d), lambda i: (blk0 + i, 0)),
                  pl.BlockSpec((1, d), lambda i: (0, 0))],
        out_specs=pl.BlockSpec((tm, d), lambda i: (i, 0)),
        out_shape=jax.ShapeDtypeStruct((m, d), out_dtype),
        compiler_params=_cparams(("parallel",)),
        name="rmsnorm",
    )(x, gain.reshape(1, d).astype(_F32))


def _mm_body(*refs, mode, n_w, rows_resident, n_in):
    a_ref, w_refs, o_ref = refs[0], refs[1:1 + n_w], refs[n_in]
    if rows_resident:
        ws = [w_ref[...].astype(_BF16) for w_ref in w_refs]
    else:
        wb_refs = refs[len(refs) - n_w:]

        @pl.when(pl.program_id(1) == 0)
        def _():
            for w_ref, wb_ref in zip(w_refs, wb_refs):
                wb_ref[...] = w_ref[...].astype(_BF16)

        ws = [wb_ref[...] for wb_ref in wb_refs]

    a = a_ref[...]
    if mode == "swiglu":
        g = jnp.dot(a, ws[0], preferred_element_type=_F32)
        u = jnp.dot(a, ws[1], preferred_element_type=_F32)
        o_ref[...] = (g * (0.5 * jnp.tanh(0.5 * g) + 0.5) * u).astype(o_ref.dtype)
    elif mode == "residual":
        o_ref[...] = refs[1 + n_w][...] + jnp.dot(a, ws[0], preferred_element_type=_F32)
    else:
        o_ref[...] = jnp.dot(a, ws[0], preferred_element_type=_F32).astype(o_ref.dtype)


def _matmul(a, ws, layer, *, mode, residual=None, res_row0=0, out_into=None, out_dtype=_F32, tm=None,
            rows_resident=False, single_buffer_w=False):
    m, k = a.shape
    n = ws[0].shape[2]
    row_offsets = [r for r in (res_row0, out_into[1] if out_into else 0) if r]
    tm = _row_tile(math.gcd(m, *row_offsets), tm or (_MM_ROWS_RESIDENT if rows_resident else _MM_ROWS))
    tn = _tile(n, _MM_COLS // len(ws))
    if rows_resident:
        grid = (m // tm, n // tn)
        ij = lambda g0, g1: (g0, g1)
        a_mode, w_mode = dict(pipeline_mode=pl.Buffered(1)), {}
        scratch = []
    else:
        grid = (n // tn, m // tm)
        ij = lambda g0, g1: (g1, g0)
        a_mode, w_mode = {}, (dict(pipeline_mode=pl.Buffered(1)) if single_buffer_w else {})
        scratch = [pltpu.VMEM((k, tn), _BF16) for _ in ws]
    in_specs = [pl.BlockSpec((tm, k), lambda g0, g1: (ij(g0, g1)[0], 0), **a_mode)]
    in_specs += [pl.BlockSpec((None, k, tn), lambda g0, g1: (layer, 0, ij(g0, g1)[1]), **w_mode) for _ in ws]
    args = [a, *[w.astype(_F32) for w in ws]]
    out_spec = pl.BlockSpec((tm, tn), lambda g0, g1: ij(g0, g1))
    out_shape = jax.ShapeDtypeStruct((m, n), out_dtype)
    aliases = {}
    if mode == "residual":
        def offset_spec(row0):
            blk0 = row0 // tm
            return pl.BlockSpec((tm, tn), lambda g0, g1: (blk0 + ij(g0, g1)[0], ij(g0, g1)[1]))

        in_specs.append(offset_spec(res_row0))
        args.append(residual)
        if out_into is None:
            out_spec = offset_spec(res_row0)
            out_shape = jax.ShapeDtypeStruct(residual.shape, residual.dtype)
            aliases = {len(args) - 1: 0}
        else:
            total_rows, row0, buffer = out_into
            out_spec = offset_spec(row0)
            out_shape = jax.ShapeDtypeStruct((total_rows, n), out_dtype)
            if buffer is not None:
                in_specs.append(pl.BlockSpec(memory_space=pl.ANY))
                args.append(buffer)
                aliases = {len(args) - 1: 0}
    return pl.pallas_call(
        functools.partial(_mm_body, mode=mode, n_w=len(ws), rows_resident=rows_resident,
                          n_in=len(args)),
        grid=grid,
        in_specs=in_specs,
        out_specs=out_spec,
        out_shape=out_shape,
        scratch_shapes=scratch,
        input_output_aliases=aliases,
        compiler_params=_cparams(("parallel", "arbitrary")),
        name="matmul_" + mode,
    )(*args)


def _scan_body(*refs, reverse, n_chunks, hb, reset_blocks, n_tblocks, slot, epilogue):
    if epilogue:
        q_ref, fl_ref, v_ref, lbl_ref, of_ref, gate_ref, gain_ref, o_ref, st_ref = refs
    else:
        q_ref, fl_ref, v_ref, lbl_ref, o_ref, st_ref = refs
    c_len, hd = _CHUNK, _HEAD_DIM

    t = pl.program_id(1)
    blk = (n_tblocks - 1 - t) if reverse else t
    reset = blk == reset_blocks[0]
    for r in reset_blocks[1:]:
        reset = jnp.logical_or(reset, blk == r)

    @pl.when(reset)
    def _():
        st_ref[...] = jnp.zeros_like(st_ref)

    lbl = lbl_ref[...]
    e = jnp.exp(lbl - jnp.max(lbl, axis=0, keepdims=True))
    lb_all = jnp.sum(e[: slot + 1], axis=0, keepdims=True) / jnp.sum(e, axis=0, keepdims=True)

    row = lax.broadcasted_iota(jnp.int32, (c_len, c_len), 0)
    col = lax.broadcasted_iota(jnp.int32, (c_len, c_len), 1)
    keep = (row <= col) if reverse else (row >= col)
    tri = keep.astype(_BF16)
    n_sub = c_len // _SUB
    w = hb * hd
    nt = (((1,), (1,)), ((), ()))
    tn = (((0,), (0,)), ((), ()))
    heads = [slice(h * hd, (h + 1) * hd) for h in range(hb)]

    def sigmoid(z):
        return 0.5 * jnp.tanh(0.5 * z) + 0.5

    def pad_rows(part, lo_r, dtype):
        pads = [jnp.zeros((lo_r, part.shape[1]), dtype), part,
                jnp.zeros((c_len - lo_r - part.shape[0], part.shape[1]), dtype)]
        return jnp.concatenate([p for p in pads if p.shape[0]], axis=0)

    def chunk_step(ci, carry):
        c = (n_chunks - 1 - ci) if reverse else ci
        r0 = pl.multiple_of(c * c_len, c_len)
        qr = q_ref[pl.ds(r0, c_len), :]
        q = qr * sigmoid(qr)
        f = lb_all + (1.0 - lb_all) * sigmoid(fl_ref[pl.ds(r0, c_len), :])
        k = 1.0 - f
        vb = v_ref[pl.ds(r0, c_len), :].astype(_BF16)
        lf = jnp.log2(f)
        hi = lf.astype(_BF16)
        rem = lf - hi.astype(_F32)
        mid = rem.astype(_BF16)
        lo = (rem - mid.astype(_F32)).astype(_BF16)
        cum = (jnp.dot(tri, hi, preferred_element_type=_F32) + jnp.dot(tri, mid, preferred_element_type=_F32)
               + jnp.dot(tri, lo, preferred_element_type=_F32))
        edge = cum[0:1] if reverse else cum[c_len - 1:c_len]
        k_end = (k * jnp.exp2(edge - cum)).astype(_BF16)
        st_decay = jnp.exp2(edge)
        refs = []
        for i in range(n_sub):
            if reverse:
                refs.append(cum[(i + 1) * _SUB:(i + 1) * _SUB + 1] if i < n_sub - 1 else jnp.zeros((1, w), _F32))
            else:
                refs.append(cum[i * _SUB - 1:i * _SUB] if i > 0 else jnp.zeros((1, w), _F32))
        ref_rows = jnp.concatenate([jnp.broadcast_to(r, (_SUB, w)) for r in refs], axis=0)
        a_f32 = q * jnp.exp2(cum - ref_rows)
        a = a_f32.astype(_BF16)
        ref_decay = jnp.concatenate([jnp.broadcast_to(jnp.exp2(r), (_SUB, w)) for r in refs], axis=0)
        q_in = (a_f32 * ref_decay).astype(_BF16)
        bs = []
        for i in range(n_sub):
            lo_r, hi_r = (i * _SUB, c_len) if reverse else (0, (i + 1) * _SUB)
            part = k[lo_r:hi_r] * jnp.exp2(jnp.minimum(refs[i] - cum[lo_r:hi_r], _EXP_CLAMP))
            bs.append(pad_rows(part.astype(_BF16), lo_r, _BF16))

        sts = [st_ref[h] for h in range(hb)]
        o_inter = [lax.dot_general(q_in[:, ln], sts[h].astype(_BF16), nt, preferred_element_type=_F32)
                   for h, ln in enumerate(heads)]
        a_cat = [jnp.concatenate([pad_rows(a[i * _SUB:(i + 1) * _SUB, ln], i * _SUB, _BF16) for i in range(n_sub)],
                                 axis=1) for ln in heads]
        b_cat = [jnp.concatenate([bs[i][:, ln] for i in range(n_sub)], axis=1) for ln in heads]
        scores = [lax.dot_general(a_cat[h], b_cat[h], nt, preferred_element_type=_F32) for h in range(hb)]
        scores = [jnp.where(keep, s, 0.0).astype(_BF16) for s in scores]
        o_intra = [jnp.dot(scores[h], vb[:, ln], preferred_element_type=_F32) for h, ln in enumerate(heads)]
        st_new = [lax.dot_general(vb[:, ln], k_end[:, ln], tn, preferred_element_type=_F32) for ln in heads]
        for h, ln in enumerate(heads):
            st_ref[h] = sts[h] * st_decay[:, ln] + st_new[h]
        o = jnp.concatenate([o_inter[h] + o_intra[h] for h in range(hb)], axis=1)
        if epilogue:
            tot = of_ref[pl.ds(r0, c_len), :] + o
            ys = []
            for ln in heads:
                th = tot[:, ln]
                ys.append(th * lax.rsqrt(jnp.mean(th * th, axis=-1, keepdims=True) + _EPS))
            g = gate_ref[pl.ds(r0, c_len), :]
            y = jnp.concatenate(ys, axis=1) * gain_ref[...]
            o_ref[pl.ds(r0, c_len), :] = (y * (g * sigmoid(g))).astype(o_ref.dtype)
        else:
            o_ref[pl.ds(r0, c_len), :] = o
        return carry

    lax.fori_loop(0, n_chunks, chunk_step, 0)


def _hgrn_scan(proj, lb_logits, seg_lens, *, reverse, slot, o_fwd=None, gain=None):
    m, d5 = proj.shape
    d = d5 // 5
    hb = math.gcd(_SCAN_HEADS, d // _HEAD_DIM)
    w = hb * _HEAD_DIM
    nhb = d // w
    epilogue = o_fwd is not None
    tb = _row_tile(math.gcd(*seg_lens) if len(seg_lens) > 1 else seg_lens[0],
                   _SCAN_ROWS if epilogue else 2 * _SCAN_ROWS)
    n_t = m // tb
    bounds = np.cumsum([0] + list(seg_lens)) // tb
    reset_blocks = tuple(int(b) - 1 for b in bounds[1:]) if reverse else tuple(int(b) for b in bounds[:-1])

    def tmap(t):
        return (n_t - 1 - t) if reverse else t

    def stream(s):
        return pl.BlockSpec((tb, w), lambda h, t, s=s: (tmap(t), s * nhb + h))

    n_slots = lb_logits.shape[0]
    in_specs = [stream(0), stream(2 if reverse else 1), stream(3),
                pl.BlockSpec((n_slots, w), lambda h, t: (0, h))]
    args = [proj, proj, proj, lb_logits.astype(_F32)]
    if epilogue:
        in_specs += [pl.BlockSpec((tb, w), lambda h, t: (tmap(t), h)), stream(4),
                     pl.BlockSpec((1, w), lambda h, t: (0, h))]
        args += [o_fwd, proj, gain.reshape(1, d).astype(_F32)]
    body = functools.partial(_scan_body, reverse=reverse, n_chunks=tb // _CHUNK, hb=hb,
                             reset_blocks=reset_blocks, n_tblocks=n_t, slot=slot, epilogue=epilogue)
    return pl.pallas_call(
        body,
        grid=(nhb, n_t),
        in_specs=in_specs,
        out_specs=pl.BlockSpec((tb, w), lambda h, t: (tmap(t), h)),
        out_shape=jax.ShapeDtypeStruct((m, d), _BF16 if epilogue else _F32),
        scratch_shapes=[pltpu.VMEM((hb, _HEAD_DIM, _HEAD_DIM), _F32)],
        compiler_params=_cparams(("parallel", "arbitrary")),
        name="hgrn_scan_bwd" if reverse else "hgrn_scan_fwd",
    )(*args)


def _inproj_conv3_body(prev_ref, cur_ref, nxt_ref, w_ref, cw_ref, cb_ref, o_ref, *, h1, tiles_per_seq):
    i_loc = lax.rem(pl.program_id(0), tiles_per_seq)
    wb = w_ref[...].astype(_BF16)
    u = jnp.dot(cur_ref[...], wb, preferred_element_type=_F32)
    u_prev = jnp.dot(prev_ref[...], wb, preferred_element_type=_F32)
    u_next = jnp.dot(nxt_ref[...], wb, preferred_element_type=_F32)
    rows = lax.broadcasted_iota(jnp.int32, u_prev.shape, 0)
    u_prev = jnp.where(i_loc == 0, jnp.where(rows == 0, 0.0, pltpu.roll(u_prev, 1, 0)), u_prev)
    u_next = jnp.where(i_loc == tiles_per_seq - 1,
                       jnp.where(rows == h1 - 1, 0.0, pltpu.roll(u_next, h1 - 1, 0)), u_next)
    if u.shape[0] > h1:
        up = jnp.concatenate([u_prev, u[:-h1]], axis=0)
        dn = jnp.concatenate([u[h1:], u_next], axis=0)
    else:
        up, dn = u_prev, u_next
    cw = cw_ref[...]
    o_ref[...] = up * cw[0:1] + u * cw[1:2] + dn * cw[2:3] + cb_ref[...]


def _inproj_conv3(hg, w, layer, conv_w, conv_b, batch, seq_len):
    m, k = hg.shape
    n = w.shape[2]
    n1, n2 = _fft_plan(seq_len)
    h1 = n1 // 2
    jb = _row_tile(n2, max(1, _MM_ROWS_RESIDENT // h1))
    tm = jb * h1
    tps = n2 // jb
    tn = _tile(n, _MM_COLS)

    def seq_slab(i, s):
        return lax.div(i, tps) * n2 + s

    def prev_slab(i, j):
        il = lax.rem(i, tps)
        return (seq_slab(i, jnp.where(il == 0, n2, il * jb) - 1), 0)

    def next_slab(i, j):
        il = lax.rem(i, tps)
        return (seq_slab(i, jnp.where(il == tps - 1, 0, (il + 1) * jb)), 0)

    body = functools.partial(_inproj_conv3_body, h1=h1, tiles_per_seq=tps)
    out = pl.pallas_call(
        body,
        grid=(m // tm, n // tn),
        in_specs=[pl.BlockSpec((h1, k), prev_slab),
                  pl.BlockSpec((tm, k), lambda i, j: (i, 0), pipeline_mode=pl.Buffered(1)),
                  pl.BlockSpec((h1, k), next_slab),
                  pl.BlockSpec((None, k, tn), lambda i, j: (layer, 0, j)),
                  pl.BlockSpec((3, tn), lambda i, j: (0, j)),
                  pl.BlockSpec((1, tn), lambda i, j: (0, j))],
        out_specs=pl.BlockSpec((tm, tn), lambda i, j: (i, j)),
        out_shape=jax.ShapeDtypeStruct((m, n), _F32),
        compiler_params=_cparams(("parallel", "arbitrary")),
        name="hyena_inproj_conv3",
    )(hg, hg, hg, w.astype(_F32), conv_w.astype(_F32), conv_b.reshape(1, n).astype(_F32))
    return out.reshape(batch, n2, h1, n)


def _permute_rows(x, groups, inverse=False):
    parts, row0 = [], 0
    for batch, seq_len in groups:
        n1, n2 = _fft_plan(seq_len)
        shape = (batch, n2, n1 // 2, -1) if inverse else (batch, n1 // 2, n2, -1)
        rows = batch * seq_len
        parts.append(x[row0:row0 + rows].reshape(shape).transpose(0, 2, 1, 3).reshape(rows, -1))
        row0 += rows
    return jnp.concatenate(parts, axis=0)


def _fft_plan(seq_len):
    n = 2 * seq_len
    n1 = 256 if n >= 32768 else 128
    while n // n1 < 8:
        n1 //= 2
    return n1, n // n1


def _kept_k1(n1):
    return n1 // 2 + 8


def _dft_tables(seq_len):
    n = 2 * seq_len
    n1, n2 = _fft_plan(seq_len)
    h1 = n1 // 2
    kp = _kept_k1(n1)
    i2 = lax.broadcasted_iota(jnp.int32, (n2, kp, n1), 0)
    k1 = lax.broadcasted_iota(jnp.int32, (n2, kp, n1), 1)
    i1 = lax.broadcasted_iota(jnp.int32, (n2, kp, n1), 2)
    phase = (i2 * k1 + n2 * (i1 * k1)) % n
    ang = phase.astype(_F32) * (2.0 * math.pi / n)
    cos = jnp.where(k1 <= h1, jnp.cos(ang), 0.0)
    sin = jnp.where(k1 <= h1, jnp.sin(ang), 0.0)
    fwd_a = jnp.concatenate([cos, -sin], axis=1).astype(_BF16)
    weight = jnp.where((k1 == 0) | (k1 == h1), 1.0, 2.0) * (1.0 / n)
    cos_t, sin_t = (cos * weight).transpose(0, 2, 1)[:, :h1], (sin * weight).transpose(0, 2, 1)[:, :h1]
    inv_a = jnp.concatenate([cos_t, -sin_t], axis=2).astype(_BF16)
    idx = np.arange(n2)
    ang_b = 2.0 * np.pi * ((idx[:, None] * idx[None, :]) % n2) / n2
    cb, sb = np.cos(ang_b), np.sin(ang_b)
    fwd_b = jnp.asarray(np.block([[cb, sb], [-sb, cb]]), _BF16)
    inv_b = jnp.asarray(np.block([[cb, -sb], [sb, cb]]), _BF16)
    fwd_a_half = fwd_a if h1 % 128 == 0 else fwd_a[:, :, :h1]
    return dict(fwd_a_full=fwd_a, fwd_a_half=fwd_a_half, inv_a=inv_a, fwd_b=fwd_b, inv_b=inv_b)


def _swap_stage_layout(x, groups, rows):
    b, _, _, d = x.shape
    return x.reshape(b, groups, 2, rows, d).transpose(0, 3, 2, 1, 4).reshape(b, rows, 2 * groups, d)


def _stage_a_fwd_body(x_ref, m_ref, o_ref):
    o_ref[...] = jnp.dot(m_ref[...], x_ref[...].astype(_BF16), preferred_element_type=_F32).astype(o_ref.dtype)


def _stage_a_fwd(xp, col_off, mats, d):
    b, n2, h1, _ = xp.shape
    r = mats.shape[1]
    td = _tile(d, _DFT_COLS)
    off = col_off // td
    return pl.pallas_call(
        _stage_a_fwd_body,
        grid=(b, n2, d // td),
        in_specs=[pl.BlockSpec((None, None, h1, td), lambda bb, j, c: (bb, j, 0, off + c)),
                  pl.BlockSpec((None, r, h1), lambda bb, j, c: (j, 0, 0))],
        out_specs=pl.BlockSpec((None, None, r, td), lambda bb, j, c: (bb, j, 0, c)),
        out_shape=jax.ShapeDtypeStruct((b, n2, r, d), _BF16),
        compiler_params=_cparams(("parallel", "parallel", "parallel")),
        name="hyena_stage_a_fwd",
    )(xp, mats)


def _stage_b_conv_body(g_ref, kg_ref, l1_ref, fb_ref, fbi_ref, o_ref, ks_ref):
    n2 = g_ref.shape[0] // 2
    fb = fb_ref[...]
    x = jnp.dot(fb, g_ref[...], preferred_element_type=_F32)

    @pl.when(pl.program_id(2) == 0)
    def _():
        ks_ref[...] = jnp.dot(fb, kg_ref[...], preferred_element_type=_F32) * (1.0 / (l1_ref[...] + _EPS))

    xr, xi = x[:n2], x[n2:]
    kr, ki = ks_ref[:n2, :], ks_ref[n2:, :]
    y = jnp.concatenate([xr * kr - xi * ki, xr * ki + xi * kr], axis=0).astype(_BF16)
    o_ref[...] = jnp.dot(fbi_ref[...], y, preferred_element_type=_F32).astype(o_ref.dtype)


def _stage_b_conv(g, kg, l1, fwd_b, inv_b):
    b, n1, r, d = g.shape
    td = _tile(d, _DFT_COLS)
    return pl.pallas_call(
        _stage_b_conv_body,
        grid=(n1, d // td, b),
        in_specs=[pl.BlockSpec((None, None, r, td), lambda k, c, bb: (bb, k, 0, c)),
                  pl.BlockSpec((None, r, td), lambda k, c, bb: (k, 0, c)),
                  pl.BlockSpec((1, td), lambda k, c, bb: (0, c)),
                  pl.BlockSpec((r, r), lambda k, c, bb: (0, 0)),
                  pl.BlockSpec((r, r), lambda k, c, bb: (0, 0))],
        out_specs=pl.BlockSpec((None, None, r, td), lambda k, c, bb: (bb, k, 0, c)),
        out_shape=jax.ShapeDtypeStruct((b, n1, r, d), _BF16),
        scratch_shapes=[pltpu.VMEM((r, td), _F32)],
        compiler_params=_cparams(("parallel", "parallel", "arbitrary")),
        name="hyena_stage_b_conv",
    )(g, kg, l1, fwd_b, inv_b)


def _stage_a_inv_body(h_ref, m_ref, y_ref, g_ref, s_ref, *rest):
    conv = jnp.dot(m_ref[...], h_ref[...], preferred_element_type=_F32)
    y_new = g_ref[...] * (conv + y_ref[...] * s_ref[...])
    if len(rest) == 1:
        rest[0][...] = y_new.astype(rest[0].dtype)
    else:
        fwd_ref, o_ref, gt_ref = rest
        o_ref[...] = y_new
        gt_ref[...] = jnp.dot(fwd_ref[...], y_new.astype(_BF16), preferred_element_type=_F32).astype(gt_ref.dtype)


def _stage_a_inv(ht, mats, yp, y_off, gp, g_off, skip, out_dtype, next_mats=None):
    b, n2, r, d = ht.shape
    h1 = mats.shape[1]
    td = _tile(d, _DFT_COLS)
    yo, go = y_off // td, g_off // td
    in_specs = [pl.BlockSpec((None, None, r, td), lambda bb, j, c: (bb, j, 0, c)),
                pl.BlockSpec((None, h1, r), lambda bb, j, c: (j, 0, 0)),
                pl.BlockSpec((None, None, h1, td), lambda bb, j, c: (bb, j, 0, yo + c)),
                pl.BlockSpec((None, None, h1, td), lambda bb, j, c: (bb, j, 0, go + c)),
                pl.BlockSpec((1, td), lambda bb, j, c: (0, c))]
    args = [ht, mats, yp, gp, skip.reshape(1, d).astype(_F32)]
    out_specs = pl.BlockSpec((None, None, h1, td), lambda bb, j, c: (bb, j, 0, c))
    out_shape = jax.ShapeDtypeStruct((b, n2, h1, d), out_dtype)
    if next_mats is not None:
        in_specs.append(pl.BlockSpec((None, r, h1), lambda bb, j, c: (j, 0, 0)))
        args.append(next_mats)
        out_specs = [out_specs, pl.BlockSpec((None, None, r, td), lambda bb, j, c: (bb, j, 0, c))]
        out_shape = [out_shape, jax.ShapeDtypeStruct((b, n2, r, d), _BF16)]
    return pl.pallas_call(
        _stage_a_inv_body,
        grid=(b, n2, d // td),
        in_specs=in_specs,
        out_specs=out_specs,
        out_shape=out_shape,
        compiler_params=_cparams(("parallel", "parallel", "parallel")),
        name="hyena_stage_a_inv",
    )(*args)


def _filter_pos(shape, j0, n1, n2, seq_len):
    r = lax.broadcasted_iota(jnp.int32, shape, 0)
    i1 = jnp.bitwise_and(r, n1 - 1)
    j = j0 + lax.shift_right_logical(r, n1.bit_length() - 1)
    n = n2 * i1 + j
    return jnp.where(i1 < n1 // 2, n, 2 * seq_len - n)


def _filter_mlp_body(band_ref, w1_ref, b1_ref, w2_ref, b2_ref, w3_ref, b3_ref, fr_ref, o_ref,
                     *, jb, n1, n2, seq_len):
    hi = lax.Precision.HIGHEST
    rows = jb * n1
    p = _filter_pos((rows, 128), pl.program_id(0) * jb, n1, n2, seq_len).astype(_F32)
    lane = lax.broadcasted_iota(jnp.int32, (rows, 128), 1)
    ang = (2.0 * math.pi / seq_len) * p * band_ref[...]
    feat = jnp.where(lane == 0, p * (1.0 / (seq_len - 1)),
                     jnp.where(lane <= _HY_BANDS, jnp.cos(ang),
                               jnp.where(lane <= 2 * _HY_BANDS, -jnp.sin(ang), 0.0)))
    fr = fr_ref[...]
    z = jnp.sin(fr * (jnp.dot(feat, w1_ref[...], preferred_element_type=_F32, precision=hi) + b1_ref[...]))
    z = jnp.sin(fr * (jnp.dot(z, w2_ref[...], preferred_element_type=_F32, precision=hi) + b2_ref[...]))
    z = jnp.sin(fr * (jnp.dot(z, w3_ref[...], preferred_element_type=_F32, precision=hi) + b3_ref[...]))
    o_ref[...] = z.reshape(jb, n1, _HY_WIDTH)


def _filter_mlp(seq_len, fc1_w, fc1_b, fc2_w, fc2_b, fc3_w, fc3_b, sin_freq):
    n1, n2 = _fft_plan(seq_len)
    jb = _row_tile(n2, 8)
    wd = _HY_WIDTH
    bands = jnp.linspace(1e-4, _HY_BANDS - 1, _HY_BANDS, dtype=_F32)
    band_row = jnp.zeros((1, 128), _F32).at[0, 1:1 + _HY_BANDS].set(bands).at[0, 1 + _HY_BANDS:_HY_EMB].set(bands)
    w1 = jnp.zeros((128, wd), _F32).at[:_HY_EMB].set(fc1_w.astype(_F32))
    const = lambda shape: pl.BlockSpec(shape, lambda i: (0,) * len(shape))
    row = lambda a: a.reshape(1, wd).astype(_F32)
    body = functools.partial(_filter_mlp_body, jb=jb, n1=n1, n2=n2, seq_len=seq_len)
    return pl.pallas_call(
        body,
        grid=(n2 // jb,),
        in_specs=[const((1, 128)), const((128, wd)), const((1, wd)), const((wd, wd)), const((1, wd)),
                  const((wd, wd)), const((1, wd)), const((1, wd))],
        out_specs=pl.BlockSpec((jb, n1, wd), lambda i: (i, 0, 0)),
        out_shape=jax.ShapeDtypeStruct((n2, n1, wd), _F32),
        compiler_params=_cparams(("parallel",)),
        name="hyena_filter_mlp",
    )(band_row, w1, row(fc1_b), fc2_w.astype(_F32), row(fc2_b), fc3_w.astype(_F32), row(fc3_b), row(sin_freq))


def _kernel_stage_a_body(z_ref, wf_ref, wb_ref, dl_ref, m_ref, o_ref, l1_ref, *, n1, n2, seq_len):
    j = pl.program_id(1)
    h1 = n1 // 2
    td = wf_ref.shape[1]
    p = _filter_pos((n1, td), j, n1, n2, seq_len)
    window = jnp.exp(-(p.astype(_F32) * (1.0 / (seq_len - 1))) * dl_ref[...])
    z = z_ref[...]
    zr = lax.broadcasted_iota(jnp.int32, (n1, 2 * _HY_WIDTH), 0)
    zc = lax.broadcasted_iota(jnp.int32, (n1, 2 * _HY_WIDTH), 1)
    zz = jnp.where((zr < h1) == (zc < _HY_WIDTH), jnp.concatenate([z, z], axis=1), 0.0)
    wcat = jnp.concatenate([wf_ref[...], wb_ref[...]], axis=0)
    kern = jnp.dot(zz.astype(_BF16), wcat.astype(_BF16), preferred_element_type=_F32) * window
    kern = jnp.where(p == seq_len, 0.0, kern)

    @pl.when(j == 0)
    def _():
        l1_ref[...] = jnp.zeros_like(l1_ref)

    l1_ref[...] += jnp.sum(jnp.abs(kern), axis=0, keepdims=True)
    o_ref[...] = jnp.dot(m_ref[...], kern.astype(_BF16), preferred_element_type=_F32).astype(o_ref.dtype)


def _kernel_stage_a(z, fc4_w, order, deltas, mats, seq_len, d):
    n1, n2 = _fft_plan(seq_len)
    r = mats.shape[1]
    td = _tile(d, _KERNEL_COLS)
    nd = d // td
    fwd_off, bwd_off = order * 2 * nd, order * 2 * nd + nd
    body = functools.partial(_kernel_stage_a_body, n1=n1, n2=n2, seq_len=seq_len)
    return pl.pallas_call(
        body,
        grid=(nd, n2),
        in_specs=[pl.BlockSpec((None, n1, _HY_WIDTH), lambda c, j: (j, 0, 0)),
                  pl.BlockSpec((_HY_WIDTH, td), lambda c, j: (0, fwd_off + c)),
                  pl.BlockSpec((_HY_WIDTH, td), lambda c, j: (0, bwd_off + c)),
                  pl.BlockSpec((1, td), lambda c, j: (0, c)),
                  pl.BlockSpec((None, r, n1), lambda c, j: (j, 0, 0))],
        out_specs=[pl.BlockSpec((None, r, td), lambda c, j: (j, 0, c)),
                   pl.BlockSpec((1, td), lambda c, j: (0, c))],
        out_shape=[jax.ShapeDtypeStruct((n2, r, d), _BF16), jax.ShapeDtypeStruct((1, d), _F32)],
        compiler_params=_cparams(("parallel", "arbitrary")),
        name="hyena_kernel_stage_a",
    )(z, fc4_w.astype(_F32), fc4_w.astype(_F32), deltas, mats)


def _hyena_long_conv(up, seq_len, d, hp):
    n1, n2 = _fft_plan(seq_len)
    kp = _kept_k1(n1)
    tabs = _dft_tables(seq_len)
    deltas = jnp.abs(jnp.linspace(_HY_MIN_DECAY, _HY_MAX_DECAY, d, dtype=_F32)).reshape(1, d)
    z = _filter_mlp(seq_len, hp["fc1_w"], hp["fc1_b"], hp["fc2_w"], hp["fc2_b"], hp["fc3_w"], hp["fc3_b"],
                    hp["sin_freq"])
    y, y_off = up, 0
    g = _stage_a_fwd(y, y_off, tabs["fwd_a_half"], d)
    for order in range(_HY_ORDER):
        kg, l1 = _kernel_stage_a(z, hp["fc4_w"], order, deltas, tabs["fwd_a_full"], seq_len, d)
        h = _stage_b_conv(_swap_stage_layout(g, n2, kp), _swap_stage_layout(kg[None], n2, kp)[0], l1,
                          tabs["fwd_b"], tabs["inv_b"])
        ht = _swap_stage_layout(h, kp, n2)
        if order == _HY_ORDER - 1:
            y = _stage_a_inv(ht, tabs["inv_a"], y, y_off, up, (order + 1) * d, hp["skip"][order], _BF16)
        else:
            y, g = _stage_a_inv(ht, tabs["inv_a"], y, y_off, up, (order + 1) * d, hp["skip"][order], _F32,
                                next_mats=tabs["fwd_a_half"])
        y_off = 0
    return y.reshape(-1, d)


def kernel(x_prompt, x_sample, norm_mix, norm_ffn, norm_final, hg_w_in, hg_lb_logits, hg_out_norm, hg_w_out,
           hy_w_in, hy_conv_w, hy_conv_b, hy_fc1_w, hy_fc1_b, hy_fc2_w, hy_fc2_b, hy_fc3_w, hy_fc3_b, hy_fc4_w,
           hy_sin_freq, hy_skip, hy_w_out, ffn_w_gate, ffn_w_up, ffn_w_down):
    b1, t1, d = x_prompt.shape
    b2, t2, _ = x_sample.shape
    groups = ((b1, t1), (b2, t2))
    group_x = (x_prompt.reshape(b1 * t1, d), x_sample.reshape(b2 * t2, d))
    m_total = b1 * t1 + b2 * t2
    depth = norm_mix.shape[0]

    x = None
    for layer in range(depth):
        slot = layer // 2
        if layer % 2 == 1:
            hp = dict(fc1_w=hy_fc1_w[slot], fc1_b=hy_fc1_b[slot], fc2_w=hy_fc2_w[slot], fc2_b=hy_fc2_b[slot],
                      fc3_w=hy_fc3_w[slot], fc3_b=hy_fc3_b[slot], fc4_w=hy_fc4_w[slot],
                      sin_freq=hy_sin_freq[slot], skip=hy_skip[slot])
        row_starts = [0, groups[0][0] * groups[0][1]]
        hs = [(_rmsnorm(xg, norm_mix[layer], _BF16) if x is None
               else _rmsnorm(x, norm_mix[layer], _BF16, r0, bg * tg))
              for (bg, tg), xg, r0 in zip(groups, group_x, row_starts)]
        ys = []
        for (bg, tg), h in zip(groups, hs):
            if layer % 2 == 0:
                proj = _matmul(h, (hg_w_in,), slot, mode="plain", rows_resident=True)
                o_f = _hgrn_scan(proj, hg_lb_logits[0], (tg,) * bg, reverse=False, slot=slot)
                ys.append(_hgrn_scan(proj, hg_lb_logits[1], (tg,) * bg, reverse=True, slot=slot, o_fwd=o_f,
                                     gain=hg_out_norm[slot]))
            else:
                up = _inproj_conv3(_permute_rows(h, ((bg, tg),)), hy_w_in, slot, hy_conv_w[slot], hy_conv_b[slot],
                                   bg, tg)
                ys.append(_permute_rows(_hyena_long_conv(up, tg, d, hp), ((bg, tg),), inverse=True))
        w_out = hg_w_out if layer % 2 == 0 else hy_w_out
        for y, xg, r0 in zip(ys, group_x, row_starts):
            if layer == 0:
                x = _matmul(y, (w_out,), slot, mode="residual", residual=xg, out_into=(m_total, r0, x))
            else:
                x = _matmul(y, (w_out,), slot, mode="residual", residual=x, res_row0=r0)
        h2 = _rmsnorm(x, norm_ffn[layer], _BF16)
        a = _matmul(h2, (ffn_w_gate, ffn_w_up), layer, mode="swiglu", out_dtype=_BF16, rows_resident=True)
        x = _matmul(a, (ffn_w_down,), layer, mode="residual", residual=x, tm=_MM_ROWS_WIDE_K, single_buffer_w=True)

    return (_rmsnorm(x, norm_final, _F32, 0, b1 * t1).reshape(b1, t1, d),
            _rmsnorm(x, norm_final, _F32, b1 * t1, b2 * t2).reshape(b2, t2, d))
```

```python
import functools
import math

import numpy as np
import jax
import jax.numpy as jnp
from jax import lax
from jax.experimental import pallas as pl
from jax.experimental.pallas import tpu as pltpu

_F32 = jnp.float32
_BF16 = jnp.bfloat16

_EPS = 1e-6
_HEAD_DIM = 128
_CHUNK = 64
_SUB = 16
_EXP_CLAMP = 115.0
_HY_ORDER = 2
_HY_EMB = 33
_HY_BANDS = (_HY_EMB - 1) // 2
_HY_WIDTH = 64
_HY_MAX_DECAY = math.log(1e-2) / 0.3
_HY_MIN_DECAY = math.log(1e-2) / 1.5

_V7X_VMEM_LIMIT = 56 * 1024 * 1024
_NORM_ROWS = 512
_MM_ROWS = 1024
_MM_ROWS_RESIDENT = 2048
_MM_COLS = 512
_MM_ROWS_WIDE_K = 256
_SCAN_HEADS = 32
_SCAN_ROWS = 128
_DFT_COLS = 4096
_DFT_GROUPS_PER_STEP = 2
_KERNEL_COLS = 2048


def _cparams(sem):
    return pltpu.CompilerParams(dimension_semantics=sem, vmem_limit_bytes=_V7X_VMEM_LIMIT)


def _tile(dim, want):
    if dim <= want:
        return dim
    t = want
    while t >= 128:
        if dim % t == 0:
            return t
        t -= 128
    return dim


def _row_tile(rows, want):
    t = min(rows, want)
    while rows % t:
        t //= 2
    return t


def _rmsnorm_body(x_ref, g_ref, o_ref):
    x = x_ref[...]
    ms = jnp.mean(x * x, axis=-1, keepdims=True)
    o_ref[...] = (x * lax.rsqrt(ms + _EPS) * g_ref[...]).astype(o_ref.dtype)


def _rmsnorm(x, gain, out_dtype, row0=0, rows=None):
    d = x.shape[1]
    m = x.shape[0] if rows is None else rows
    tm = _row_tile(math.gcd(m, row0) if row0 else m, _NORM_ROWS)
    blk0 = row0 // tm
    return pl.pallas_call(
        _rmsnorm_body,
        grid=(m // tm,),
        in_specs=[pl.BlockSpec((tm, d), lambda i: (blk0 + i, 0)),
                  pl.BlockSpec((1, d), lambda i: (0, 0))],
        out_specs=pl.BlockSpec((tm, d), lambda i: (i, 0)),
        out_shape=jax.ShapeDtypeStruct((m, d), out_dtype),
        compiler_params=_cparams(("parallel",)),
        name="rmsnorm",
    )(x, gain.reshape(1, d).astype(_F32))


def _mm_body(*refs, mode, n_w, rows_resident, n_in):
    a_ref, w_refs, o_ref = refs[0], refs[1:1 + n_w], refs[n_in]
    if rows_resident:
        ws = [w_ref[...].astype(_BF16) for w_ref in w_refs]
    else:
        wb_refs = refs[len(refs) - n_w:]

        @pl.when(pl.program_id(1) == 0)
        def _():
            for w_ref, wb_ref in zip(w_refs, wb_refs):
                wb_ref[...] = w_ref[...].astype(_BF16)

        ws = [wb_ref[...] for wb_ref in wb_refs]

    a = a_ref[...]
    if mode == "swiglu":
        g = jnp.dot(a, ws[0], preferred_element_type=_F32)
        u = jnp.dot(a, ws[1], preferred_element_type=_F32)
        o_ref[...] = (g * (0.5 * jnp.tanh(0.5 * g) + 0.5) * u).astype(o_ref.dtype)
    elif mode == "residual":
        o_ref[...] = refs[1 + n_w][...] + jnp.dot(a, ws[0], preferred_element_type=_F32)
    else:
        o_ref[...] = jnp.dot(a, ws[0], preferred_element_type=_F32).astype(o_ref.dtype)


def _matmul(a, ws, layer, *, mode, residual=None, res_row0=0, out_into=None, out_dtype=_F32, tm=None,
            rows_resident=False, single_buffer_w=False):
    m, k = a.shape
    n = ws[0].shape[2]
    row_offsets = [r for r in (res_row0, out_into[1] if out_into else 0) if r]
    tm = _row_tile(math.gcd(m, *row_offsets), tm or (_MM_ROWS_RESIDENT if rows_resident else _MM_ROWS))
    tn = _tile(n, _MM_COLS // len(ws))
    if rows_resident:
        grid = (m // tm, n // tn)
        ij = lambda g0, g1: (g0, g1)
        a_mode, w_mode = dict(pipeline_mode=pl.Buffered(1)), {}
        scratch = []
    else:
        grid = (n // tn, m // tm)
        ij = lambda g0, g1: (g1, g0)
        a_mode, w_mode = {}, (dict(pipeline_mode=pl.Buffered(1)) if single_buffer_w else {})
        scratch = [pltpu.VMEM((k, tn), _BF16) for _ in ws]
    in_specs = [pl.BlockSpec((tm, k), lambda g0, g1: (ij(g0, g1)[0], 0), **a_mode)]
    in_specs += [pl.BlockSpec((None, k, tn), lambda g0, g1: (layer, 0, ij(g0, g1)[1]), **w_mode) for _ in ws]
    args = [a, *[w.astype(_F32) for w in ws]]
    out_spec = pl.BlockSpec((tm, tn), lambda g0, g1: ij(g0, g1))
    out_shape = jax.ShapeDtypeStruct((m, n), out_dtype)
    aliases = {}
    if mode == "residual":
        def offset_spec(row0):
            blk0 = row0 // tm
            return pl.BlockSpec((tm, tn), lambda g0, g1: (blk0 + ij(g0, g1)[0], ij(g0, g1)[1]))

        in_specs.append(offset_spec(res_row0))
        args.append(residual)
        if out_into is None:
            out_spec = offset_spec(res_row0)
            out_shape = jax.ShapeDtypeStruct(residual.shape, residual.dtype)
            aliases = {len(args) - 1: 0}
        else:
            total_rows, row0, buffer = out_into
            out_spec = offset_spec(row0)
            out_shape = jax.ShapeDtypeStruct((total_rows, n), out_dtype)
            if buffer is not None:
                in_specs.append(pl.BlockSpec(memory_space=pl.ANY))
                args.append(buffer)
                aliases = {len(args) - 1: 0}
    return pl.pallas_call(
        functools.partial(_mm_body, mode=mode, n_w=len(ws), rows_resident=rows_resident,
                          n_in=len(args)),
        grid=grid,
        in_specs=in_specs,
        out_specs=out_spec,
        out_shape=out_shape,
        scratch_shapes=scratch,
        input_output_aliases=aliases,
        compiler_params=_cparams(("parallel", "arbitrary")),
        name="matmul_" + mode,
    )(*args)


def _scan_body(*refs, reverse, n_chunks, hb, reset_blocks, n_tblocks, slot, epilogue):
    if epilogue:
        q_ref, fl_ref, v_ref, lbl_ref, of_ref, gate_ref, gain_ref, o_ref, st_ref = refs
    else:
        q_ref, fl_ref, v_ref, lbl_ref, o_ref, st_ref = refs
    c_len, hd = _CHUNK, _HEAD_DIM

    t = pl.program_id(1)
    blk = (n_tblocks - 1 - t) if reverse else t
    reset = blk == reset_blocks[0]
    for r in reset_blocks[1:]:
        reset = jnp.logical_or(reset, blk == r)

    @pl.when(reset)
    def _():
        st_ref[...] = jnp.zeros_like(st_ref)

    lbl = lbl_ref[...]
    e = jnp.exp(lbl - jnp.max(lbl, axis=0, keepdims=True))
    lb_all = jnp.sum(e[: slot + 1], axis=0, keepdims=True) / jnp.sum(e, axis=0, keepdims=True)

    row = lax.broadcasted_iota(jnp.int32, (c_len, c_len), 0)
    col = lax.broadcasted_iota(jnp.int32, (c_len, c_len), 1)
    keep = (row <= col) if reverse else (row >= col)
    tri = keep.astype(_BF16)
    n_sub = c_len // _SUB
    w = hb * hd
    nt = (((1,), (1,)), ((), ()))
    tn = (((0,), (0,)), ((), ()))
    heads = [slice(h * hd, (h + 1) * hd) for h in range(hb)]

    def sigmoid(z):
        return 0.5 * jnp.tanh(0.5 * z) + 0.5

    def pad_rows(part, lo_r, dtype):
        pads = [jnp.zeros((lo_r, part.shape[1]), dtype), part,
                jnp.zeros((c_len - lo_r - part.shape[0], part.shape[1]), dtype)]
        return jnp.concatenate([p for p in pads if p.shape[0]], axis=0)

    def chunk_step(ci, carry):
        c = (n_chunks - 1 - ci) if reverse else ci
        r0 = pl.multiple_of(c * c_len, c_len)
        qr = q_ref[pl.ds(r0, c_len), :]
        q = qr * sigmoid(qr)
        f = lb_all + (1.0 - lb_all) * sigmoid(fl_ref[pl.ds(r0, c_len), :])
        k = 1.0 - f
        vb = v_ref[pl.ds(r0, c_len), :].astype(_BF16)
        lf = jnp.log2(f)
        hi = lf.astype(_BF16)
        rem = lf - hi.astype(_F32)
        mid = rem.astype(_BF16)
        lo = (rem - mid.astype(_F32)).astype(_BF16)
        cum = (jnp.dot(tri, hi, preferred_element_type=_F32) + jnp.dot(tri, mid, preferred_element_type=_F32)
               + jnp.dot(tri, lo, preferred_element_type=_F32))
        edge = cum[0:1] if reverse else cum[c_len - 1:c_len]
        k_end = (k * jnp.exp2(edge - cum)).astype(_BF16)
        st_decay = jnp.exp2(edge)
        refs = []
        for i in range(n_sub):
            if reverse:
                refs.append(cum[(i + 1) * _SUB:(i + 1) * _SUB + 1] if i < n_sub - 1 else jnp.zeros((1, w), _F32))
            else:
                refs.append(cum[i * _SUB - 1:i * _SUB] if i > 0 else jnp.zeros((1, w), _F32))
        ref_rows = jnp.concatenate([jnp.broadcast_to(r, (_SUB, w)) for r in refs], axis=0)
        a_f32 = q * jnp.exp2(cum - ref_rows)
        a = a_f32.astype(_BF16)
        ref_decay = jnp.concatenate([jnp.broadcast_to(jnp.exp2(r), (_SUB, w)) for r in refs], axis=0)
        q_in = (a_f32 * ref_decay).astype(_BF16)
        bs = []
        for i in range(n_sub):
            lo_r, hi_r = (i * _SUB, c_len) if reverse else (0, (i + 1) * _SUB)
            part = k[lo_r:hi_r] * jnp.exp2(jnp.minimum(refs[i] - cum[lo_r:hi_r], _EXP_CLAMP))
            bs.append(pad_rows(part.astype(_BF16), lo_r, _BF16))

        sts = [st_ref[h] for h in range(hb)]
        o_inter = [lax.dot_general(q_in[:, ln], sts[h].astype(_BF16), nt, preferred_element_type=_F32)
                   for h, ln in enumerate(heads)]
        a_cat = [jnp.concatenate([pad_rows(a[i * _SUB:(i + 1) * _SUB, ln], i * _SUB, _BF16) for i in range(n_sub)],
                                 axis=1) for ln in heads]
        b_cat = [jnp.concatenate([bs[i][:, ln] for i in range(n_sub)], axis=1) for ln in heads]
        scores = [lax.dot_general(a_cat[h], b_cat[h], nt, preferred_element_type=_F32) for h in range(hb)]
        scores = [jnp.where(keep, s, 0.0).astype(_BF16) for s in scores]
        o_intra = [jnp.dot(scores[h], vb[:, ln], preferred_element_type=_F32) for h, ln in enumerate(heads)]
        st_new = [lax.dot_general(vb[:, ln], k_end[:, ln], tn, preferred_element_type=_F32) for ln in heads]
        for h, ln in enumerate(heads):
            st_ref[h] = sts[h] * st_decay[:, ln] + st_new[h]
        o = jnp.concatenate([o_inter[h] + o_intra[h] for h in range(hb)], axis=1)
        if epilogue:
            tot = of_ref[pl.ds(r0, c_len), :] + o
            ys = []
            for ln in heads:
                th = tot[:, ln]
                ys.append(th * lax.rsqrt(jnp.mean(th * th, axis=-1, keepdims=True) + _EPS))
            g = gate_ref[pl.ds(r0, c_len), :]
            y = jnp.concatenate(ys, axis=1) * gain_ref[...]
            o_ref[pl.ds(r0, c_len), :] = (y * (g * sigmoid(g))).astype(o_ref.dtype)
        else:
            o_ref[pl.ds(r0, c_len), :] = o
        return carry

    lax.fori_loop(0, n_chunks, chunk_step, 0)


def _hgrn_scan(proj, lb_logits, seg_lens, *, reverse, slot, o_fwd=None, gain=None):
    m, d5 = proj.shape
    d = d5 // 5
    hb = math.gcd(_SCAN_HEADS, d // _HEAD_DIM)
    w = hb * _HEAD_DIM
    nhb = d // w
    epilogue = o_fwd is not None
    tb = _row_tile(math.gcd(*seg_lens) if len(seg_lens) > 1 else seg_lens[0],
                   _SCAN_ROWS if epilogue else 2 * _SCAN_ROWS)
    n_t = m // tb
    bounds = np.cumsum([0] + list(seg_lens)) // tb
    reset_blocks = tuple(int(b) - 1 for b in bounds[1:]) if reverse else tuple(int(b) for b in bounds[:-1])

    def tmap(t):
        return (n_t - 1 - t) if reverse else t

    def stream(s):
        return pl.BlockSpec((tb, w), lambda h, t, s=s: (tmap(t), s * nhb + h))

    n_slots = lb_logits.shape[0]
    in_specs = [stream(0), stream(2 if reverse else 1), stream(3),
                pl.BlockSpec((n_slots, w), lambda h, t: (0, h))]
    args = [proj, proj, proj, lb_logits.astype(_F32)]
    if epilogue:
        in_specs += [pl.BlockSpec((tb, w), lambda h, t: (tmap(t), h)), stream(4),
                     pl.BlockSpec((1, w), lambda h, t: (0, h))]
        args += [o_fwd, proj, gain.reshape(1, d).astype(_F32)]
    body = functools.partial(_scan_body, reverse=reverse, n_chunks=tb // _CHUNK, hb=hb,
                             reset_blocks=reset_blocks, n_tblocks=n_t, slot=slot, epilogue=epilogue)
    return pl.pallas_call(
        body,
        grid=(nhb, n_t),
        in_specs=in_specs,
        out_specs=pl.BlockSpec((tb, w), lambda h, t: (tmap(t), h)),
        out_shape=jax.ShapeDtypeStruct((m, d), _BF16 if epilogue else _F32),
        scratch_shapes=[pltpu.VMEM((hb, _HEAD_DIM, _HEAD_DIM), _F32)],
        compiler_params=_cparams(("parallel", "arbitrary")),
        name="hgrn_scan_bwd" if reverse else "hgrn_scan_fwd",
    )(*args)


def _inproj_conv3_body(prev_ref, cur_ref, nxt_ref, w_ref, cw_ref, cb_ref, o_ref, *, h1, tiles_per_seq):
    i_loc = lax.rem(pl.program_id(0), tiles_per_seq)
    wb = w_ref[...].astype(_BF16)
    u = jnp.dot(cur_ref[...], wb, preferred_element_type=_F32)
    u_prev = jnp.dot(prev_ref[...], wb, preferred_element_type=_F32)
    u_next = jnp.dot(nxt_ref[...], wb, preferred_element_type=_F32)
    rows = lax.broadcasted_iota(jnp.int32, u_prev.shape, 0)
    u_prev = jnp.where(i_loc == 0, jnp.where(rows == 0, 0.0, pltpu.roll(u_prev, 1, 0)), u_prev)
    u_next = jnp.where(i_loc == tiles_per_seq - 1,
                       jnp.where(rows == h1 - 1, 0.0, pltpu.roll(u_next, h1 - 1, 0)), u_next)
    if u.shape[0] > h1:
        up = jnp.concatenate([u_prev, u[:-h1]], axis=0)
        dn = jnp.concatenate([u[h1:], u_next], axis=0)
    else:
        up, dn = u_prev, u_next
    cw = cw_ref[...]
    o_ref[...] = up * cw[0:1] + u * cw[1:2] + dn * cw[2:3] + cb_ref[...]


def _inproj_conv3(hg, w, layer, conv_w, conv_b, batch, seq_len):
    m, k = hg.shape
    n = w.shape[2]
    n1, n2 = _fft_plan(seq_len)
    h1 = n1 // 2
    jb = _row_tile(n2, max(1, _MM_ROWS_RESIDENT // h1))
    tm = jb * h1
    tps = n2 // jb
    tn = _tile(n, _MM_COLS)

    def seq_slab(i, s):
        return lax.div(i, tps) * n2 + s

    def prev_slab(i, j):
        il = lax.rem(i, tps)
        return (seq_slab(i, jnp.where(il == 0, n2, il * jb) - 1), 0)

    def next_slab(i, j):
        il = lax.rem(i, tps)
        return (seq_slab(i, jnp.where(il == tps - 1, 0, (il + 1) * jb)), 0)

    body = functools.partial(_inproj_conv3_body, h1=h1, tiles_per_seq=tps)
    out = pl.pallas_call(
        body,
        grid=(m // tm, n // tn),
        in_specs=[pl.BlockSpec((h1, k), prev_slab),
                  pl.BlockSpec((tm, k), lambda i, j: (i, 0), pipeline_mode=pl.Buffered(1)),
                  pl.BlockSpec((h1, k), next_slab),
                  pl.BlockSpec((None, k, tn), lambda i, j: (layer, 0, j)),
                  pl.BlockSpec((3, tn), lambda i, j: (0, j)),
                  pl.BlockSpec((1, tn), lambda i, j: (0, j))],
        out_specs=pl.BlockSpec((tm, tn), lambda i, j: (i, j)),
        out_shape=jax.ShapeDtypeStruct((m, n), _F32),
        compiler_params=_cparams(("parallel", "arbitrary")),
        name="hyena_inproj_conv3",
    )(hg, hg, hg, w.astype(_F32), conv_w.astype(_F32), conv_b.reshape(1, n).astype(_F32))
    return out.reshape(batch, n2, h1, n)


def _permute_rows(x, groups, inverse=False):
    parts, row0 = [], 0
    for batch, seq_len in groups:
        n1, n2 = _fft_plan(seq_len)
        shape = (batch, n2, n1 // 2, -1) if inverse else (batch, n1 // 2, n2, -1)
        rows = batch * seq_len
        parts.append(x[row0:row0 + rows].reshape(shape).transpose(0, 2, 1, 3).reshape(rows, -1))
        row0 += rows
    return jnp.concatenate(parts, axis=0)


def _fft_plan(seq_len):
    n = 2 * seq_len
    n1 = 256 if n >= 32768 else 128
    while n // n1 < 8:
        n1 //= 2
    return n1, n // n1


def _kept_k1(n1):
    return n1 // 2 + 8


def _dft_tables(seq_len):
    n = 2 * seq_len
    n1, n2 = _fft_plan(seq_len)
    h1 = n1 // 2
    kp = _kept_k1(n1)
    i2 = lax.broadcasted_iota(jnp.int32, (n2, kp, n1), 0)
    k1 = lax.broadcasted_iota(jnp.int32, (n2, kp, n1), 1)
    i1 = lax.broadcasted_iota(jnp.int32, (n2, kp, n1), 2)
    phase = (i2 * k1 + n2 * (i1 * k1)) % n
    ang = phase.astype(_F32) * (2.0 * math.pi / n)
    cos = jnp.where(k1 <= h1, jnp.cos(ang), 0.0)
    sin = jnp.where(k1 <= h1, jnp.sin(ang), 0.0)
    fwd_a = jnp.concatenate([cos, -sin], axis=1).astype(_BF16)
    weight = jnp.where((k1 == 0) | (k1 == h1), 1.0, 2.0) * (1.0 / n)
    cos_t, sin_t = (cos * weight).transpose(0, 2, 1)[:, :h1], (sin * weight).transpose(0, 2, 1)[:, :h1]
    inv_a = jnp.concatenate([cos_t, -sin_t], axis=2).astype(_BF16)
    idx = np.arange(n2)
    ang_b = 2.0 * np.pi * ((idx[:, None] * idx[None, :]) % n2) / n2
    cb, sb = np.cos(ang_b), np.sin(ang_b)
    fwd_b = jnp.asarray(np.block([[cb, sb], [-sb, cb]]), _BF16)
    inv_b = jnp.asarray(np.block([[cb, -sb], [sb, cb]]), _BF16)
    fwd_a_half = fwd_a if h1 % 128 == 0 else fwd_a[:, :, :h1]
    return dict(fwd_a_full=fwd_a, fwd_a_half=fwd_a_half, inv_a=inv_a, fwd_b=fwd_b, inv_b=inv_b)


def _swap_stage_layout(x, groups, rows):
    b, _, _, d = x.shape
    return x.reshape(b, groups, 2, rows, d).transpose(0, 3, 2, 1, 4).reshape(b, rows, 2 * groups, d)


def _stage_a_fwd_body(x_ref, m_ref, o_ref):
    o_ref[...] = jnp.dot(m_ref[...], x_ref[...].astype(_BF16), preferred_element_type=_F32).astype(o_ref.dtype)


def _stage_a_fwd(xp, col_off, mats, d):
    b, n2, h1, _ = xp.shape
    r = mats.shape[1]
    td = _tile(d, _DFT_COLS)
    off = col_off // td
    return pl.pallas_call(
        _stage_a_fwd_body,
        grid=(b, n2, d // td),
        in_specs=[pl.BlockSpec((None, None, h1, td), lambda bb, j, c: (bb, j, 0, off + c)),
                  pl.BlockSpec((None, r, h1), lambda bb, j, c: (j, 0, 0))],
        out_specs=pl.BlockSpec((None, None, r, td), lambda bb, j, c: (bb, j, 0, c)),
        out_shape=jax.ShapeDtypeStruct((b, n2, r, d), _BF16),
        compiler_params=_cparams(("parallel", "parallel", "parallel")),
        name="hyena_stage_a_fwd",
    )(xp, mats)


def _stage_b_conv_body(g_ref, kg_ref, l1_ref, fb_ref, fbi_ref, o_ref, ks_ref):
    kb = g_ref.shape[0]
    n2 = g_ref.shape[1] // 2
    fb = fb_ref[...]

    @pl.when(pl.program_id(2) == 0)
    def _():
        inv_l1 = 1.0 / (l1_ref[...] + _EPS)
        for i in range(kb):
            ks_ref[i] = jnp.dot(fb, kg_ref[i], preferred_element_type=_F32) * inv_l1

    for i in range(kb):
        x = jnp.dot(fb, g_ref[i], preferred_element_type=_F32)
        xr, xi = x[:n2], x[n2:]
        kr, ki = ks_ref[i, :n2, :], ks_ref[i, n2:, :]
        y = jnp.concatenate([xr * kr - xi * ki, xr * ki + xi * kr], axis=0).astype(_BF16)
        o_ref[i] = jnp.dot(fbi_ref[...], y, preferred_element_type=_F32).astype(o_ref.dtype)


def _stage_b_conv(g, kg, l1, fwd_b, inv_b):
    b, n1, r, d = g.shape
    td = _tile(d, _DFT_COLS)
    kb = _row_tile(n1, _DFT_GROUPS_PER_STEP)
    return pl.pallas_call(
        _stage_b_conv_body,
        grid=(n1 // kb, d // td, b),
        in_specs=[pl.BlockSpec((None, kb, r, td), lambda k, c, bb: (bb, k, 0, c)),
                  pl.BlockSpec((kb, r, td), lambda k, c, bb: (k, 0, c)),
                  pl.BlockSpec((1, td), lambda k, c, bb: (0, c)),
                  pl.BlockSpec((r, r), lambda k, c, bb: (0, 0)),
                  pl.BlockSpec((r, r), lambda k, c, bb: (0, 0))],
        out_specs=pl.BlockSpec((None, kb, r, td), lambda k, c, bb: (bb, k, 0, c)),
        out_shape=jax.ShapeDtypeStruct((b, n1, r, d), _BF16),
        scratch_shapes=[pltpu.VMEM((kb, r, td), _F32)],
        compiler_params=_cparams(("parallel", "parallel", "arbitrary")),
        name="hyena_stage_b_conv",
    )(g, kg, l1, fwd_b, inv_b)


def _stage_a_inv_body(h_ref, m_ref, y_ref, g_ref, s_ref, *rest):
    for i in range(h_ref.shape[0]):
        conv = jnp.dot(m_ref[i], h_ref[i], preferred_element_type=_F32)
        y_new = g_ref[i] * (conv + y_ref[i] * s_ref[...])
        if len(rest) == 1:
            rest[0][i] = y_new.astype(rest[0].dtype)
        else:
            fwd_ref, o_ref, gt_ref = rest
            o_ref[i] = y_new
            gt_ref[i] = jnp.dot(fwd_ref[i], y_new.astype(_BF16), preferred_element_type=_F32).astype(gt_ref.dtype)


def _stage_a_inv(ht, mats, yp, y_off, gp, g_off, skip, out_dtype, next_mats=None):
    b, n2, r, d = ht.shape
    h1 = mats.shape[1]
    td = _tile(d, _DFT_COLS)
    yo, go = y_off // td, g_off // td
    jb = _row_tile(n2, _DFT_GROUPS_PER_STEP)
    in_specs = [pl.BlockSpec((None, jb, r, td), lambda bb, j, c: (bb, j, 0, c)),
                pl.BlockSpec((jb, h1, r), lambda bb, j, c: (j, 0, 0)),
                pl.BlockSpec((None, jb, h1, td), lambda bb, j, c: (bb, j, 0, yo + c)),
                pl.BlockSpec((None, jb, h1, td), lambda bb, j, c: (bb, j, 0, go + c)),
                pl.BlockSpec((1, td), lambda bb, j, c: (0, c))]
    args = [ht, mats, yp, gp, skip.reshape(1, d).astype(_F32)]
    out_specs = pl.BlockSpec((None, jb, h1, td), lambda bb, j, c: (bb, j, 0, c))
    out_shape = jax.ShapeDtypeStruct((b, n2, h1, d), out_dtype)
    if next_mats is not None:
        in_specs.append(pl.BlockSpec((jb, r, h1), lambda bb, j, c: (j, 0, 0)))
        args.append(next_mats)
        out_specs = [out_specs, pl.BlockSpec((None, jb, r, td), lambda bb, j, c: (bb, j, 0, c))]
        out_shape = [out_shape, jax.ShapeDtypeStruct((b, n2, r, d), _BF16)]
    return pl.pallas_call(
        _stage_a_inv_body,
        grid=(b, n2 // jb, d // td),
        in_specs=in_specs,
        out_specs=out_specs,
        out_shape=out_shape,
        compiler_params=_cparams(("parallel", "parallel", "parallel")),
        name="hyena_stage_a_inv",
    )(*args)


def _filter_pos(shape, j0, n1, n2, seq_len):
    r = lax.broadcasted_iota(jnp.int32, shape, 0)
    i1 = jnp.bitwise_and(r, n1 - 1)
    j = j0 + lax.shift_right_logical(r, n1.bit_length() - 1)
    n = n2 * i1 + j
    return jnp.where(i1 < n1 // 2, n, 2 * seq_len - n)


def _filter_mlp_body(band_ref, w1_ref, b1_ref, w2_ref, b2_ref, w3_ref, b3_ref, fr_ref, o_ref,
                     *, jb, n1, n2, seq_len):
    hi = lax.Precision.HIGHEST
    rows = jb * n1
    p = _filter_pos((rows, 128), pl.program_id(0) * jb, n1, n2, seq_len).astype(_F32)
    lane = lax.broadcasted_iota(jnp.int32, (rows, 128), 1)
    ang = (2.0 * math.pi / seq_len) * p * band_ref[...]
    feat = jnp.where(lane == 0, p * (1.0 / (seq_len - 1)),
                     jnp.where(lane <= _HY_BANDS, jnp.cos(ang),
                               jnp.where(lane <= 2 * _HY_BANDS, -jnp.sin(ang), 0.0)))
    fr = fr_ref[...]
    z = jnp.sin(fr * (jnp.dot(feat, w1_ref[...], preferred_element_type=_F32, precision=hi) + b1_ref[...]))
    z = jnp.sin(fr * (jnp.dot(z, w2_ref[...], preferred_element_type=_F32, precision=hi) + b2_ref[...]))
    z = jnp.sin(fr * (jnp.dot(z, w3_ref[...], preferred_element_type=_F32, precision=hi) + b3_ref[...]))
    o_ref[...] = z.reshape(jb, n1, _HY_WIDTH)


def _filter_mlp(seq_len, fc1_w, fc1_b, fc2_w, fc2_b, fc3_w, fc3_b, sin_freq):
    n1, n2 = _fft_plan(seq_len)
    jb = _row_tile(n2, 8)
    wd = _HY_WIDTH
    bands = jnp.linspace(1e-4, _HY_BANDS - 1, _HY_BANDS, dtype=_F32)
    band_row = jnp.zeros((1, 128), _F32).at[0, 1:1 + _HY_BANDS].set(bands).at[0, 1 + _HY_BANDS:_HY_EMB].set(bands)
    w1 = jnp.zeros((128, wd), _F32).at[:_HY_EMB].set(fc1_w.astype(_F32))
    const = lambda shape: pl.BlockSpec(shape, lambda i: (0,) * len(shape))
    row = lambda a: a.reshape(1, wd).astype(_F32)
    body = functools.partial(_filter_mlp_body, jb=jb, n1=n1, n2=n2, seq_len=seq_len)
    return pl.pallas_call(
        body,
        grid=(n2 // jb,),
        in_specs=[const((1, 128)), const((128, wd)), const((1, wd)), const((wd, wd)), const((1, wd)),
                  const((wd, wd)), const((1, wd)), const((1, wd))],
        out_specs=pl.BlockSpec((jb, n1, wd), lambda i: (i, 0, 0)),
        out_shape=jax.ShapeDtypeStruct((n2, n1, wd), _F32),
        compiler_params=_cparams(("parallel",)),
        name="hyena_filter_mlp",
    )(band_row, w1, row(fc1_b), fc2_w.astype(_F32), row(fc2_b), fc3_w.astype(_F32), row(fc3_b), row(sin_freq))


def _kernel_stage_a_body(z_ref, wf_ref, wb_ref, dl_ref, m_ref, o_ref, l1_ref, *, n1, n2, seq_len):
    j = pl.program_id(1)
    h1 = n1 // 2
    td = wf_ref.shape[1]
    p = _filter_pos((n1, td), j, n1, n2, seq_len)
    window = jnp.exp(-(p.astype(_F32) * (1.0 / (seq_len - 1))) * dl_ref[...])
    z = z_ref[...]
    zr = lax.broadcasted_iota(jnp.int32, (n1, 2 * _HY_WIDTH), 0)
    zc = lax.broadcasted_iota(jnp.int32, (n1, 2 * _HY_WIDTH), 1)
    zz = jnp.where((zr < h1) == (zc < _HY_WIDTH), jnp.concatenate([z, z], axis=1), 0.0)
    wcat = jnp.concatenate([wf_ref[...], wb_ref[...]], axis=0)
    kern = jnp.dot(zz.astype(_BF16), wcat.astype(_BF16), preferred_element_type=_F32) * window
    kern = jnp.where(p == seq_len, 0.0, kern)

    @pl.when(j == 0)
    def _():
        l1_ref[...] = jnp.zeros_like(l1_ref)

    l1_ref[...] += jnp.sum(jnp.abs(kern), axis=0, keepdims=True)
    o_ref[...] = jnp.dot(m_ref[...], kern.astype(_BF16), preferred_element_type=_F32).astype(o_ref.dtype)


def _kernel_stage_a(z, fc4_w, order, deltas, mats, seq_len, d):
    n1, n2 = _fft_plan(seq_len)
    r = mats.shape[1]
    td = _tile(d, _KERNEL_COLS)
    nd = d // td
    fwd_off, bwd_off = order * 2 * nd, order * 2 * nd + nd
    body = functools.partial(_kernel_stage_a_body, n1=n1, n2=n2, seq_len=seq_len)
    return pl.pallas_call(
        body,
        grid=(nd, n2),
        in_specs=[pl.BlockSpec((None, n1, _HY_WIDTH), lambda c, j: (j, 0, 0)),
                  pl.BlockSpec((_HY_WIDTH, td), lambda c, j: (0, fwd_off + c)),
                  pl.BlockSpec((_HY_WIDTH, td), lambda c, j: (0, bwd_off + c)),
                  pl.BlockSpec((1, td), lambda c, j: (0, c)),
                  pl.BlockSpec((None, r, n1), lambda c, j: (j, 0, 0))],
        out_specs=[pl.BlockSpec((None, r, td), lambda c, j: (j, 0, c)),
                   pl.BlockSpec((1, td), lambda c, j: (0, c))],
        out_shape=[jax.ShapeDtypeStruct((n2, r, d), _BF16), jax.ShapeDtypeStruct((1, d), _F32)],
        compiler_params=_cparams(("parallel", "arbitrary")),
        name="hyena_kernel_stage_a",
    )(z, fc4_w.astype(_F32), fc4_w.astype(_F32), deltas, mats)


def _hyena_long_conv(up, seq_len, d, hp):
    n1, n2 = _fft_plan(seq_len)
    kp = _kept_k1(n1)
    tabs = _dft_tables(seq_len)
    deltas = jnp.abs(jnp.linspace(_HY_MIN_DECAY, _HY_MAX_DECAY, d, dtype=_F32)).reshape(1, d)
    z = _filter_mlp(seq_len, hp["fc1_w"], hp["fc1_b"], hp["fc2_w"], hp["fc2_b"], hp["fc3_w"], hp["fc3_b"],
                    hp["sin_freq"])
    y, y_off = up, 0
    g = _stage_a_fwd(y, y_off, tabs["fwd_a_half"], d)
    for order in range(_HY_ORDER):
        kg, l1 = _kernel_stage_a(z, hp["fc4_w"], order, deltas, tabs["fwd_a_full"], seq_len, d)
        h = _stage_b_conv(_swap_stage_layout(g, n2, kp), _swap_stage_layout(kg[None], n2, kp)[0], l1,
                          tabs["fwd_b"], tabs["inv_b"])
        ht = _swap_stage_layout(h, kp, n2)
        if order == _HY_ORDER - 1:
            y = _stage_a_inv(ht, tabs["inv_a"], y, y_off, up, (order + 1) * d, hp["skip"][order], _BF16)
        else:
            y, g = _stage_a_inv(ht, tabs["inv_a"], y, y_off, up, (order + 1) * d, hp["skip"][order], _F32,
                                next_mats=tabs["fwd_a_half"])
        y_off = 0
    return y.reshape(-1, d)


def kernel(x_prompt, x_sample, norm_mix, norm_ffn, norm_final, hg_w_in, hg_lb_logits, hg_out_norm, hg_w_out,
           hy_w_in, hy_conv_w, hy_conv_b, hy_fc1_w, hy_fc1_b, hy_fc2_w, hy_fc2_b, hy_fc3_w, hy_fc3_b, hy_fc4_w,
           hy_sin_freq, hy_skip, hy_w_out, ffn_w_gate, ffn_w_up, ffn_w_down):
    b1, t1, d = x_prompt.shape
    b2, t2, _ = x_sample.shape
    groups = ((b1, t1), (b2, t2))
    group_x = (x_prompt.reshape(b1 * t1, d), x_sample.reshape(b2 * t2, d))
    m_total = b1 * t1 + b2 * t2
    depth = norm_mix.shape[0]

    x = None
    for layer in range(depth):
        slot = layer // 2
        if layer % 2 == 1:
            hp = dict(fc1_w=hy_fc1_w[slot], fc1_b=hy_fc1_b[slot], fc2_w=hy_fc2_w[slot], fc2_b=hy_fc2_b[slot],
                      fc3_w=hy_fc3_w[slot], fc3_b=hy_fc3_b[slot], fc4_w=hy_fc4_w[slot],
                      sin_freq=hy_sin_freq[slot], skip=hy_skip[slot])
        row_starts = [0, groups[0][0] * groups[0][1]]
        hs = [(_rmsnorm(xg, norm_mix[layer], _BF16) if x is None
               else _rmsnorm(x, norm_mix[layer], _BF16, r0, bg * tg))
              for (bg, tg), xg, r0 in zip(groups, group_x, row_starts)]
        ys = []
        for (bg, tg), h in zip(groups, hs):
            if layer % 2 == 0:
                proj = _matmul(h, (hg_w_in,), slot, mode="plain", rows_resident=True)
                o_f = _hgrn_scan(proj, hg_lb_logits[0], (tg,) * bg, reverse=False, slot=slot)
                ys.append(_hgrn_scan(proj, hg_lb_logits[1], (tg,) * bg, reverse=True, slot=slot, o_fwd=o_f,
                                     gain=hg_out_norm[slot]))
            else:
                up = _inproj_conv3(_permute_rows(h, ((bg, tg),)), hy_w_in, slot, hy_conv_w[slot], hy_conv_b[slot],
                                   bg, tg)
                ys.append(_permute_rows(_hyena_long_conv(up, tg, d, hp), ((bg, tg),), inverse=True))
        w_out = hg_w_out if layer % 2 == 0 else hy_w_out
        for y, xg, r0 in zip(ys, group_x, row_starts):
            if layer == 0:
                x = _matmul(y, (w_out,), slot, mode="residual", residual=xg, out_into=(m_total, r0, x))
            else:
                x = _matmul(y, (w_out,), slot, mode="residual", residual=x, res_row0=r0)
        h2 = _rmsnorm(x, norm_ffn[layer], _BF16)
        a = _matmul(h2, (ffn_w_gate, ffn_w_up), layer, mode="swiglu", out_dtype=_BF16, rows_resident=True)
        x = _matmul(a, (ffn_w_down,), layer, mode="residual", residual=x, tm=_MM_ROWS_WIDE_K, single_buffer_w=True)

    return (_rmsnorm(x, norm_final, _F32, 0, b1 * t1).reshape(b1, t1, d),
            _rmsnorm(x, norm_final, _F32, b1 * t1, b2 * t2).reshape(b2, t2, d))
```

```python
import functools
import math

import numpy as np
import jax
import jax.numpy as jnp
from jax import lax
from jax.experimental import pallas as pl
from jax.experimental.pallas import tpu as pltpu

_F32 = jnp.float32
_BF16 = jnp.bfloat16

_EPS = 1e-6
_HEAD_DIM = 128
_CHUNK = 64
_SUB = 16
_EXP_CLAMP = 115.0
_HY_ORDER = 2
_HY_EMB = 33
_HY_BANDS = (_HY_EMB - 1) // 2
_HY_WIDTH = 64
_HY_MAX_DECAY = math.log(1e-2) / 0.3
_HY_MIN_DECAY = math.log(1e-2) / 1.5

_V7X_VMEM_LIMIT = 56 * 1024 * 1024
_NORM_ROWS = 512
_MM_ROWS = 1024
_MM_ROWS_RESIDENT = 2048
_MM_COLS = 512
_MM_ROWS_WIDE_K = 384
_SCAN_HEADS = 32
_SCAN_ROWS = 128
_DFT_COLS = 4096
_DFT_GROUPS_PER_STEP = 2
_KERNEL_COLS = 2048


def _cparams(sem):
    return pltpu.CompilerParams(dimension_semantics=sem, vmem_limit_bytes=_V7X_VMEM_LIMIT)


def _tile(dim, want):
    if dim <= want:
        return dim
    t = want
    while t >= 128:
        if dim % t == 0:
            return t
        t -= 128
    return dim


def _row_tile(rows, want):
    if rows <= want:
        return rows
    for step in (8, 1):
        t = want - want % step
        while t >= step:
            if rows % t == 0:
                return t
            t -= step
    return rows


def _rmsnorm_body(x_ref, g_ref, o_ref):
    x = x_ref[...]
    ms = jnp.mean(x * x, axis=-1, keepdims=True)
    o_ref[...] = (x * lax.rsqrt(ms + _EPS) * g_ref[...]).astype(o_ref.dtype)


def _rmsnorm(x, gain, out_dtype, row0=0, rows=None):
    d = x.shape[1]
    m = x.shape[0] if rows is None else rows
    tm = _row_tile(math.gcd(m, row0) if row0 else m, _NORM_ROWS)
    blk0 = row0 // tm
    return pl.pallas_call(
        _rmsnorm_body,
        grid=(m // tm,),
        in_specs=[pl.BlockSpec((tm, d), lambda i: (blk0 + i, 0)),
                  pl.BlockSpec((1, d), lambda i: (0, 0))],
        out_specs=pl.BlockSpec((tm, d), lambda i: (i, 0)),
        out_shape=jax.ShapeDtypeStruct((m, d), out_dtype),
        compiler_params=_cparams(("parallel",)),
        name="rmsnorm",
    )(x, gain.reshape(1, d).astype(_F32))


def _mm_body(*refs, mode, n_w, rows_resident, n_in):
    a_ref, w_refs, o_ref = refs[0], refs[1:1 + n_w], refs[n_in]
    if rows_resident:
        ws = [w_ref[...].astype(_BF16) for w_ref in w_refs]
    else:
        wb_refs = refs[len(refs) - n_w:]

        @pl.when(pl.program_id(1) == 0)
        def _():
            for w_ref, wb_ref in zip(w_refs, wb_refs):
                wb_ref[...] = w_ref[...].astype(_BF16)

        ws = [wb_ref[...] for wb_ref in wb_refs]

    a = a_ref[...]
    if mode == "swiglu":
        g = jnp.dot(a, ws[0], preferred_element_type=_F32)
        u = jnp.dot(a, ws[1], preferred_element_type=_F32)
        o_ref[...] = (g * (0.5 * jnp.tanh(0.5 * g) + 0.5) * u).astype(o_ref.dtype)
    elif mode == "residual":
        o_ref[...] = refs[1 + n_w][...] + jnp.dot(a, ws[0], preferred_element_type=_F32)
    else:
        o_ref[...] = jnp.dot(a, ws[0], preferred_element_type=_F32).astype(o_ref.dtype)


def _matmul(a, ws, layer, *, mode, residual=None, res_row0=0, out_into=None, out_dtype=_F32, tm=None,
            rows_resident=False, single_buffer_w=False):
    m, k = a.shape
    n = ws[0].shape[2]
    row_offsets = [r for r in (res_row0, out_into[1] if out_into else 0) if r]
    tm = _row_tile(math.gcd(m, *row_offsets), tm or (_MM_ROWS_RESIDENT if rows_resident else _MM_ROWS))
    tn = _tile(n, _MM_COLS // len(ws))
    if rows_resident:
        grid = (m // tm, n // tn)
        ij = lambda g0, g1: (g0, g1)
        a_mode, w_mode = dict(pipeline_mode=pl.Buffered(1)), {}
        scratch = []
    else:
        grid = (n // tn, m // tm)
        ij = lambda g0, g1: (g1, g0)
        a_mode, w_mode = {}, (dict(pipeline_mode=pl.Buffered(1)) if single_buffer_w else {})
        scratch = [pltpu.VMEM((k, tn), _BF16) for _ in ws]
    in_specs = [pl.BlockSpec((tm, k), lambda g0, g1: (ij(g0, g1)[0], 0), **a_mode)]
    in_specs += [pl.BlockSpec((None, k, tn), lambda g0, g1: (layer, 0, ij(g0, g1)[1]), **w_mode) for _ in ws]
    args = [a, *[w.astype(_F32) for w in ws]]
    out_spec = pl.BlockSpec((tm, tn), lambda g0, g1: ij(g0, g1))
    out_shape = jax.ShapeDtypeStruct((m, n), out_dtype)
    aliases = {}
    if mode == "residual":
        def offset_spec(row0):
            blk0 = row0 // tm
            return pl.BlockSpec((tm, tn), lambda g0, g1: (blk0 + ij(g0, g1)[0], ij(g0, g1)[1]))

        in_specs.append(offset_spec(res_row0))
        args.append(residual)
        if out_into is None:
            out_spec = offset_spec(res_row0)
            out_shape = jax.ShapeDtypeStruct(residual.shape, residual.dtype)
            aliases = {len(args) - 1: 0}
        else:
            total_rows, row0, buffer = out_into
            out_spec = offset_spec(row0)
            out_shape = jax.ShapeDtypeStruct((total_rows, n), out_dtype)
            if buffer is not None:
                in_specs.append(pl.BlockSpec(memory_space=pl.ANY))
                args.append(buffer)
                aliases = {len(args) - 1: 0}
    return pl.pallas_call(
        functools.partial(_mm_body, mode=mode, n_w=len(ws), rows_resident=rows_resident,
                          n_in=len(args)),
        grid=grid,
        in_specs=in_specs,
        out_specs=out_spec,
        out_shape=out_shape,
        scratch_shapes=scratch,
        input_output_aliases=aliases,
        compiler_params=_cparams(("parallel", "arbitrary")),
        name="matmul_" + mode,
    )(*args)


def _scan_body(*refs, reverse, n_chunks, hb, reset_blocks, n_tblocks, slot, epilogue):
    if epilogue:
        q_ref, fl_ref, v_ref, lbl_ref, of_ref, gate_ref, gain_ref, o_ref, st_ref = refs
    else:
        q_ref, fl_ref, v_ref, lbl_ref, o_ref, st_ref = refs
    c_len, hd = _CHUNK, _HEAD_DIM

    t = pl.program_id(1)
    blk = (n_tblocks - 1 - t) if reverse else t
    reset = blk == reset_blocks[0]
    for r in reset_blocks[1:]:
        reset = jnp.logical_or(reset, blk == r)

    @pl.when(reset)
    def _():
        st_ref[...] = jnp.zeros_like(st_ref)

    lbl = lbl_ref[...]
    e = jnp.exp(lbl - jnp.max(lbl, axis=0, keepdims=True))
    lb_all = jnp.sum(e[: slot + 1], axis=0, keepdims=True) / jnp.sum(e, axis=0, keepdims=True)

    row = lax.broadcasted_iota(jnp.int32, (c_len, c_len), 0)
    col = lax.broadcasted_iota(jnp.int32, (c_len, c_len), 1)
    keep = (row <= col) if reverse else (row >= col)
    tri = keep.astype(_BF16)
    n_sub = c_len // _SUB
    w = hb * hd
    nt = (((1,), (1,)), ((), ()))
    tn = (((0,), (0,)), ((), ()))
    heads = [slice(h * hd, (h + 1) * hd) for h in range(hb)]

    def sigmoid(z):
        return 0.5 * jnp.tanh(0.5 * z) + 0.5

    def pad_rows(part, lo_r, dtype):
        pads = [jnp.zeros((lo_r, part.shape[1]), dtype), part,
                jnp.zeros((c_len - lo_r - part.shape[0], part.shape[1]), dtype)]
        return jnp.concatenate([p for p in pads if p.shape[0]], axis=0)

    def chunk_step(ci, carry):
        c = (n_chunks - 1 - ci) if reverse else ci
        r0 = pl.multiple_of(c * c_len, c_len)
        qr = q_ref[pl.ds(r0, c_len), :]
        q = qr * sigmoid(qr)
        f = lb_all + (1.0 - lb_all) * sigmoid(fl_ref[pl.ds(r0, c_len), :])
        k = 1.0 - f
        vb = v_ref[pl.ds(r0, c_len), :].astype(_BF16)
        lf = jnp.log2(f)
        hi = lf.astype(_BF16)
        rem = lf - hi.astype(_F32)
        mid = rem.astype(_BF16)
        lo = (rem - mid.astype(_F32)).astype(_BF16)
        cum = (jnp.dot(tri, hi, preferred_element_type=_F32) + jnp.dot(tri, mid, preferred_element_type=_F32)
               + jnp.dot(tri, lo, preferred_element_type=_F32))
        edge = cum[0:1] if reverse else cum[c_len - 1:c_len]
        k_end = (k * jnp.exp2(edge - cum)).astype(_BF16)
        st_decay = jnp.exp2(edge)
        refs = []
        for i in range(n_sub):
            if reverse:
                refs.append(cum[(i + 1) * _SUB:(i + 1) * _SUB + 1] if i < n_sub - 1 else jnp.zeros((1, w), _F32))
            else:
                refs.append(cum[i * _SUB - 1:i * _SUB] if i > 0 else jnp.zeros((1, w), _F32))
        ref_rows = jnp.concatenate([jnp.broadcast_to(r, (_SUB, w)) for r in refs], axis=0)
        a_f32 = q * jnp.exp2(cum - ref_rows)
        a = a_f32.astype(_BF16)
        ref_decay = jnp.concatenate([jnp.broadcast_to(jnp.exp2(r), (_SUB, w)) for r in refs], axis=0)
        q_in = (a_f32 * ref_decay).astype(_BF16)
        bs = []
        for i in range(n_sub):
            lo_r, hi_r = (i * _SUB, c_len) if reverse else (0, (i + 1) * _SUB)
            part = k[lo_r:hi_r] * jnp.exp2(jnp.minimum(refs[i] - cum[lo_r:hi_r], _EXP_CLAMP))
            bs.append(pad_rows(part.astype(_BF16), lo_r, _BF16))

        sts = [st_ref[h] for h in range(hb)]
        o_inter = [lax.dot_general(q_in[:, ln], sts[h].astype(_BF16), nt, preferred_element_type=_F32)
                   for h, ln in enumerate(heads)]
        a_cat = [jnp.concatenate([pad_rows(a[i * _SUB:(i + 1) * _SUB, ln], i * _SUB, _BF16) for i in range(n_sub)],
                                 axis=1) for ln in heads]
        b_cat = [jnp.concatenate([bs[i][:, ln] for i in range(n_sub)], axis=1) for ln in heads]
        scores = [lax.dot_general(a_cat[h], b_cat[h], nt, preferred_element_type=_F32) for h in range(hb)]
        scores = [jnp.where(keep, s, 0.0).astype(_BF16) for s in scores]
        o_intra = [jnp.dot(scores[h], vb[:, ln], preferred_element_type=_F32) for h, ln in enumerate(heads)]
        st_new = [lax.dot_general(vb[:, ln], k_end[:, ln], tn, preferred_element_type=_F32) for ln in heads]
        for h, ln in enumerate(heads):
            st_ref[h] = sts[h] * st_decay[:, ln] + st_new[h]
        o = jnp.concatenate([o_inter[h] + o_intra[h] for h in range(hb)], axis=1)
        if epilogue:
            tot = of_ref[pl.ds(r0, c_len), :] + o
            ys = []
            for ln in heads:
                th = tot[:, ln]
                ys.append(th * lax.rsqrt(jnp.mean(th * th, axis=-1, keepdims=True) + _EPS))
            g = gate_ref[pl.ds(r0, c_len), :]
            y = jnp.concatenate(ys, axis=1) * gain_ref[...]
            o_ref[pl.ds(r0, c_len), :] = (y * (g * sigmoid(g))).astype(o_ref.dtype)
        else:
            o_ref[pl.ds(r0, c_len), :] = o
        return carry

    lax.fori_loop(0, n_chunks, chunk_step, 0)


def _hgrn_scan(proj, lb_logits, seg_lens, *, reverse, slot, o_fwd=None, gain=None):
    m, d5 = proj.shape
    d = d5 // 5
    hb = math.gcd(_SCAN_HEADS, d // _HEAD_DIM)
    w = hb * _HEAD_DIM
    nhb = d // w
    epilogue = o_fwd is not None
    tb = _row_tile(math.gcd(*seg_lens) if len(seg_lens) > 1 else seg_lens[0],
                   _SCAN_ROWS if epilogue else 2 * _SCAN_ROWS)
    n_t = m // tb
    bounds = np.cumsum([0] + list(seg_lens)) // tb
    reset_blocks = tuple(int(b) - 1 for b in bounds[1:]) if reverse else tuple(int(b) for b in bounds[:-1])

    def tmap(t):
        return (n_t - 1 - t) if reverse else t

    def stream(s):
        return pl.BlockSpec((tb, w), lambda h, t, s=s: (tmap(t), s * nhb + h))

    n_slots = lb_logits.shape[0]
    in_specs = [stream(0), stream(2 if reverse else 1), stream(3),
                pl.BlockSpec((n_slots, w), lambda h, t: (0, h))]
    args = [proj, proj, proj, lb_logits.astype(_F32)]
    if epilogue:
        in_specs += [pl.BlockSpec((tb, w), lambda h, t: (tmap(t), h)), stream(4),
                     pl.BlockSpec((1, w), lambda h, t: (0, h))]
        args += [o_fwd, proj, gain.reshape(1, d).astype(_F32)]
    body = functools.partial(_scan_body, reverse=reverse, n_chunks=tb // _CHUNK, hb=hb,
                             reset_blocks=reset_blocks, n_tblocks=n_t, slot=slot, epilogue=epilogue)
    return pl.pallas_call(
        body,
        grid=(nhb, n_t),
        in_specs=in_specs,
        out_specs=pl.BlockSpec((tb, w), lambda h, t: (tmap(t), h)),
        out_shape=jax.ShapeDtypeStruct((m, d), _BF16 if epilogue else _F32),
        scratch_shapes=[pltpu.VMEM((hb, _HEAD_DIM, _HEAD_DIM), _F32)],
        compiler_params=_cparams(("parallel", "arbitrary")),
        name="hgrn_scan_bwd" if reverse else "hgrn_scan_fwd",
    )(*args)


def _inproj_conv3_body(prev_ref, cur_ref, nxt_ref, w_ref, cw_ref, cb_ref, o_ref, *, h1, tiles_per_seq):
    i_loc = lax.rem(pl.program_id(0), tiles_per_seq)
    wb = w_ref[...].astype(_BF16)
    u = jnp.dot(cur_ref[...], wb, preferred_element_type=_F32)
    u_prev = jnp.dot(prev_ref[...], wb, preferred_element_type=_F32)
    u_next = jnp.dot(nxt_ref[...], wb, preferred_element_type=_F32)
    rows = lax.broadcasted_iota(jnp.int32, u_prev.shape, 0)
    u_prev = jnp.where(i_loc == 0, jnp.where(rows == 0, 0.0, pltpu.roll(u_prev, 1, 0)), u_prev)
    u_next = jnp.where(i_loc == tiles_per_seq - 1,
                       jnp.where(rows == h1 - 1, 0.0, pltpu.roll(u_next, h1 - 1, 0)), u_next)
    if u.shape[0] > h1:
        up = jnp.concatenate([u_prev, u[:-h1]], axis=0)
        dn = jnp.concatenate([u[h1:], u_next], axis=0)
    else:
        up, dn = u_prev, u_next
    cw = cw_ref[...]
    o_ref[...] = up * cw[0:1] + u * cw[1:2] + dn * cw[2:3] + cb_ref[...]


def _inproj_conv3(hg, w, layer, conv_w, conv_b, batch, seq_len):
    m, k = hg.shape
    n = w.shape[2]
    n1, n2 = _fft_plan(seq_len)
    h1 = n1 // 2
    jb = _row_tile(n2, max(1, _MM_ROWS_RESIDENT // h1))
    tm = jb * h1
    tps = n2 // jb
    tn = _tile(n, _MM_COLS)

    def seq_slab(i, s):
        return lax.div(i, tps) * n2 + s

    def prev_slab(i, j):
        il = lax.rem(i, tps)
        return (seq_slab(i, jnp.where(il == 0, n2, il * jb) - 1), 0)

    def next_slab(i, j):
        il = lax.rem(i, tps)
        return (seq_slab(i, jnp.where(il == tps - 1, 0, (il + 1) * jb)), 0)

    body = functools.partial(_inproj_conv3_body, h1=h1, tiles_per_seq=tps)
    out = pl.pallas_call(
        body,
        grid=(m // tm, n // tn),
        in_specs=[pl.BlockSpec((h1, k), prev_slab),
                  pl.BlockSpec((tm, k), lambda i, j: (i, 0), pipeline_mode=pl.Buffered(1)),
                  pl.BlockSpec((h1, k), next_slab),
                  pl.BlockSpec((None, k, tn), lambda i, j: (layer, 0, j)),
                  pl.BlockSpec((3, tn), lambda i, j: (0, j)),
                  pl.BlockSpec((1, tn), lambda i, j: (0, j))],
        out_specs=pl.BlockSpec((tm, tn), lambda i, j: (i, j)),
        out_shape=jax.ShapeDtypeStruct((m, n), _F32),
        compiler_params=_cparams(("parallel", "arbitrary")),
        name="hyena_inproj_conv3",
    )(hg, hg, hg, w.astype(_F32), conv_w.astype(_F32), conv_b.reshape(1, n).astype(_F32))
    return out.reshape(batch, n2, h1, n)


def _permute_rows(x, groups, inverse=False):
    parts, row0 = [], 0
    for batch, seq_len in groups:
        n1, n2 = _fft_plan(seq_len)
        shape = (batch, n2, n1 // 2, -1) if inverse else (batch, n1 // 2, n2, -1)
        rows = batch * seq_len
        parts.append(x[row0:row0 + rows].reshape(shape).transpose(0, 2, 1, 3).reshape(rows, -1))
        row0 += rows
    return jnp.concatenate(parts, axis=0)


def _fft_plan(seq_len):
    n = 2 * seq_len
    n1 = 256 if n >= 32768 else 128
    while n // n1 < 8:
        n1 //= 2
    return n1, n // n1


def _kept_k1(n1):
    return n1 // 2 + 8


def _dft_tables(seq_len):
    n = 2 * seq_len
    n1, n2 = _fft_plan(seq_len)
    h1 = n1 // 2
    kp = _kept_k1(n1)
    i2 = lax.broadcasted_iota(jnp.int32, (n2, kp, n1), 0)
    k1 = lax.broadcasted_iota(jnp.int32, (n2, kp, n1), 1)
    i1 = lax.broadcasted_iota(jnp.int32, (n2, kp, n1), 2)
    phase = (i2 * k1 + n2 * (i1 * k1)) % n
    ang = phase.astype(_F32) * (2.0 * math.pi / n)
    cos = jnp.where(k1 <= h1, jnp.cos(ang), 0.0)
    sin = jnp.where(k1 <= h1, jnp.sin(ang), 0.0)
    fwd_a = jnp.concatenate([cos, -sin], axis=1).astype(_BF16)
    weight = jnp.where((k1 == 0) | (k1 == h1), 1.0, 2.0) * (1.0 / n)
    cos_t, sin_t = (cos * weight).transpose(0, 2, 1)[:, :h1], (sin * weight).transpose(0, 2, 1)[:, :h1]
    inv_a = jnp.concatenate([cos_t, -sin_t], axis=2).astype(_BF16)
    idx = np.arange(n2)
    ang_b = 2.0 * np.pi * ((idx[:, None] * idx[None, :]) % n2) / n2
    cb, sb = np.cos(ang_b), np.sin(ang_b)
    fwd_b = jnp.asarray(np.block([[cb, sb], [-sb, cb]]), _BF16)
    inv_b = jnp.asarray(np.block([[cb, -sb], [sb, cb]]), _BF16)
    fwd_a_half = fwd_a if h1 % 128 == 0 else fwd_a[:, :, :h1]
    return dict(fwd_a_full=fwd_a, fwd_a_half=fwd_a_half, inv_a=inv_a, fwd_b=fwd_b, inv_b=inv_b)


def _swap_stage_layout(x, groups, rows):
    b, _, _, d = x.shape
    return x.reshape(b, groups, 2, rows, d).transpose(0, 3, 2, 1, 4).reshape(b, rows, 2 * groups, d)


def _stage_a_fwd_body(x_ref, m_ref, o_ref):
    o_ref[...] = jnp.dot(m_ref[...], x_ref[...].astype(_BF16), preferred_element_type=_F32).astype(o_ref.dtype)


def _stage_a_fwd(xp, col_off, mats, d):
    b, n2, h1, _ = xp.shape
    r = mats.shape[1]
    td = _tile(d, _DFT_COLS)
    off = col_off // td
    return pl.pallas_call(
        _stage_a_fwd_body,
        grid=(b, n2, d // td),
        in_specs=[pl.BlockSpec((None, None, h1, td), lambda bb, j, c: (bb, j, 0, off + c)),
                  pl.BlockSpec((None, r, h1), lambda bb, j, c: (j, 0, 0))],
        out_specs=pl.BlockSpec((None, None, r, td), lambda bb, j, c: (bb, j, 0, c)),
        out_shape=jax.ShapeDtypeStruct((b, n2, r, d), _BF16),
        compiler_params=_cparams(("parallel", "parallel", "parallel")),
        name="hyena_stage_a_fwd",
    )(xp, mats)


def _stage_b_conv_body(g_ref, kg_ref, l1_ref, fb_ref, fbi_ref, o_ref, ks_ref):
    kb = g_ref.shape[0]
    n2 = g_ref.shape[1] // 2
    fb = fb_ref[...]

    @pl.when(pl.program_id(2) == 0)
    def _():
        inv_l1 = 1.0 / (l1_ref[...] + _EPS)
        for i in range(kb):
            ks_ref[i] = jnp.dot(fb, kg_ref[i], preferred_element_type=_F32) * inv_l1

    for i in range(kb):
        x = jnp.dot(fb, g_ref[i], preferred_element_type=_F32)
        xr, xi = x[:n2], x[n2:]
        kr, ki = ks_ref[i, :n2, :], ks_ref[i, n2:, :]
        y = jnp.concatenate([xr * kr - xi * ki, xr * ki + xi * kr], axis=0).astype(_BF16)
        o_ref[i] = jnp.dot(fbi_ref[...], y, preferred_element_type=_F32).astype(o_ref.dtype)


def _stage_b_conv(g, kg, l1, fwd_b, inv_b):
    b, n1, r, d = g.shape
    td = _tile(d, _DFT_COLS)
    kb = _row_tile(n1, _DFT_GROUPS_PER_STEP)
    return pl.pallas_call(
        _stage_b_conv_body,
        grid=(n1 // kb, d // td, b),
        in_specs=[pl.BlockSpec((None, kb, r, td), lambda k, c, bb: (bb, k, 0, c)),
                  pl.BlockSpec((kb, r, td), lambda k, c, bb: (k, 0, c)),
                  pl.BlockSpec((1, td), lambda k, c, bb: (0, c)),
                  pl.BlockSpec((r, r), lambda k, c, bb: (0, 0)),
                  pl.BlockSpec((r, r), lambda k, c, bb: (0, 0))],
        out_specs=pl.BlockSpec((None, kb, r, td), lambda k, c, bb: (bb, k, 0, c)),
        out_shape=jax.ShapeDtypeStruct((b, n1, r, d), _BF16),
        scratch_shapes=[pltpu.VMEM((kb, r, td), _F32)],
        compiler_params=_cparams(("parallel", "parallel", "arbitrary")),
        name="hyena_stage_b_conv",
    )(g, kg, l1, fwd_b, inv_b)


def _stage_a_inv_body(h_ref, m_ref, y_ref, g_ref, s_ref, *rest):
    for i in range(h_ref.shape[0]):
        conv = jnp.dot(m_ref[i], h_ref[i], preferred_element_type=_F32)
        y_new = g_ref[i] * (conv + y_ref[i] * s_ref[...])
        if len(rest) == 1:
            rest[0][i] = y_new.astype(rest[0].dtype)
        else:
            fwd_ref, o_ref, gt_ref = rest
            o_ref[i] = y_new
            gt_ref[i] = jnp.dot(fwd_ref[i], y_new.astype(_BF16), preferred_element_type=_F32).astype(gt_ref.dtype)


def _stage_a_inv(ht, mats, yp, y_off, gp, g_off, skip, out_dtype, next_mats=None):
    b, n2, r, d = ht.shape
    h1 = mats.shape[1]
    td = _tile(d, _DFT_COLS)
    yo, go = y_off // td, g_off // td
    jb = _row_tile(n2, _DFT_GROUPS_PER_STEP)
    in_specs = [pl.BlockSpec((None, jb, r, td), lambda bb, j, c: (bb, j, 0, c)),
                pl.BlockSpec((jb, h1, r), lambda bb, j, c: (j, 0, 0)),
                pl.BlockSpec((None, jb, h1, td), lambda bb, j, c: (bb, j, 0, yo + c)),
                pl.BlockSpec((None, jb, h1, td), lambda bb, j, c: (bb, j, 0, go + c)),
                pl.BlockSpec((1, td), lambda bb, j, c: (0, c))]
    args = [ht, mats, yp, gp, skip.reshape(1, d).astype(_F32)]
    out_specs = pl.BlockSpec((None, jb, h1, td), lambda bb, j, c: (bb, j, 0, c))
    out_shape = jax.ShapeDtypeStruct((b, n2, h1, d), out_dtype)
    if next_mats is not None:
        in_specs.append(pl.BlockSpec((jb, r, h1), lambda bb, j, c: (j, 0, 0)))
        args.append(next_mats)
        out_specs = [out_specs, pl.BlockSpec((None, jb, r, td), lambda bb, j, c: (bb, j, 0, c))]
        out_shape = [out_shape, jax.ShapeDtypeStruct((b, n2, r, d), _BF16)]
    return pl.pallas_call(
        _stage_a_inv_body,
        grid=(b, n2 // jb, d // td),
        in_specs=in_specs,
        out_specs=out_specs,
        out_shape=out_shape,
        compiler_params=_cparams(("parallel", "parallel", "parallel")),
        name="hyena_stage_a_inv",
    )(*args)


def _filter_pos(shape, j0, n1, n2, seq_len):
    r = lax.broadcasted_iota(jnp.int32, shape, 0)
    i1 = jnp.bitwise_and(r, n1 - 1)
    j = j0 + lax.shift_right_logical(r, n1.bit_length() - 1)
    n = n2 * i1 + j
    return jnp.where(i1 < n1 // 2, n, 2 * seq_len - n)


def _filter_mlp_body(band_ref, w1_ref, b1_ref, w2_ref, b2_ref, w3_ref, b3_ref, fr_ref, o_ref,
                     *, jb, n1, n2, seq_len):
    hi = lax.Precision.HIGHEST
    rows = jb * n1
    p = _filter_pos((rows, 128), pl.program_id(0) * jb, n1, n2, seq_len).astype(_F32)
    lane = lax.broadcasted_iota(jnp.int32, (rows, 128), 1)
    ang = (2.0 * math.pi / seq_len) * p * band_ref[...]
    feat = jnp.where(lane == 0, p * (1.0 / (seq_len - 1)),
                     jnp.where(lane <= _HY_BANDS, jnp.cos(ang),
                               jnp.where(lane <= 2 * _HY_BANDS, -jnp.sin(ang), 0.0)))
    fr = fr_ref[...]
    z = jnp.sin(fr * (jnp.dot(feat, w1_ref[...], preferred_element_type=_F32, precision=hi) + b1_ref[...]))
    z = jnp.sin(fr * (jnp.dot(z, w2_ref[...], preferred_element_type=_F32, precision=hi) + b2_ref[...]))
    z = jnp.sin(fr * (jnp.dot(z, w3_ref[...], preferred_element_type=_F32, precision=hi) + b3_ref[...]))
    o_ref[...] = z.reshape(jb, n1, _HY_WIDTH)


def _filter_mlp(seq_len, fc1_w, fc1_b, fc2_w, fc2_b, fc3_w, fc3_b, sin_freq):
    n1, n2 = _fft_plan(seq_len)
    jb = _row_tile(n2, 8)
    wd = _HY_WIDTH
    bands = jnp.linspace(1e-4, _HY_BANDS - 1, _HY_BANDS, dtype=_F32)
    band_row = jnp.zeros((1, 128), _F32).at[0, 1:1 + _HY_BANDS].set(bands).at[0, 1 + _HY_BANDS:_HY_EMB].set(bands)
    w1 = jnp.zeros((128, wd), _F32).at[:_HY_EMB].set(fc1_w.astype(_F32))
    const = lambda shape: pl.BlockSpec(shape, lambda i: (0,) * len(shape))
    row = lambda a: a.reshape(1, wd).astype(_F32)
    body = functools.partial(_filter_mlp_body, jb=jb, n1=n1, n2=n2, seq_len=seq_len)
    return pl.pallas_call(
        body,
        grid=(n2 // jb,),
        in_specs=[const((1, 128)), const((128, wd)), const((1, wd)), const((wd, wd)), const((1, wd)),
                  const((wd, wd)), const((1, wd)), const((1, wd))],
        out_specs=pl.BlockSpec((jb, n1, wd), lambda i: (i, 0, 0)),
        out_shape=jax.ShapeDtypeStruct((n2, n1, wd), _F32),
        compiler_params=_cparams(("parallel",)),
        name="hyena_filter_mlp",
    )(band_row, w1, row(fc1_b), fc2_w.astype(_F32), row(fc2_b), fc3_w.astype(_F32), row(fc3_b), row(sin_freq))


def _kernel_stage_a_body(z_ref, wf_ref, wb_ref, dl_ref, m_ref, o_ref, l1_ref, *, n1, n2, seq_len):
    j = pl.program_id(1)
    h1 = n1 // 2
    td = wf_ref.shape[1]
    p = _filter_pos((n1, td), j, n1, n2, seq_len)
    window = jnp.exp(-(p.astype(_F32) * (1.0 / (seq_len - 1))) * dl_ref[...])
    z = z_ref[...]
    zr = lax.broadcasted_iota(jnp.int32, (n1, 2 * _HY_WIDTH), 0)
    zc = lax.broadcasted_iota(jnp.int32, (n1, 2 * _HY_WIDTH), 1)
    zz = jnp.where((zr < h1) == (zc < _HY_WIDTH), jnp.concatenate([z, z], axis=1), 0.0)
    wcat = jnp.concatenate([wf_ref[...], wb_ref[...]], axis=0)
    kern = jnp.dot(zz.astype(_BF16), wcat.astype(_BF16), preferred_element_type=_F32) * window
    kern = jnp.where(p == seq_len, 0.0, kern)

    @pl.when(j == 0)
    def _():
        l1_ref[...] = jnp.zeros_like(l1_ref)

    l1_ref[...] += jnp.sum(jnp.abs(kern), axis=0, keepdims=True)
    o_ref[...] = jnp.dot(m_ref[...], kern.astype(_BF16), preferred_element_type=_F32).astype(o_ref.dtype)


def _kernel_stage_a(z, fc4_w, order, deltas, mats, seq_len, d):
    n1, n2 = _fft_plan(seq_len)
    r = mats.shape[1]
    td = _tile(d, _KERNEL_COLS)
    nd = d // td
    fwd_off, bwd_off = order * 2 * nd, order * 2 * nd + nd
    body = functools.partial(_kernel_stage_a_body, n1=n1, n2=n2, seq_len=seq_len)
    return pl.pallas_call(
        body,
        grid=(nd, n2),
        in_specs=[pl.BlockSpec((None, n1, _HY_WIDTH), lambda c, j: (j, 0, 0)),
                  pl.BlockSpec((_HY_WIDTH, td), lambda c, j: (0, fwd_off + c)),
                  pl.BlockSpec((_HY_WIDTH, td), lambda c, j: (0, bwd_off + c)),
                  pl.BlockSpec((1, td), lambda c, j: (0, c)),
                  pl.BlockSpec((None, r, n1), lambda c, j: (j, 0, 0))],
        out_specs=[pl.BlockSpec((None, r, td), lambda c, j: (j, 0, c)),
                   pl.BlockSpec((1, td), lambda c, j: (0, c))],
        out_shape=[jax.ShapeDtypeStruct((n2, r, d), _BF16), jax.ShapeDtypeStruct((1, d), _F32)],
        compiler_params=_cparams(("parallel", "arbitrary")),
        name="hyena_kernel_stage_a",
    )(z, fc4_w.astype(_F32), fc4_w.astype(_F32), deltas, mats)


def _hyena_long_conv(up, seq_len, d, hp):
    n1, n2 = _fft_plan(seq_len)
    kp = _kept_k1(n1)
    tabs = _dft_tables(seq_len)
    deltas = jnp.abs(jnp.linspace(_HY_MIN_DECAY, _HY_MAX_DECAY, d, dtype=_F32)).reshape(1, d)
    z = _filter_mlp(seq_len, hp["fc1_w"], hp["fc1_b"], hp["fc2_w"], hp["fc2_b"], hp["fc3_w"], hp["fc3_b"],
                    hp["sin_freq"])
    y, y_off = up, 0
    g = _stage_a_fwd(y, y_off, tabs["fwd_a_half"], d)
    for order in range(_HY_ORDER):
        kg, l1 = _kernel_stage_a(z, hp["fc4_w"], order, deltas, tabs["fwd_a_full"], seq_len, d)
        h = _stage_b_conv(_swap_stage_layout(g, n2, kp), _swap_stage_layout(kg[None], n2, kp)[0], l1,
                          tabs["fwd_b"], tabs["inv_b"])
        ht = _swap_stage_layout(h, kp, n2)
        if order == _HY_ORDER - 1:
            y = _stage_a_inv(ht, tabs["inv_a"], y, y_off, up, (order + 1) * d, hp["skip"][order], _BF16)
        else:
            y, g = _stage_a_inv(ht, tabs["inv_a"], y, y_off, up, (order + 1) * d, hp["skip"][order], _F32,
                                next_mats=tabs["fwd_a_half"])
        y_off = 0
    return y.reshape(-1, d)


def kernel(x_prompt, x_sample, norm_mix, norm_ffn, norm_final, hg_w_in, hg_lb_logits, hg_out_norm, hg_w_out,
           hy_w_in, hy_conv_w, hy_conv_b, hy_fc1_w, hy_fc1_b, hy_fc2_w, hy_fc2_b, hy_fc3_w, hy_fc3_b, hy_fc4_w,
           hy_sin_freq, hy_skip, hy_w_out, ffn_w_gate, ffn_w_up, ffn_w_down):
    b1, t1, d = x_prompt.shape
    b2, t2, _ = x_sample.shape
    groups = ((b1, t1), (b2, t2))
    group_x = (x_prompt.reshape(b1 * t1, d), x_sample.reshape(b2 * t2, d))
    m_total = b1 * t1 + b2 * t2
    depth = norm_mix.shape[0]

    x = None
    for layer in range(depth):
        slot = layer // 2
        if layer % 2 == 1:
            hp = dict(fc1_w=hy_fc1_w[slot], fc1_b=hy_fc1_b[slot], fc2_w=hy_fc2_w[slot], fc2_b=hy_fc2_b[slot],
                      fc3_w=hy_fc3_w[slot], fc3_b=hy_fc3_b[slot], fc4_w=hy_fc4_w[slot],
                      sin_freq=hy_sin_freq[slot], skip=hy_skip[slot])
        row_starts = [0, groups[0][0] * groups[0][1]]
        hs = [(_rmsnorm(xg, norm_mix[layer], _BF16) if x is None
               else _rmsnorm(x, norm_mix[layer], _BF16, r0, bg * tg))
              for (bg, tg), xg, r0 in zip(groups, group_x, row_starts)]
        ys = []
        for (bg, tg), h in zip(groups, hs):
            if layer % 2 == 0:
                proj = _matmul(h, (hg_w_in,), slot, mode="plain", rows_resident=True)
                o_f = _hgrn_scan(proj, hg_lb_logits[0], (tg,) * bg, reverse=False, slot=slot)
                ys.append(_hgrn_scan(proj, hg_lb_logits[1], (tg,) * bg, reverse=True, slot=slot, o_fwd=o_f,
                                     gain=hg_out_norm[slot]))
            else:
                up = _inproj_conv3(_permute_rows(h, ((bg, tg),)), hy_w_in, slot, hy_conv_w[slot], hy_conv_b[slot],
                                   bg, tg)
                ys.append(_permute_rows(_hyena_long_conv(up, tg, d, hp), ((bg, tg),), inverse=True))
        w_out = hg_w_out if layer % 2 == 0 else hy_w_out
        for y, xg, r0 in zip(ys, group_x, row_starts):
            if layer == 0:
                x = _matmul(y, (w_out,), slot, mode="residual", residual=xg, out_into=(m_total, r0, x))
            else:
                x = _matmul(y, (w_out,), slot, mode="residual", residual=x, res_row0=r0)
        h2 = _rmsnorm(x, norm_ffn[layer], _BF16)
        a = _matmul(h2, (ffn_w_gate, ffn_w_up), layer, mode="swiglu", out_dtype=_BF16, rows_resident=True)
        x = _matmul(a, (ffn_w_down,), layer, mode="residual", residual=x, tm=_MM_ROWS_WIDE_K, single_buffer_w=True)

    return (_rmsnorm(x, norm_final, _F32, 0, b1 * t1).reshape(b1, t1, d),
            _rmsnorm(x, norm_final, _F32, b1 * t1, b2 * t2).reshape(b2, t2, d))
```

```python
import functools
import math

import numpy as np
import jax
import jax.numpy as jnp
from jax import lax
from jax.experimental import pallas as pl
from jax.experimental.pallas import tpu as pltpu

_F32 = jnp.float32
_BF16 = jnp.bfloat16

_EPS = 1e-6
_HEAD_DIM = 128
_CHUNK = 64
_SUB = 16
_EXP_CLAMP = 115.0
_HY_ORDER = 2
_HY_EMB = 33
_HY_BANDS = (_HY_EMB - 1) // 2
_HY_WIDTH = 64
_HY_MAX_DECAY = math.log(1e-2) / 0.3
_HY_MIN_DECAY = math.log(1e-2) / 1.5

_V7X_VMEM_LIMIT = 56 * 1024 * 1024
_NORM_ROWS = 512
_MM_ROWS = 1024
_MM_ROWS_RESIDENT = 3072
_MM_COLS = 512
_MM_ROWS_WIDE_K = 384
_SCAN_HEADS = 32
_SCAN_ROWS = 128
_DFT_COLS = 4096
_DFT_GROUPS_PER_STEP = 2
_KERNEL_COLS = 2048


def _cparams(sem):
    return pltpu.CompilerParams(dimension_semantics=sem, vmem_limit_bytes=_V7X_VMEM_LIMIT)


def _tile(dim, want):
    if dim <= want:
        return dim
    t = want
    while t >= 128:
        if dim % t == 0:
            return t
        t -= 128
    return dim


def _row_tile(rows, want):
    if rows <= want:
        return rows
    for step in (8, 1):
        t = want - want % step
        while t >= step:
            if rows % t == 0:
                return t
            t -= step
    return rows


def _rmsnorm_body(x_ref, g_ref, o_ref):
    x = x_ref[...]
    ms = jnp.mean(x * x, axis=-1, keepdims=True)
    o_ref[...] = (x * lax.rsqrt(ms + _EPS) * g_ref[...]).astype(o_ref.dtype)


def _rmsnorm(x, gain, out_dtype, row0=0, rows=None):
    d = x.shape[1]
    m = x.shape[0] if rows is None else rows
    tm = _row_tile(math.gcd(m, row0) if row0 else m, _NORM_ROWS)
    blk0 = row0 // tm
    return pl.pallas_call(
        _rmsnorm_body,
        grid=(m // tm,),
        in_specs=[pl.BlockSpec((tm, d), lambda i: (blk0 + i, 0)),
                  pl.BlockSpec((1, d), lambda i: (0, 0))],
        out_specs=pl.BlockSpec((tm, d), lambda i: (i, 0)),
        out_shape=jax.ShapeDtypeStruct((m, d), out_dtype),
        compiler_params=_cparams(("parallel",)),
        name="rmsnorm",
    )(x, gain.reshape(1, d).astype(_F32))


def _mm_body(*refs, mode, n_w, rows_resident, n_in):
    a_ref, w_refs, o_ref = refs[0], refs[1:1 + n_w], refs[n_in]
    if rows_resident:
        ws = [w_ref[...].astype(_BF16) for w_ref in w_refs]
    else:
        wb_refs = refs[len(refs) - n_w:]

        @pl.when(pl.program_id(1) == 0)
        def _():
            for w_ref, wb_ref in zip(w_refs, wb_refs):
                wb_ref[...] = w_ref[...].astype(_BF16)

        ws = [wb_ref[...] for wb_ref in wb_refs]

    a = a_ref[...]
    if mode == "swiglu":
        g = jnp.dot(a, ws[0], preferred_element_type=_F32)
        u = jnp.dot(a, ws[1], preferred_element_type=_F32)
        o_ref[...] = (g * (0.5 * jnp.tanh(0.5 * g) + 0.5) * u).astype(o_ref.dtype)
    elif mode == "residual":
        o_ref[...] = refs[1 + n_w][...] + jnp.dot(a, ws[0], preferred_element_type=_F32)
    else:
        o_ref[...] = jnp.dot(a, ws[0], preferred_element_type=_F32).astype(o_ref.dtype)


def _matmul(a, ws, layer, *, mode, residual=None, res_row0=0, out_into=None, out_dtype=_F32, tm=None,
            rows_resident=False, single_buffer_w=False):
    m, k = a.shape
    n = ws[0].shape[2]
    row_offsets = [r for r in (res_row0, out_into[1] if out_into else 0) if r]
    tm = _row_tile(math.gcd(m, *row_offsets), tm or (_MM_ROWS_RESIDENT if rows_resident else _MM_ROWS))
    tn = _tile(n, _MM_COLS // len(ws))
    if rows_resident:
        grid = (m // tm, n // tn)
        ij = lambda g0, g1: (g0, g1)
        a_mode, w_mode = dict(pipeline_mode=pl.Buffered(1)), {}
        scratch = []
    else:
        grid = (n // tn, m // tm)
        ij = lambda g0, g1: (g1, g0)
        a_mode, w_mode = {}, (dict(pipeline_mode=pl.Buffered(1)) if single_buffer_w else {})
        scratch = [pltpu.VMEM((k, tn), _BF16) for _ in ws]
    in_specs = [pl.BlockSpec((tm, k), lambda g0, g1: (ij(g0, g1)[0], 0), **a_mode)]
    in_specs += [pl.BlockSpec((None, k, tn), lambda g0, g1: (layer, 0, ij(g0, g1)[1]), **w_mode) for _ in ws]
    args = [a, *[w.astype(_F32) for w in ws]]
    out_spec = pl.BlockSpec((tm, tn), lambda g0, g1: ij(g0, g1))
    out_shape = jax.ShapeDtypeStruct((m, n), out_dtype)
    aliases = {}
    if mode == "residual":
        def offset_spec(row0):
            blk0 = row0 // tm
            return pl.BlockSpec((tm, tn), lambda g0, g1: (blk0 + ij(g0, g1)[0], ij(g0, g1)[1]))

        in_specs.append(offset_spec(res_row0))
        args.append(residual)
        if out_into is None:
            out_spec = offset_spec(res_row0)
            out_shape = jax.ShapeDtypeStruct(residual.shape, residual.dtype)
            aliases = {len(args) - 1: 0}
        else:
            total_rows, row0, buffer = out_into
            out_spec = offset_spec(row0)
            out_shape = jax.ShapeDtypeStruct((total_rows, n), out_dtype)
            if buffer is not None:
                in_specs.append(pl.BlockSpec(memory_space=pl.ANY))
                args.append(buffer)
                aliases = {len(args) - 1: 0}
    return pl.pallas_call(
        functools.partial(_mm_body, mode=mode, n_w=len(ws), rows_resident=rows_resident,
                          n_in=len(args)),
        grid=grid,
        in_specs=in_specs,
        out_specs=out_spec,
        out_shape=out_shape,
        scratch_shapes=scratch,
        input_output_aliases=aliases,
        compiler_params=_cparams(("parallel", "arbitrary")),
        name="matmul_" + mode,
    )(*args)


def _scan_body(*refs, reverse, n_chunks, hb, reset_blocks, n_tblocks, slot, epilogue):
    if epilogue:
        q_ref, fl_ref, v_ref, lbl_ref, of_ref, gate_ref, gain_ref, o_ref, st_ref = refs
    else:
        q_ref, fl_ref, v_ref, lbl_ref, o_ref, st_ref = refs
    c_len, hd = _CHUNK, _HEAD_DIM

    t = pl.program_id(1)
    blk = (n_tblocks - 1 - t) if reverse else t
    reset = blk == reset_blocks[0]
    for r in reset_blocks[1:]:
        reset = jnp.logical_or(reset, blk == r)

    @pl.when(reset)
    def _():
        st_ref[...] = jnp.zeros_like(st_ref)

    lbl = lbl_ref[...]
    e = jnp.exp(lbl - jnp.max(lbl, axis=0, keepdims=True))
    lb_all = jnp.sum(e[: slot + 1], axis=0, keepdims=True) / jnp.sum(e, axis=0, keepdims=True)

    row = lax.broadcasted_iota(jnp.int32, (c_len, c_len), 0)
    col = lax.broadcasted_iota(jnp.int32, (c_len, c_len), 1)
    keep = (row <= col) if reverse else (row >= col)
    tri = keep.astype(_BF16)
    n_sub = c_len // _SUB
    w = hb * hd
    nt = (((1,), (1,)), ((), ()))
    tn = (((0,), (0,)), ((), ()))
    heads = [slice(h * hd, (h + 1) * hd) for h in range(hb)]

    def sigmoid(z):
        return 0.5 * jnp.tanh(0.5 * z) + 0.5

    def pad_rows(part, lo_r, dtype):
        pads = [jnp.zeros((lo_r, part.shape[1]), dtype), part,
                jnp.zeros((c_len - lo_r - part.shape[0], part.shape[1]), dtype)]
        return jnp.concatenate([p for p in pads if p.shape[0]], axis=0)

    def chunk_step(ci, carry):
        c = (n_chunks - 1 - ci) if reverse else ci
        r0 = pl.multiple_of(c * c_len, c_len)
        qr = q_ref[pl.ds(r0, c_len), :]
        q = qr * sigmoid(qr)
        f = lb_all + (1.0 - lb_all) * sigmoid(fl_ref[pl.ds(r0, c_len), :])
        k = 1.0 - f
        vb = v_ref[pl.ds(r0, c_len), :].astype(_BF16)
        lf = jnp.log2(f)
        hi = lf.astype(_BF16)
        rem = lf - hi.astype(_F32)
        mid = rem.astype(_BF16)
        lo = (rem - mid.astype(_F32)).astype(_BF16)
        cum = (jnp.dot(tri, hi, preferred_element_type=_F32) + jnp.dot(tri, mid, preferred_element_type=_F32)
               + jnp.dot(tri, lo, preferred_element_type=_F32))
        edge = cum[0:1] if reverse else cum[c_len - 1:c_len]
        k_end = (k * jnp.exp2(edge - cum)).astype(_BF16)
        st_decay = jnp.exp2(edge)
        refs = []
        for i in range(n_sub):
            if reverse:
                refs.append(cum[(i + 1) * _SUB:(i + 1) * _SUB + 1] if i < n_sub - 1 else jnp.zeros((1, w), _F32))
            else:
                refs.append(cum[i * _SUB - 1:i * _SUB] if i > 0 else jnp.zeros((1, w), _F32))
        ref_rows = jnp.concatenate([jnp.broadcast_to(r, (_SUB, w)) for r in refs], axis=0)
        a_f32 = q * jnp.exp2(cum - ref_rows)
        a = a_f32.astype(_BF16)
        ref_decay = jnp.concatenate([jnp.broadcast_to(jnp.exp2(r), (_SUB, w)) for r in refs], axis=0)
        q_in = (a_f32 * ref_decay).astype(_BF16)
        bs = []
        for i in range(n_sub):
            lo_r, hi_r = (i * _SUB, c_len) if reverse else (0, (i + 1) * _SUB)
            part = k[lo_r:hi_r] * jnp.exp2(jnp.minimum(refs[i] - cum[lo_r:hi_r], _EXP_CLAMP))
            bs.append(pad_rows(part.astype(_BF16), lo_r, _BF16))

        sts = [st_ref[h] for h in range(hb)]
        o_inter = [lax.dot_general(q_in[:, ln], sts[h].astype(_BF16), nt, preferred_element_type=_F32)
                   for h, ln in enumerate(heads)]
        a_cat = [jnp.concatenate([pad_rows(a[i * _SUB:(i + 1) * _SUB, ln], i * _SUB, _BF16) for i in range(n_sub)],
                                 axis=1) for ln in heads]
        b_cat = [jnp.concatenate([bs[i][:, ln] for i in range(n_sub)], axis=1) for ln in heads]
        scores = [lax.dot_general(a_cat[h], b_cat[h], nt, preferred_element_type=_F32) for h in range(hb)]
        scores = [jnp.where(keep, s, 0.0).astype(_BF16) for s in scores]
        o_intra = [jnp.dot(scores[h], vb[:, ln], preferred_element_type=_F32) for h, ln in enumerate(heads)]
        st_new = [lax.dot_general(vb[:, ln], k_end[:, ln], tn, preferred_element_type=_F32) for ln in heads]
        for h, ln in enumerate(heads):
            st_ref[h] = sts[h] * st_decay[:, ln] + st_new[h]
        o = jnp.concatenate([o_inter[h] + o_intra[h] for h in range(hb)], axis=1)
        if epilogue:
            tot = of_ref[pl.ds(r0, c_len), :] + o
            ys = []
            for ln in heads:
                th = tot[:, ln]
                ys.append(th * lax.rsqrt(jnp.mean(th * th, axis=-1, keepdims=True) + _EPS))
            g = gate_ref[pl.ds(r0, c_len), :]
            y = jnp.concatenate(ys, axis=1) * gain_ref[...]
            o_ref[pl.ds(r0, c_len), :] = (y * (g * sigmoid(g))).astype(o_ref.dtype)
        else:
            o_ref[pl.ds(r0, c_len), :] = o
        return carry

    lax.fori_loop(0, n_chunks, chunk_step, 0)


def _hgrn_scan(proj, lb_logits, seg_lens, *, reverse, slot, o_fwd=None, gain=None):
    m, d5 = proj.shape
    d = d5 // 5
    hb = math.gcd(_SCAN_HEADS, d // _HEAD_DIM)
    w = hb * _HEAD_DIM
    nhb = d // w
    epilogue = o_fwd is not None
    tb = _row_tile(math.gcd(*seg_lens) if len(seg_lens) > 1 else seg_lens[0],
                   _SCAN_ROWS if epilogue else 2 * _SCAN_ROWS)
    n_t = m // tb
    bounds = np.cumsum([0] + list(seg_lens)) // tb
    reset_blocks = tuple(int(b) - 1 for b in bounds[1:]) if reverse else tuple(int(b) for b in bounds[:-1])

    def tmap(t):
        return (n_t - 1 - t) if reverse else t

    def stream(s):
        return pl.BlockSpec((tb, w), lambda h, t, s=s: (tmap(t), s * nhb + h))

    n_slots = lb_logits.shape[0]
    in_specs = [stream(0), stream(2 if reverse else 1), stream(3),
                pl.BlockSpec((n_slots, w), lambda h, t: (0, h))]
    args = [proj, proj, proj, lb_logits.astype(_F32)]
    if epilogue:
        in_specs += [pl.BlockSpec((tb, w), lambda h, t: (tmap(t), h)), stream(4),
                     pl.BlockSpec((1, w), lambda h, t: (0, h))]
        args += [o_fwd, proj, gain.reshape(1, d).astype(_F32)]
    body = functools.partial(_scan_body, reverse=reverse, n_chunks=tb // _CHUNK, hb=hb,
                             reset_blocks=reset_blocks, n_tblocks=n_t, slot=slot, epilogue=epilogue)
    return pl.pallas_call(
        body,
        grid=(nhb, n_t),
        in_specs=in_specs,
        out_specs=pl.BlockSpec((tb, w), lambda h, t: (tmap(t), h)),
        out_shape=jax.ShapeDtypeStruct((m, d), _BF16 if epilogue else _F32),
        scratch_shapes=[pltpu.VMEM((hb, _HEAD_DIM, _HEAD_DIM), _F32)],
        compiler_params=_cparams(("parallel", "arbitrary")),
        name="hgrn_scan_bwd" if reverse else "hgrn_scan_fwd",
    )(*args)


def _inproj_conv3_body(prev_ref, cur_ref, nxt_ref, w_ref, cw_ref, cb_ref, o_ref, *, h1, tiles_per_seq):
    i_loc = lax.rem(pl.program_id(0), tiles_per_seq)
    wb = w_ref[...].astype(_BF16)
    u = jnp.dot(cur_ref[...], wb, preferred_element_type=_F32)
    u_prev = jnp.dot(prev_ref[...], wb, preferred_element_type=_F32)
    u_next = jnp.dot(nxt_ref[...], wb, preferred_element_type=_F32)
    rows = lax.broadcasted_iota(jnp.int32, u_prev.shape, 0)
    u_prev = jnp.where(i_loc == 0, jnp.where(rows == 0, 0.0, pltpu.roll(u_prev, 1, 0)), u_prev)
    u_next = jnp.where(i_loc == tiles_per_seq - 1,
                       jnp.where(rows == h1 - 1, 0.0, pltpu.roll(u_next, h1 - 1, 0)), u_next)
    if u.shape[0] > h1:
        up = jnp.concatenate([u_prev, u[:-h1]], axis=0)
        dn = jnp.concatenate([u[h1:], u_next], axis=0)
    else:
        up, dn = u_prev, u_next
    cw = cw_ref[...]
    o_ref[...] = up * cw[0:1] + u * cw[1:2] + dn * cw[2:3] + cb_ref[...]


def _inproj_conv3(hg, w, layer, conv_w, conv_b, batch, seq_len):
    m, k = hg.shape
    n = w.shape[2]
    n1, n2 = _fft_plan(seq_len)
    h1 = n1 // 2
    jb = _row_tile(n2, max(1, _MM_ROWS_RESIDENT // h1))
    tm = jb * h1
    tps = n2 // jb
    tn = _tile(n, _MM_COLS)

    def seq_slab(i, s):
        return lax.div(i, tps) * n2 + s

    def prev_slab(i, j):
        il = lax.rem(i, tps)
        return (seq_slab(i, jnp.where(il == 0, n2, il * jb) - 1), 0)

    def next_slab(i, j):
        il = lax.rem(i, tps)
        return (seq_slab(i, jnp.where(il == tps - 1, 0, (il + 1) * jb)), 0)

    body = functools.partial(_inproj_conv3_body, h1=h1, tiles_per_seq=tps)
    out = pl.pallas_call(
        body,
        grid=(m // tm, n // tn),
        in_specs=[pl.BlockSpec((h1, k), prev_slab),
                  pl.BlockSpec((tm, k), lambda i, j: (i, 0), pipeline_mode=pl.Buffered(1)),
                  pl.BlockSpec((h1, k), next_slab),
                  pl.BlockSpec((None, k, tn), lambda i, j: (layer, 0, j)),
                  pl.BlockSpec((3, tn), lambda i, j: (0, j)),
                  pl.BlockSpec((1, tn), lambda i, j: (0, j))],
        out_specs=pl.BlockSpec((tm, tn), lambda i, j: (i, j)),
        out_shape=jax.ShapeDtypeStruct((m, n), _F32),
        compiler_params=_cparams(("parallel", "arbitrary")),
        name="hyena_inproj_conv3",
    )(hg, hg, hg, w.astype(_F32), conv_w.astype(_F32), conv_b.reshape(1, n).astype(_F32))
    return out.reshape(batch, n2, h1, n)


def _permute_rows(x, groups, inverse=False):
    parts, row0 = [], 0
    for batch, seq_len in groups:
        n1, n2 = _fft_plan(seq_len)
        shape = (batch, n2, n1 // 2, -1) if inverse else (batch, n1 // 2, n2, -1)
        rows = batch * seq_len
        parts.append(x[row0:row0 + rows].reshape(shape).transpose(0, 2, 1, 3).reshape(rows, -1))
        row0 += rows
    return jnp.concatenate(parts, axis=0)


def _fft_plan(seq_len):
    n = 2 * seq_len
    n1 = 256 if n >= 32768 else 128
    while n // n1 < 8:
        n1 //= 2
    return n1, n // n1


def _kept_k1(n1):
    return n1 // 2 + 8


def _dft_tables(seq_len):
    n = 2 * seq_len
    n1, n2 = _fft_plan(seq_len)
    h1 = n1 // 2
    kp = _kept_k1(n1)
    i2 = lax.broadcasted_iota(jnp.int32, (n2, kp, n1), 0)
    k1 = lax.broadcasted_iota(jnp.int32, (n2, kp, n1), 1)
    i1 = lax.broadcasted_iota(jnp.int32, (n2, kp, n1), 2)
    phase = (i2 * k1 + n2 * (i1 * k1)) % n
    ang = phase.astype(_F32) * (2.0 * math.pi / n)
    cos = jnp.where(k1 <= h1, jnp.cos(ang), 0.0)
    sin = jnp.where(k1 <= h1, jnp.sin(ang), 0.0)
    fwd_a = jnp.concatenate([cos, -sin], axis=1).astype(_BF16)
    weight = jnp.where((k1 == 0) | (k1 == h1), 1.0, 2.0) * (1.0 / n)
    cos_t, sin_t = (cos * weight).transpose(0, 2, 1)[:, :h1], (sin * weight).transpose(0, 2, 1)[:, :h1]
    inv_a = jnp.concatenate([cos_t, -sin_t], axis=2).astype(_BF16)
    idx = np.arange(n2)
    ang_b = 2.0 * np.pi * ((idx[:, None] * idx[None, :]) % n2) / n2
    cb, sb = np.cos(ang_b), np.sin(ang_b)
    fwd_b = jnp.asarray(np.block([[cb, sb], [-sb, cb]]), _BF16)
    inv_b = jnp.asarray(np.block([[cb, -sb], [sb, cb]]), _BF16)
    fwd_a_half = fwd_a if h1 % 128 == 0 else fwd_a[:, :, :h1]
    return dict(fwd_a_full=fwd_a, fwd_a_half=fwd_a_half, inv_a=inv_a, fwd_b=fwd_b, inv_b=inv_b)


def _swap_stage_layout(x, groups, rows):
    b, _, _, d = x.shape
    return x.reshape(b, groups, 2, rows, d).transpose(0, 3, 2, 1, 4).reshape(b, rows, 2 * groups, d)


def _stage_a_fwd_body(x_ref, m_ref, o_ref):
    o_ref[...] = jnp.dot(m_ref[...], x_ref[...].astype(_BF16), preferred_element_type=_F32).astype(o_ref.dtype)


def _stage_a_fwd(xp, col_off, mats, d):
    b, n2, h1, _ = xp.shape
    r = mats.shape[1]
    td = _tile(d, _DFT_COLS)
    off = col_off // td
    return pl.pallas_call(
        _stage_a_fwd_body,
        grid=(b, n2, d // td),
        in_specs=[pl.BlockSpec((None, None, h1, td), lambda bb, j, c: (bb, j, 0, off + c)),
                  pl.BlockSpec((None, r, h1), lambda bb, j, c: (j, 0, 0))],
        out_specs=pl.BlockSpec((None, None, r, td), lambda bb, j, c: (bb, j, 0, c)),
        out_shape=jax.ShapeDtypeStruct((b, n2, r, d), _BF16),
        compiler_params=_cparams(("parallel", "parallel", "parallel")),
        name="hyena_stage_a_fwd",
    )(xp, mats)


def _stage_b_conv_body(g_ref, kg_ref, l1_ref, fb_ref, fbi_ref, o_ref, ks_ref):
    kb = g_ref.shape[0]
    n2 = g_ref.shape[1] // 2
    fb = fb_ref[...]

    @pl.when(pl.program_id(2) == 0)
    def _():
        inv_l1 = 1.0 / (l1_ref[...] + _EPS)
        for i in range(kb):
            ks_ref[i] = jnp.dot(fb, kg_ref[i], preferred_element_type=_F32) * inv_l1

    for i in range(kb):
        x = jnp.dot(fb, g_ref[i], preferred_element_type=_F32)
        xr, xi = x[:n2], x[n2:]
        kr, ki = ks_ref[i, :n2, :], ks_ref[i, n2:, :]
        y = jnp.concatenate([xr * kr - xi * ki, xr * ki + xi * kr], axis=0).astype(_BF16)
        o_ref[i] = jnp.dot(fbi_ref[...], y, preferred_element_type=_F32).astype(o_ref.dtype)


def _stage_b_conv(g, kg, l1, fwd_b, inv_b):
    b, n1, r, d = g.shape
    td = _tile(d, _DFT_COLS)
    kb = _row_tile(n1, _DFT_GROUPS_PER_STEP)
    return pl.pallas_call(
        _stage_b_conv_body,
        grid=(n1 // kb, d // td, b),
        in_specs=[pl.BlockSpec((None, kb, r, td), lambda k, c, bb: (bb, k, 0, c)),
                  pl.BlockSpec((kb, r, td), lambda k, c, bb: (k, 0, c)),
                  pl.BlockSpec((1, td), lambda k, c, bb: (0, c)),
                  pl.BlockSpec((r, r), lambda k, c, bb: (0, 0)),
                  pl.BlockSpec((r, r), lambda k, c, bb: (0, 0))],
        out_specs=pl.BlockSpec((None, kb, r, td), lambda k, c, bb: (bb, k, 0, c)),
        out_shape=jax.ShapeDtypeStruct((b, n1, r, d), _BF16),
        scratch_shapes=[pltpu.VMEM((kb, r, td), _F32)],
        compiler_params=_cparams(("parallel", "parallel", "arbitrary")),
        name="hyena_stage_b_conv",
    )(g, kg, l1, fwd_b, inv_b)


def _stage_a_inv_body(h_ref, m_ref, y_ref, g_ref, s_ref, *rest):
    for i in range(h_ref.shape[0]):
        conv = jnp.dot(m_ref[i], h_ref[i], preferred_element_type=_F32)
        y_new = g_ref[i] * (conv + y_ref[i] * s_ref[...])
        if len(rest) == 1:
            rest[0][i] = y_new.astype(rest[0].dtype)
        else:
            fwd_ref, o_ref, gt_ref = rest
            o_ref[i] = y_new
            gt_ref[i] = jnp.dot(fwd_ref[i], y_new.astype(_BF16), preferred_element_type=_F32).astype(gt_ref.dtype)


def _stage_a_inv(ht, mats, yp, y_off, gp, g_off, skip, out_dtype, next_mats=None):
    b, n2, r, d = ht.shape
    h1 = mats.shape[1]
    td = _tile(d, _DFT_COLS)
    yo, go = y_off // td, g_off // td
    jb = _row_tile(n2, _DFT_GROUPS_PER_STEP)
    in_specs = [pl.BlockSpec((None, jb, r, td), lambda bb, j, c: (bb, j, 0, c)),
                pl.BlockSpec((jb, h1, r), lambda bb, j, c: (j, 0, 0)),
                pl.BlockSpec((None, jb, h1, td), lambda bb, j, c: (bb, j, 0, yo + c)),
                pl.BlockSpec((None, jb, h1, td), lambda bb, j, c: (bb, j, 0, go + c)),
                pl.BlockSpec((1, td), lambda bb, j, c: (0, c))]
    args = [ht, mats, yp, gp, skip.reshape(1, d).astype(_F32)]
    out_specs = pl.BlockSpec((None, jb, h1, td), lambda bb, j, c: (bb, j, 0, c))
    out_shape = jax.ShapeDtypeStruct((b, n2, h1, d), out_dtype)
    if next_mats is not None:
        in_specs.append(pl.BlockSpec((jb, r, h1), lambda bb, j, c: (j, 0, 0)))
        args.append(next_mats)
        out_specs = [out_specs, pl.BlockSpec((None, jb, r, td), lambda bb, j, c: (bb, j, 0, c))]
        out_shape = [out_shape, jax.ShapeDtypeStruct((b, n2, r, d), _BF16)]
    return pl.pallas_call(
        _stage_a_inv_body,
        grid=(b, n2 // jb, d // td),
        in_specs=in_specs,
        out_specs=out_specs,
        out_shape=out_shape,
        compiler_params=_cparams(("parallel", "parallel", "parallel")),
        name="hyena_stage_a_inv",
    )(*args)


def _filter_pos(shape, j0, n1, n2, seq_len):
    r = lax.broadcasted_iota(jnp.int32, shape, 0)
    i1 = jnp.bitwise_and(r, n1 - 1)
    j = j0 + lax.shift_right_logical(r, n1.bit_length() - 1)
    n = n2 * i1 + j
    return jnp.where(i1 < n1 // 2, n, 2 * seq_len - n)


def _filter_mlp_body(band_ref, w1_ref, b1_ref, w2_ref, b2_ref, w3_ref, b3_ref, fr_ref, o_ref,
                     *, jb, n1, n2, seq_len):
    hi = lax.Precision.HIGHEST
    rows = jb * n1
    p = _filter_pos((rows, 128), pl.program_id(0) * jb, n1, n2, seq_len).astype(_F32)
    lane = lax.broadcasted_iota(jnp.int32, (rows, 128), 1)
    ang = (2.0 * math.pi / seq_len) * p * band_ref[...]
    feat = jnp.where(lane == 0, p * (1.0 / (seq_len - 1)),
                     jnp.where(lane <= _HY_BANDS, jnp.cos(ang),
                               jnp.where(lane <= 2 * _HY_BANDS, -jnp.sin(ang), 0.0)))
    fr = fr_ref[...]
    z = jnp.sin(fr * (jnp.dot(feat, w1_ref[...], preferred_element_type=_F32, precision=hi) + b1_ref[...]))
    z = jnp.sin(fr * (jnp.dot(z, w2_ref[...], preferred_element_type=_F32, precision=hi) + b2_ref[...]))
    z = jnp.sin(fr * (jnp.dot(z, w3_ref[...], preferred_element_type=_F32, precision=hi) + b3_ref[...]))
    o_ref[...] = z.reshape(jb, n1, _HY_WIDTH)


def _filter_mlp(seq_len, fc1_w, fc1_b, fc2_w, fc2_b, fc3_w, fc3_b, sin_freq):
    n1, n2 = _fft_plan(seq_len)
    jb = _row_tile(n2, 8)
    wd = _HY_WIDTH
    bands = jnp.linspace(1e-4, _HY_BANDS - 1, _HY_BANDS, dtype=_F32)
    band_row = jnp.zeros((1, 128), _F32).at[0, 1:1 + _HY_BANDS].set(bands).at[0, 1 + _HY_BANDS:_HY_EMB].set(bands)
    w1 = jnp.zeros((128, wd), _F32).at[:_HY_EMB].set(fc1_w.astype(_F32))
    const = lambda shape: pl.BlockSpec(shape, lambda i: (0,) * len(shape))
    row = lambda a: a.reshape(1, wd).astype(_F32)
    body = functools.partial(_filter_mlp_body, jb=jb, n1=n1, n2=n2, seq_len=seq_len)
    return pl.pallas_call(
        body,
        grid=(n2 // jb,),
        in_specs=[const((1, 128)), const((128, wd)), const((1, wd)), const((wd, wd)), const((1, wd)),
                  const((wd, wd)), const((1, wd)), const((1, wd))],
        out_specs=pl.BlockSpec((jb, n1, wd), lambda i: (i, 0, 0)),
        out_shape=jax.ShapeDtypeStruct((n2, n1, wd), _F32),
        compiler_params=_cparams(("parallel",)),
        name="hyena_filter_mlp",
    )(band_row, w1, row(fc1_b), fc2_w.astype(_F32), row(fc2_b), fc3_w.astype(_F32), row(fc3_b), row(sin_freq))


def _kernel_stage_a_body(z_ref, wf_ref, wb_ref, dl_ref, m_ref, o_ref, l1_ref, *, n1, n2, seq_len):
    j = pl.program_id(1)
    h1 = n1 // 2
    td = wf_ref.shape[1]
    p = _filter_pos((n1, td), j, n1, n2, seq_len)
    window = jnp.exp(-(p.astype(_F32) * (1.0 / (seq_len - 1))) * dl_ref[...])
    z = z_ref[...]
    zr = lax.broadcasted_iota(jnp.int32, (n1, 2 * _HY_WIDTH), 0)
    zc = lax.broadcasted_iota(jnp.int32, (n1, 2 * _HY_WIDTH), 1)
    zz = jnp.where((zr < h1) == (zc < _HY_WIDTH), jnp.concatenate([z, z], axis=1), 0.0)
    wcat = jnp.concatenate([wf_ref[...], wb_ref[...]], axis=0)
    kern = jnp.dot(zz.astype(_BF16), wcat.astype(_BF16), preferred_element_type=_F32) * window
    kern = jnp.where(p == seq_len, 0.0, kern)

    @pl.when(j == 0)
    def _():
        l1_ref[...] = jnp.zeros_like(l1_ref)

    l1_ref[...] += jnp.sum(jnp.abs(kern), axis=0, keepdims=True)
    o_ref[...] = jnp.dot(m_ref[...], kern.astype(_BF16), preferred_element_type=_F32).astype(o_ref.dtype)


def _kernel_stage_a(z, fc4_w, order, deltas, mats, seq_len, d):
    n1, n2 = _fft_plan(seq_len)
    r = mats.shape[1]
    td = _tile(d, _KERNEL_COLS)
    nd = d // td
    fwd_off, bwd_off = order * 2 * nd, order * 2 * nd + nd
    body = functools.partial(_kernel_stage_a_body, n1=n1, n2=n2, seq_len=seq_len)
    return pl.pallas_call(
        body,
        grid=(nd, n2),
        in_specs=[pl.BlockSpec((None, n1, _HY_WIDTH), lambda c, j: (j, 0, 0)),
                  pl.BlockSpec((_HY_WIDTH, td), lambda c, j: (0, fwd_off + c)),
                  pl.BlockSpec((_HY_WIDTH, td), lambda c, j: (0, bwd_off + c)),
                  pl.BlockSpec((1, td), lambda c, j: (0, c)),
                  pl.BlockSpec((None, r, n1), lambda c, j: (j, 0, 0))],
        out_specs=[pl.BlockSpec((None, r, td), lambda c, j: (j, 0, c)),
                   pl.BlockSpec((1, td), lambda c, j: (0, c))],
        out_shape=[jax.ShapeDtypeStruct((n2, r, d), _BF16), jax.ShapeDtypeStruct((1, d), _F32)],
        compiler_params=_cparams(("parallel", "arbitrary")),
        name="hyena_kernel_stage_a",
    )(z, fc4_w.astype(_F32), fc4_w.astype(_F32), deltas, mats)


def _hyena_long_conv(up, seq_len, d, hp):
    n1, n2 = _fft_plan(seq_len)
    kp = _kept_k1(n1)
    tabs = _dft_tables(seq_len)
    deltas = jnp.abs(jnp.linspace(_HY_MIN_DECAY, _HY_MAX_DECAY, d, dtype=_F32)).reshape(1, d)
    z = _filter_mlp(seq_len, hp["fc1_w"], hp["fc1_b"], hp["fc2_w"], hp["fc2_b"], hp["fc3_w"], hp["fc3_b"],
                    hp["sin_freq"])
    y, y_off = up, 0
    g = _stage_a_fwd(y, y_off, tabs["fwd_a_half"], d)
    for order in range(_HY_ORDER):
        kg, l1 = _kernel_stage_a(z, hp["fc4_w"], order, deltas, tabs["fwd_a_full"], seq_len, d)
        h = _stage_b_conv(_swap_stage_layout(g, n2, kp), _swap_stage_layout(kg[None], n2, kp)[0], l1,
                          tabs["fwd_b"], tabs["inv_b"])
        ht = _swap_stage_layout(h, kp, n2)
        if order == _HY_ORDER - 1:
            y = _stage_a_inv(ht, tabs["inv_a"], y, y_off, up, (order + 1) * d, hp["skip"][order], _BF16)
        else:
            y, g = _stage_a_inv(ht, tabs["inv_a"], y, y_off, up, (order + 1) * d, hp["skip"][order], _F32,
                                next_mats=tabs["fwd_a_half"])
        y_off = 0
    return y.reshape(-1, d)


def kernel(x_prompt, x_sample, norm_mix, norm_ffn, norm_final, hg_w_in, hg_lb_logits, hg_out_norm, hg_w_out,
           hy_w_in, hy_conv_w, hy_conv_b, hy_fc1_w, hy_fc1_b, hy_fc2_w, hy_fc2_b, hy_fc3_w, hy_fc3_b, hy_fc4_w,
           hy_sin_freq, hy_skip, hy_w_out, ffn_w_gate, ffn_w_up, ffn_w_down):
    b1, t1, d = x_prompt.shape
    b2, t2, _ = x_sample.shape
    groups = ((b1, t1), (b2, t2))
    group_x = (x_prompt.reshape(b1 * t1, d), x_sample.reshape(b2 * t2, d))
    m_total = b1 * t1 + b2 * t2
    depth = norm_mix.shape[0]

    x = None
    for layer in range(depth):
        slot = layer // 2
        if layer % 2 == 1:
            hp = dict(fc1_w=hy_fc1_w[slot], fc1_b=hy_fc1_b[slot], fc2_w=hy_fc2_w[slot], fc2_b=hy_fc2_b[slot],
                      fc3_w=hy_fc3_w[slot], fc3_b=hy_fc3_b[slot], fc4_w=hy_fc4_w[slot],
                      sin_freq=hy_sin_freq[slot], skip=hy_skip[slot])
        row_starts = [0, groups[0][0] * groups[0][1]]
        hs = [(_rmsnorm(xg, norm_mix[layer], _BF16) if x is None
               else _rmsnorm(x, norm_mix[layer], _BF16, r0, bg * tg))
              for (bg, tg), xg, r0 in zip(groups, group_x, row_starts)]
        ys = []
        for (bg, tg), h in zip(groups, hs):
            if layer % 2 == 0:
                proj = _matmul(h, (hg_w_in,), slot, mode="plain", rows_resident=True)
                o_f = _hgrn_scan(proj, hg_lb_logits[0], (tg,) * bg, reverse=False, slot=slot)
                ys.append(_hgrn_scan(proj, hg_lb_logits[1], (tg,) * bg, reverse=True, slot=slot, o_fwd=o_f,
                                     gain=hg_out_norm[slot]))
            else:
                up = _inproj_conv3(_permute_rows(h, ((bg, tg),)), hy_w_in, slot, hy_conv_w[slot], hy_conv_b[slot],
                                   bg, tg)
                ys.append(_permute_rows(_hyena_long_conv(up, tg, d, hp), ((bg, tg),), inverse=True))
        w_out = hg_w_out if layer % 2 == 0 else hy_w_out
        for y, xg, r0 in zip(ys, group_x, row_starts):
            if layer == 0:
                x = _matmul(y, (w_out,), slot, mode="residual", residual=xg, out_into=(m_total, r0, x))
            else:
                x = _matmul(y, (w_out,), slot, mode="residual", residual=x, res_row0=r0)
        h2 = _rmsnorm(x, norm_ffn[layer], _BF16)
        a = _matmul(h2, (ffn_w_gate, ffn_w_up), layer, mode="swiglu", out_dtype=_BF16, rows_resident=True)
        x = _matmul(a, (ffn_w_down,), layer, mode="residual", residual=x, tm=_MM_ROWS_WIDE_K, single_buffer_w=True)

    return (_rmsnorm(x, norm_final, _F32, 0, b1 * t1).reshape(b1, t1, d),
            _rmsnorm(x, norm_final, _F32, b1 * t1, b2 * t2).reshape(b2, t2, d))
```
